```python
import jax
import jax.numpy as jnp
from jax import lax
import numpy as np


D_MODEL = 2048
BATCH = 1
SEQ = 8192
DEPTH = 2

GRID_W = 64
CTX_LEN = 256
CHUNK = 64
EPS = 1e-6
GN_EPS = 64e-5
L2_EPS = 1e-12
D_FF = 5504
N_EVEN = (DEPTH + 1) // 2
N_ODD = DEPTH // 2

RET_HEADS = 8
RET_DK = 128
RET_DV = 128

RWKV_HEADS = 16
RWKV_N = 64
RWKV_W = RWKV_HEADS * RWKV_N
RWKV_DECAY_LORA = 64
RWKV_A_LORA = 64
RWKV_GATE_LORA = 160
RWKV_DIR_F = 3 * RWKV_W + RWKV_DECAY_LORA + RWKV_A_LORA

MLSTM_HEADS = 4
MLSTM_DK = 128
MLSTM_DV = 256

GDN_HEADS = 8
GDN_DK = 128
GDN_DV = 128
GDN_QKV = GDN_HEADS * (2 * GDN_DK + GDN_DV)
SHORT_CONV = 3

EVEN_COLS = (RET_HEADS * RET_DK, RET_HEADS * RET_DK, RET_HEADS * RET_DV, RET_HEADS * RET_DV,
             3 * RWKV_W, 2 * RWKV_DECAY_LORA, 2 * RWKV_A_LORA, RWKV_GATE_LORA)
ODD_COLS = (MLSTM_HEADS * MLSTM_DK, MLSTM_HEADS * MLSTM_DK, MLSTM_HEADS * MLSTM_DV, MLSTM_HEADS * MLSTM_DV,
            4 * MLSTM_HEADS, GDN_QKV, GDN_HEADS * GDN_DV, 2 * GDN_HEADS, 2 * GDN_HEADS)
F_EVEN = sum(EVEN_COLS)
F_ODD = sum(ODD_COLS)

kernel_name = 'hybrid_bidir_retention_rwkv7_mlstm_gdn_trunk'


def split_cols(p, sizes):
    return jnp.split(p, [int(s) for s in np.cumsum(sizes)[:-1]], axis=-1)


def rmsnorm(x, g):
    xf = x.astype(jnp.float32)
    y = xf * lax.rsqrt(jnp.mean(xf * xf, axis=-1, keepdims=True) + EPS)
    return (y * g.astype(jnp.float32)).astype(x.dtype)


def modulate(h, shift, scale):
    return h * (1.0 + scale) + shift


def head_rms(o):
    o = o * lax.rsqrt(jnp.mean(o * o, axis=-1, keepdims=True) + EPS)
    return o.reshape(o.shape[:2] + (-1,))


def head_groupnorm(y, g, b):
    mu = jnp.mean(y, axis=-1, keepdims=True)
    yc = y - mu
    yn = yc * lax.rsqrt(jnp.mean(yc * yc, axis=-1, keepdims=True) + GN_EPS)
    return yn.reshape(y.shape[:2] + (-1,)) * g + b


def l2norm(t):
    return t * lax.rsqrt(jnp.sum(t * t, axis=-1, keepdims=True) + L2_EPS)


def to_chunks(t):
    b, n, h = t.shape[:3]
    t = t.reshape((b, n // CHUNK, CHUNK, h) + t.shape[3:])
    return jnp.moveaxis(t, (1, 3), (0, 2))


def from_chunks(o):
    o = jnp.moveaxis(o, (0, 2), (1, 3))
    return o.reshape((o.shape[0], o.shape[1] * o.shape[2]) + o.shape[3:])


def flip_seq(tree):
    return jax.tree_util.tree_map(lambda a: jnp.flip(a, axis=1), tree)


def bidirectional(dir_fn, ctx_f, lat_f, ctx_b, lat_b, p_f, p_b, state0):
    oc_f, s_f = dir_fn(ctx_f, p_f, state0)
    ol_f, _ = dir_fn(lat_f, p_f, s_f)
    oc_b, s_b = dir_fn(flip_seq(ctx_b), p_b, state0)
    ol_b, _ = dir_fn(flip_seq(lat_b), p_b, s_b)
    add = lambda a, b: a + jnp.flip(b, axis=1)
    return jax.tree_util.tree_map(add, oc_f, oc_b), jax.tree_util.tree_map(add, ol_f, ol_b)


def retention_dir(inp, log_gamma, state0):
    q, k, v = inp
    pos = jnp.arange(CHUNK, dtype=jnp.float32)
    lg = log_gamma[:, None, None]
    diff = pos[:, None] - pos[None, :]
    d_intra = jnp.where(diff >= 0, jnp.exp(lg * jnp.maximum(diff, 0.0)), 0.0)
    d_query = jnp.exp(lg[..., 0] * (pos + 1.0))[..., None]
    d_key = jnp.exp(lg[..., 0] * (CHUNK - 1.0 - pos))[..., None]
    d_chunk = jnp.exp(lg * CHUNK)

    def step(r_state, xs):
        qb, kb, vb = xs
        scores = jnp.einsum('bhid,bhjd->bhij', qb, kb) * d_intra
        out = jnp.einsum('bhij,bhjv->bhiv', scores, vb) + jnp.einsum('bhid,bhdv->bhiv', qb * d_query, r_state)
        r_state = d_chunk * r_state + jnp.einsum('bhjd,bhjv->bhdv', kb * d_key, vb)
        return r_state, out

    r_fin, out = lax.scan(step, state0, (to_chunks(q), to_chunks(k), to_chunks(v)))
    return from_chunks(out), r_fin


def rwkv_dir(feats, p, state0):
    mu, w0, w_up, a0, a_up, k_k, k_a, r_k = p
    b, n, _ = feats.shape
    prev = jnp.pad(feats, ((0, 0), (1, 0), (0, 0)))[:, :-1]
    xs = feats + (prev - feats) * mu
    r, k, v, wd, ad = split_cols(xs, (RWKV_W, RWKV_W, RWKV_W, RWKV_DECAY_LORA, RWKV_A_LORA))
    w_log = -jax.nn.softplus(-(w0 + jnp.tanh(wd) @ w_up)) - 0.5
    decay = jnp.exp(-jnp.exp(w_log))
    a = jax.nn.sigmoid(a0 + ad @ a_up)
    heads = lambda t: t.reshape(b, n, RWKV_HEADS, RWKV_N)
    r, k, v, decay, a = heads(r), heads(k), heads(v), heads(decay), heads(a)
    kk = l2norm(k * k_k.reshape(RWKV_HEADS, RWKV_N))
    k = k * (1.0 + (a - 1.0) * k_a.reshape(RWKV_HEADS, RWKV_N))
    bonus = jnp.sum(r * k * r_k, axis=-1, keepdims=True) * v

    def step(s_state, xs_t):
        r_t, w_t, k_t, v_t, kk_t, b_t = xs_t
        sa = jnp.einsum('bhvk,bhk->bhv', s_state, kk_t)
        s_state = (s_state * w_t[:, :, None, :] - sa[..., None] * b_t[:, :, None, :]
                   + v_t[..., None] * k_t[:, :, None, :])
        return s_state, jnp.einsum('bhvk,bhk->bhv', s_state, r_t)

    tm = lambda t: jnp.moveaxis(t, 1, 0)
    s_fin, y = lax.scan(step, state0, (tm(r), tm(decay), tm(k), tm(v), tm(kk), tm(kk * a)))
    return (jnp.moveaxis(y, 0, 1), bonus), s_fin


def mlstm_dir(inp, gate_bias, state0):
    q, k, v, i_pre, f_pre = inp
    i_log = i_pre + gate_bias[0]
    f_log = jax.nn.log_sigmoid(f_pre + gate_bias[1])
    incl = jnp.tril(jnp.ones((CHUNK, CHUNK), dtype=bool))

    def step(carry, xs):
        c_mat, n_vec, m = carry
        qb, kb, vb, ib, fb = xs
        bcum = jnp.cumsum(fb, axis=-1)
        d_log = jnp.where(incl, bcum[..., :, None] - bcum[..., None, :] + ib[..., None, :], -jnp.inf)
        inter = bcum + m[..., None]
        m_t = jnp.maximum(jnp.max(d_log, axis=-1), inter)
        w = jnp.exp(d_log - m_t[..., None])
        s_inter = jnp.exp(inter - m_t)[..., None]
        qk = jnp.einsum('bhid,bhjd->bhij', qb, kb) * w
        num = jnp.einsum('bhij,bhjv->bhiv', qk, vb) + s_inter * jnp.einsum('bhid,bhvd->bhiv', qb, c_mat)
        den = jnp.sum(qk, axis=-1, keepdims=True) + s_inter * jnp.einsum('bhid,bhd->bhi', qb, n_vec)[..., None]
        h = num / jnp.maximum(jnp.abs(den), jnp.exp(-m_t)[..., None])
        b_end = bcum[..., -1]
        w_log = b_end[..., None] - bcum + ib
        m_new = jnp.maximum(b_end + m, jnp.max(w_log, axis=-1))
        w_end = jnp.exp(w_log - m_new[..., None])[..., None]
        carry_decay = jnp.exp(b_end + m - m_new)
        c_mat = carry_decay[..., None, None] * c_mat + jnp.einsum('bhjv,bhjd->bhvd', vb * w_end, kb)
        n_vec = carry_decay[..., None] * n_vec + jnp.sum(kb * w_end, axis=2)
        return (c_mat, n_vec, m_new), h

    state, h = lax.scan(step, state0, (to_chunks(q), to_chunks(k), to_chunks(v), to_chunks(i_log), to_chunks(f_log)))
    return from_chunks(h), state


def gdn_dir(inp, p, state0):
    q, k, v, a_pre, b_pre = inp
    a_log, dt_bias = p
    log_alpha = -jnp.exp(a_log) * jax.nn.softplus(a_pre + dt_bias)
    beta = jax.nn.sigmoid(b_pre)
    incl = jnp.tril(jnp.ones((CHUNK, CHUNK), dtype=bool))
    strict = jnp.tril(jnp.ones((CHUNK, CHUNK), dtype=bool), k=-1)

    def step(s_state, xs):
        qb, kb, vb, gb, bb = xs
        g_cum = jnp.cumsum(gb, axis=-1)
        decay = jnp.exp(jnp.where(incl, g_cum[..., :, None] - g_cum[..., None, :], -jnp.inf))
        kk = jnp.einsum('bhid,bhjd->bhij', kb, kb)
        lower = jnp.where(strict, bb[..., :, None] * kk * decay, 0.0)
        e_g = jnp.exp(g_cum)[..., None]
        rhs = bb[..., None] * (vb - e_g * jnp.einsum('bhid,bhvd->bhiv', kb, s_state))
        u = lax.linalg.triangular_solve(lower, rhs, left_side=True, lower=True, unit_diagonal=True)
        o = (e_g * jnp.einsum('bhid,bhvd->bhiv', qb, s_state)
             + jnp.einsum('bhij,bhjv->bhiv', jnp.einsum('bhid,bhjd->bhij', qb, kb) * decay, u))
        g_end = g_cum[..., -1]
        s_state = (jnp.exp(g_end)[..., None, None] * s_state
                   + jnp.einsum('bhjv,bhjd->bhvd', u, kb * jnp.exp(g_end[..., None] - g_cum)[..., None]))
        return s_state, o

    s_fin, o = lax.scan(step, state0, (to_chunks(q), to_chunks(k), to_chunks(v), to_chunks(log_alpha), to_chunks(beta)))
    return from_chunks(o), s_fin


def short_conv(x, w):
    xp = jnp.pad(x, ((0, 0), (1, 1), (0, 0)))
    return xp[:, :-2] * w[0] + xp[:, 1:-1] * w[1] + xp[:, 2:] * w[2]


def conv_glu(h, w_in, dw_w, dw_b, w_out, rows, width):
    b, n, _ = h.shape
    gate, up = jnp.split(h @ w_in, 2, axis=-1)
    gate = gate.reshape(b, rows, width, D_FF)
    gate = lax.conv_general_dilated(gate, dw_w.reshape(3, 3, 1, D_FF), window_strides=(1, 1), padding='SAME',
                                    dimension_numbers=('NHWC', 'HWIO', 'NHWC'), feature_group_count=D_FF)
    gate = gate.reshape(b, n, D_FF) + dw_b
    return (jax.nn.gelu(gate, approximate=False) * up) @ w_out


def even_mixer(hc, hl, w_in, ret_decay_logit, rwkv_mu, rwkv_w0, rwkv_w_up, rwkv_a0, rwkv_a_up,
               rwkv_k_k, rwkv_k_a, rwkv_r_k, rwkv_ln_g, rwkv_ln_b, rwkv_g_up):
    b = hl.shape[0]
    pc = split_cols((hc @ w_in).astype(jnp.float32), EVEN_COLS)
    pl = split_cols((hl @ w_in).astype(jnp.float32), EVEN_COLS)

    def ret_inputs(p):
        n = p[0].shape[1]
        q = p[0].reshape(b, n, RET_HEADS, RET_DK)
        k = p[1].reshape(b, n, RET_HEADS, RET_DK) * RET_DK ** -0.5
        v = p[2].reshape(b, n, RET_HEADS, RET_DV)
        return (q, k, v)

    rc_in, rl_in = ret_inputs(pc), ret_inputs(pl)
    log_gamma = jax.nn.log_sigmoid(ret_decay_logit.astype(jnp.float32))
    ret_state0 = jnp.zeros((b, RET_HEADS, RET_DK, RET_DV), jnp.float32)
    ret_c, ret_l = bidirectional(retention_dir, rc_in, rl_in, rc_in, rl_in, log_gamma[0], log_gamma[1], ret_state0)

    def rwkv_inputs(p, d):
        return jnp.concatenate([p[4], p[5][..., d * RWKV_DECAY_LORA:(d + 1) * RWKV_DECAY_LORA],
                                p[6][..., d * RWKV_A_LORA:(d + 1) * RWKV_A_LORA]], axis=-1)

    def rwkv_params(d):
        return (rwkv_mu[d], rwkv_w0[d], rwkv_w_up[d], rwkv_a0[d], rwkv_a_up[d], rwkv_k_k, rwkv_k_a, rwkv_r_k)

    rwkv_state0 = jnp.zeros((b, RWKV_HEADS, RWKV_N, RWKV_N), jnp.float32)
    rwkv_c, rwkv_l = bidirectional(rwkv_dir, rwkv_inputs(pc, 0), rwkv_inputs(pl, 0), rwkv_inputs(pc, 1),
                                   rwkv_inputs(pl, 1), rwkv_params(0), rwkv_params(1), rwkv_state0)

    def merge(p, ret_o, rwkv_o):
        y_ret = head_rms(ret_o) * jax.nn.silu(p[3])
        y_scan, bonus = rwkv_o
        y_rwkv = head_groupnorm(y_scan, rwkv_ln_g, rwkv_ln_b) + bonus.reshape(bonus.shape[:2] + (-1,))
        y_rwkv = y_rwkv * (jax.nn.sigmoid(p[7]) @ rwkv_g_up)
        return jnp.concatenate([y_ret, y_rwkv], axis=-1)

    return merge(pc, ret_c, rwkv_c), merge(pl, ret_l, rwkv_l)


def odd_mixer(hc, hl, w_in, mlstm_gate_b, mlstm_norm_g, gdn_conv, gdn_a_log, gdn_dt_bias, gdn_norm_g):
    b = hl.shape[0]
    pc = split_cols((hc @ w_in).astype(jnp.float32), ODD_COLS)
    pl = split_cols((hl @ w_in).astype(jnp.float32), ODD_COLS)

    def mlstm_inputs(p, d):
        n = p[0].shape[1]
        gates = p[4].reshape(b, n, 2, 2, MLSTM_HEADS)
        q = p[0].reshape(b, n, MLSTM_HEADS, MLSTM_DK) * MLSTM_DK ** -0.5
        k = p[1].reshape(b, n, MLSTM_HEADS, MLSTM_DK)
        v = p[2].reshape(b, n, MLSTM_HEADS, MLSTM_DV)
        return (q, k, v, gates[:, :, d, 0], gates[:, :, d, 1])

    mlstm_state0 = (jnp.zeros((b, MLSTM_HEADS, MLSTM_DV, MLSTM_DK), jnp.float32),
                    jnp.zeros((b, MLSTM_HEADS, MLSTM_DK), jnp.float32),
                    jnp.zeros((b, MLSTM_HEADS), jnp.float32))
    ml_c, ml_l = bidirectional(mlstm_dir, mlstm_inputs(pc, 0), mlstm_inputs(pl, 0), mlstm_inputs(pc, 1),
                               mlstm_inputs(pl, 1), mlstm_gate_b[0], mlstm_gate_b[1], mlstm_state0)

    def gdn_qkv(p):
        n = p[5].shape[1]
        qkv = jax.nn.silu(short_conv(p[5], gdn_conv))
        q, k, v = split_cols(qkv, (GDN_HEADS * GDN_DK, GDN_HEADS * GDN_DK, GDN_HEADS * GDN_DV))
        q = l2norm(q.reshape(b, n, GDN_HEADS, GDN_DK)) * GDN_DK ** -0.5
        k = l2norm(k.reshape(b, n, GDN_HEADS, GDN_DK))
        v = v.reshape(b, n, GDN_HEADS, GDN_DV)
        return (q, k, v)

    def gdn_inputs(p, qkv, d):
        n = p[7].shape[1]
        q, k, v = qkv
        a_pre = p[7].reshape(b, n, 2, GDN_HEADS)[:, :, d]
        b_pre = p[8].reshape(b, n, 2, GDN_HEADS)[:, :, d]
        return (q, k, v, a_pre, b_pre)

    qkv_c, qkv_l = gdn_qkv(pc), gdn_qkv(pl)
    gdn_state0 = jnp.zeros((b, GDN_HEADS, GDN_DV, GDN_DK), jnp.float32)
    gd_c, gd_l = bidirectional(gdn_dir, gdn_inputs(pc, qkv_c, 0), gdn_inputs(pl, qkv_l, 0),
                               gdn_inputs(pc, qkv_c, 1), gdn_inputs(pl, qkv_l, 1),
                               (gdn_a_log[0], gdn_dt_bias[0]), (gdn_a_log[1], gdn_dt_bias[1]), gdn_state0)

    def merge(p, ml_o, gd_o):
        y_ml = head_rms(ml_o) * mlstm_norm_g * jax.nn.sigmoid(p[3])
        y_gd = head_rms(gd_o) * gdn_norm_g * jax.nn.silu(p[6])
        return jnp.concatenate([y_ml, y_gd], axis=-1)

    return merge(pc, ml_c, gd_c), merge(pl, ml_l, gd_l)


def setup_inputs(seed: int = 0) -> dict:
    key = jax.random.key(seed)
    keys = list(jax.random.split(key, 48))

    def nrm(shape, scale):
        return scale * jax.random.normal(keys.pop(), shape, jnp.float32)

    def gain(shape):
        return 1.0 + nrm(shape, 0.02)

    D = D_MODEL
    ret_logit = jnp.log(2.0 ** (5.0 + jnp.arange(RET_HEADS, dtype=jnp.float32)) - 1.0)
    w0_base = jnp.linspace(-6.5, -1.5, RWKV_W, dtype=jnp.float32)
    f_base = jnp.linspace(3.0, 6.0, MLSTM_HEADS, dtype=jnp.float32)
    a_log = jnp.log(jax.random.uniform(keys.pop(), (N_ODD, 2, GDN_HEADS), jnp.float32, 1.0, 16.0))
    dt = jnp.exp(jax.random.uniform(keys.pop(), (N_ODD, 2, GDN_HEADS), jnp.float32, np.log(1e-3), np.log(1e-1)))
    dt_bias = dt + jnp.log(-jnp.expm1(-dt))
    mu = jax.random.uniform(keys.pop(), (N_EVEN, 2, RWKV_DIR_F), jnp.float32, 0.2, 0.8)
    return {
        'x': nrm((BATCH, SEQ, D), 1.0),
        'c': nrm((BATCH, D), 1.0),
        'ctx': nrm((BATCH, CTX_LEN, D), 1.0),
        'c_ctx': nrm((D,), 1.0),
        'ada_w': nrm((DEPTH, D, 6 * D), 0.5 * D ** -0.5),
        'ada_b': nrm((DEPTH, 6 * D), 0.02),
        'norm1_g': gain((DEPTH, D)),
        'norm2_g': gain((DEPTH, D)),
        'mix_w_out': nrm((DEPTH, D, D), D ** -0.5),
        'ffn_w_in': nrm((DEPTH, D, 2 * D_FF), D ** -0.5),
        'ffn_dw': nrm((DEPTH, 3, 3, D_FF), 1.0 / 3.0),
        'ffn_dw_b': nrm((DEPTH, D_FF), 0.02),
        'ffn_w_out': nrm((DEPTH, D_FF, D), D_FF ** -0.5),
        'final_norm_g': gain((D,)),
        'ev_w_in': nrm((N_EVEN, D, F_EVEN), D ** -0.5),
        'ret_decay_logit': ret_logit + nrm((N_EVEN, 2, RET_HEADS), 0.05),
        'rwkv_mu': mu,
        'rwkv_w0': w0_base + nrm((N_EVEN, 2, RWKV_W), 0.1),
        'rwkv_w_up': nrm((N_EVEN, 2, RWKV_DECAY_LORA, RWKV_W), 0.1),
        'rwkv_a0': nrm((N_EVEN, 2, RWKV_W), 0.1),
        'rwkv_a_up': nrm((N_EVEN, 2, RWKV_A_LORA, RWKV_W), 0.1),
        'rwkv_k_k': 0.85 + nrm((N_EVEN, RWKV_W), 0.02),
        'rwkv_k_a': 1.0 + nrm((N_EVEN, RWKV_W), 0.02),
        'rwkv_r_k': nrm((N_EVEN, RWKV_HEADS, RWKV_N), 0.1),
        'rwkv_ln_g': gain((N_EVEN, RWKV_W)),
        'rwkv_ln_b': nrm((N_EVEN, RWKV_W), 0.02),
        'rwkv_g_up': nrm((N_EVEN, RWKV_GATE_LORA, RWKV_W), RWKV_GATE_LORA ** -0.5),
        'od_w_in': nrm((N_ODD, D, F_ODD), D ** -0.5),
        'mlstm_gate_b': jnp.stack([nrm((N_ODD, 2, MLSTM_HEADS), 0.1),
                                   f_base + nrm((N_ODD, 2, MLSTM_HEADS), 0.1)], axis=2),
        'mlstm_norm_g': gain((N_ODD, MLSTM_HEADS * MLSTM_DV)),
        'gdn_conv': nrm((N_ODD, SHORT_CONV, GDN_QKV), SHORT_CONV ** -0.5),
        'gdn_a_log': a_log,
        'gdn_dt_bias': dt_bias,
        'gdn_norm_g': gain((N_ODD, GDN_HEADS * GDN_DV)),
    }


def reference(x, c, ctx, c_ctx, ada_w, ada_b, norm1_g, norm2_g, mix_w_out, ffn_w_in, ffn_dw, ffn_dw_b,
              ffn_w_out, final_norm_g, ev_w_in, ret_decay_logit, rwkv_mu, rwkv_w0, rwkv_w_up, rwkv_a0,
              rwkv_a_up, rwkv_k_k, rwkv_k_a, rwkv_r_k, rwkv_ln_g, rwkv_ln_b, rwkv_g_up, od_w_in,
              mlstm_gate_b, mlstm_norm_g, gdn_conv, gdn_a_log, gdn_dt_bias, gdn_norm_g):
    n_lat = x.shape[1]
    rows = n_lat // GRID_W
    xl, xc = x, ctx
    for l in range(DEPTH):
        last = l == DEPTH - 1
        mod_l = jax.nn.silu(c) @ ada_w[l] + ada_b[l]
        mod_c = jax.nn.silu(c_ctx) @ ada_w[l] + ada_b[l]
        sh1, sc1, g1, sh2, sc2, g2 = jnp.split(mod_l[:, None, :], 6, axis=-1)
        csh1, csc1, cg1, csh2, csc2, cg2 = jnp.split(mod_c, 6, axis=-1)
        hl = modulate(rmsnorm(xl, norm1_g[l]), sh1, sc1)
        hc = modulate(rmsnorm(xc, norm1_g[l]), csh1, csc1)
        if l % 2 == 0:
            e = l // 2
            yc, yl = even_mixer(hc, hl, ev_w_in[e], ret_decay_logit[e], rwkv_mu[e], rwkv_w0[e], rwkv_w_up[e],
                                rwkv_a0[e], rwkv_a_up[e], rwkv_k_k[e], rwkv_k_a[e], rwkv_r_k[e], rwkv_ln_g[e],
                                rwkv_ln_b[e], rwkv_g_up[e])
        else:
            o = l // 2
            yc, yl = odd_mixer(hc, hl, od_w_in[o], mlstm_gate_b[o], mlstm_norm_g[o], gdn_conv[o], gdn_a_log[o],
                               gdn_dt_bias[o], gdn_norm_g[o])
        xl = xl + g1 * (yl.astype(x.dtype) @ mix_w_out[l])
        hl = modulate(rmsnorm(xl, norm2_g[l]), sh2, sc2)
        xl = xl + g2 * conv_glu(hl, ffn_w_in[l], ffn_dw[l], ffn_dw_b[l], ffn_w_out[l], rows, GRID_W)
        if not last:
            xc = xc + cg1 * (yc.astype(x.dtype) @ mix_w_out[l])
            hc = modulate(rmsnorm(xc, norm2_g[l]), csh2, csc2)
            xc = xc + cg2 * conv_glu(hc, ffn_w_in[l], ffn_dw[l], ffn_dw_b[l], ffn_w_out[l], 1, xc.shape[1])
    return rmsnorm(xl, final_norm_g)
```

```python
import functools
import math

import jax
import jax.numpy as jnp
from jax import lax
from jax.experimental import pallas as pl
from jax.experimental.pallas import tpu as pltpu

F32 = jnp.float32
BF16 = jnp.bfloat16

D_MODEL = 2048
DEPTH = 2
GRID_W = 64
EPS = 1e-6
GN_EPS = 64e-5
L2_EPS = 1e-12
D_FF = 5504

RET_HEADS = 8
RET_DK = 128
RWKV_HEADS = 16
RWKV_N = 64
RWKV_W = RWKV_HEADS * RWKV_N
RWKV_LORA = 64
RWKV_GATE_LORA = 160
MLSTM_HEADS = 4
MLSTM_DK = 128
MLSTM_DV = 256
GDN_HEADS = 8
GDN_DK = 128
GDN_QKV = 3072

LANES = 128
SUBLANES = 8
VMEM_LIMIT = 56 * 1024 * 1024

F_PAD = 7680
FF_PAD = 5632
ROW_TILE = 768
RET_CHUNK = 256
MLSTM_CHUNK = 64
GDN_CHUNK = 64
RWKV_CHUNK = 64


def _cparams(n_axes):
    return pltpu.CompilerParams(dimension_semantics=("arbitrary",) * n_axes, vmem_limit_bytes=VMEM_LIMIT)


def _bdot(a, b):
    return jnp.dot(a.astype(BF16), b.astype(BF16), preferred_element_type=F32)


def _bdot_nt(a, b):
    return lax.dot_general(a.astype(BF16), b.astype(BF16), (((1,), (1,)), ((), ())), preferred_element_type=F32)


def _bdot_tn(a, b):
    return lax.dot_general(a.astype(BF16), b.astype(BF16), (((0,), (0,)), ((), ())), preferred_element_type=F32)


def _split3(x):
    x1 = x.astype(BF16)
    r1 = x - x1.astype(F32)
    x2 = r1.astype(BF16)
    x3 = (r1 - x2.astype(F32)).astype(BF16)
    return x1, x2, x3


def _dot_mask_lhs(m_bf16, x):
    x1, x2, x3 = _split3(x)
    d = lambda t: jnp.dot(m_bf16, t, preferred_element_type=F32)
    return (d(x3) + d(x2)) + d(x1)


def _dot3(a, b):
    a1 = a.astype(BF16)
    a2 = (a - a1.astype(F32)).astype(BF16)
    b1 = b.astype(BF16)
    b2 = (b - b1.astype(F32)).astype(BF16)
    d = lambda u, v: jnp.dot(u, v, preferred_element_type=F32)
    return (d(a2, b1) + d(a1, b2)) + d(a1, b1)


def _sigmoid(x):
    return 1.0 / (1.0 + jnp.exp(-x))


def _silu(x):
    return x * _sigmoid(x)


def _softplus(x):
    return jnp.maximum(x, 0.0) + jnp.log1p(jnp.exp(-jnp.abs(x)))


def _log_sigmoid(x):
    return -_softplus(-x)


def _order_masks(c, rev):
    i = lax.broadcasted_iota(jnp.int32, (c, c), 0)
    j = lax.broadcasted_iota(jnp.int32, (c, c), 1)
    if rev:
        return j >= i, j > i
    return j <= i, j < i


def _neumann_inverse(n, c, dotf):
    i = lax.broadcasted_iota(jnp.int32, (c, c), 0)
    j = lax.broadcasted_iota(jnp.int32, (c, c), 1)
    x = jnp.where(i == j, 1.0, 0.0).astype(F32) + n
    p = dotf(n, n)
    for _ in range(int(math.log2(c)) - 2):
        r = dotf(jnp.concatenate([x, p], axis=0), p)
        x = x + r[:c]
        p = r[c:]
    return x + dotf(x, p)


def _chunk_index(s, rev, n_ctx_chunks, n_chunks):
    if not rev:
        return s
    return jnp.where(s < n_ctx_chunks, n_ctx_chunks - 1 - s, n_chunks + n_ctx_chunks - 1 - s)


def _seg_sum(x, seg):
    c, w = x.shape
    if seg == LANES:
        parts = [jnp.broadcast_to(jnp.sum(x[:, b * LANES:(b + 1) * LANES], axis=-1, keepdims=True), (c, LANES))
                 for b in range(w // LANES)]
        return jnp.concatenate(parts, axis=-1)
    assert seg * 2 == LANES
    lane = lax.broadcasted_iota(jnp.int32, (c, LANES), 1)
    low = lane < seg
    parts = []
    for b in range(w // LANES):
        xb = x[:, b * LANES:(b + 1) * LANES]
        s_lo = jnp.sum(jnp.where(low, xb, 0.0), axis=-1, keepdims=True)
        s_hi = jnp.sum(jnp.where(low, 0.0, xb), axis=-1, keepdims=True)
        parts.append(jnp.where(low, s_lo, s_hi))
    return jnp.concatenate(parts, axis=-1)


def _ada_kernel(c_ref, w_ref, b_ref, o_ref):
    cv = c_ref[...]
    s = _silu(cv)
    w = w_ref[0]
    bias = b_ref[0]
    r0 = jnp.sum(w * s[:, 0:1], axis=0, keepdims=True) + bias
    r1 = jnp.sum(w * s[:, 1:2], axis=0, keepdims=True) + bias
    row = lax.broadcasted_iota(jnp.int32, (SUBLANES, w.shape[1]), 0)
    o_ref[0] = jnp.where(row == 0, r0, jnp.where(row == 1, r1, 0.0))


def _ada_call(cvecs, ada_w, ada_b):
    depth, d, n = ada_w.shape
    tn = 1024
    return pl.pallas_call(
        _ada_kernel,
        grid=(depth, n // tn),
        in_specs=[pl.BlockSpec((d, 2), lambda l, j: (0, 0)),
                  pl.BlockSpec((1, d, tn), lambda l, j: (l, 0, j)),
                  pl.BlockSpec((1, 1, tn), lambda l, j: (l, 0, j))],
        out_specs=pl.BlockSpec((1, SUBLANES, tn), lambda l, j: (l, 0, j)),
        out_shape=jax.ShapeDtypeStruct((depth, SUBLANES, n), F32),
        compiler_params=_cparams(2),
        name="ada_modulation",
    )(cvecs, ada_w, ada_b.reshape(depth, 1, n))


def _mod_rows(mod, k, rows, n_ctx):
    d = D_MODEL
    vc = mod[0:1, k * d:(k + 1) * d]
    vl = mod[1:2, k * d:(k + 1) * d]
    return jnp.where(rows < n_ctx, vc, vl)


def _norm_mod(x, g, shift, scale):
    y = x * lax.rsqrt(jnp.mean(x * x, axis=-1, keepdims=True) + EPS) * g
    return y * (1.0 + scale) + shift


def _proj_kernel(x_ref, g_ref, mod_ref, w_ref, o_ref, h_scr, *, tm, n_ctx):
    i = pl.program_id(0)

    @pl.when(pl.program_id(1) == 0)
    def _():
        rows = i * tm + lax.broadcasted_iota(jnp.int32, (tm, 1), 0)
        mod = mod_ref[0]
        h = _norm_mod(x_ref[...], g_ref[...], _mod_rows(mod, 0, rows, n_ctx), _mod_rows(mod, 1, rows, n_ctx))
        h_scr[...] = h.astype(BF16)

    o_ref[...] = jnp.dot(h_scr[...], w_ref[...], preferred_element_type=F32)


def _proj_call(x, g, mod, layer, w_bf16, n_ctx):
    t, d = x.shape
    f = w_bf16.shape[1]
    tm, tn = ROW_TILE, 768
    return pl.pallas_call(
        functools.partial(_proj_kernel, tm=tm, n_ctx=n_ctx),
        grid=(t // tm, f // tn),
        in_specs=[pl.BlockSpec((tm, d), lambda i, j: (i, 0)),
                  pl.BlockSpec((1, d), lambda i, j: (0, 0)),
                  pl.BlockSpec((1, SUBLANES, 6 * d), lambda i, j: (layer, 0, 0)),
                  pl.BlockSpec((d, tn), lambda i, j: (0, j))],
        out_specs=pl.BlockSpec((tm, tn), lambda i, j: (i, j)),
        out_shape=jax.ShapeDtypeStruct((t, f), F32),
        scratch_shapes=[pltpu.VMEM((tm, d), BF16)],
        compiler_params=_cparams(2),
        name="norm_mod_in_proj",
    )(x, g.reshape(1, d), mod, w_bf16)


def _outproj_kernel(ya_ref, yb_ref, wa_ref, wb_ref, x_ref, g_ref, mod_ref, xo_ref, h_ref, *, tm, n_ctx):
    i = pl.program_id(0)
    rows = i * tm + lax.broadcasted_iota(jnp.int32, (tm, 1), 0)
    mod = mod_ref[0]
    acc = jnp.dot(ya_ref[...], wa_ref[...], preferred_element_type=F32)
    acc = acc + jnp.dot(yb_ref[...], wb_ref[...], preferred_element_type=F32)
    xn = x_ref[...] + _mod_rows(mod, 2, rows, n_ctx) * acc
    xo_ref[...] = xn
    h = _norm_mod(xn, g_ref[...], _mod_rows(mod, 3, rows, n_ctx), _mod_rows(mod, 4, rows, n_ctx))
    h_ref[...] = h.astype(BF16)


def _outproj_call(ya, yb, wa, wb, x, g2, mod, layer, n_ctx):
    t, d = x.shape
    half = ya.shape[1]
    tm = 384
    return pl.pallas_call(
        functools.partial(_outproj_kernel, tm=tm, n_ctx=n_ctx),
        grid=(t // tm,),
        in_specs=[pl.BlockSpec((tm, half), lambda i: (i, 0)),
                  pl.BlockSpec((tm, half), lambda i: (i, 0)),
                  pl.BlockSpec((half, d), lambda i: (0, 0)),
                  pl.BlockSpec((half, d), lambda i: (0, 0)),
                  pl.BlockSpec((tm, d), lambda i: (i, 0)),
                  pl.BlockSpec((1, d), lambda i: (0, 0)),
                  pl.BlockSpec((1, SUBLANES, 6 * d), lambda i: (layer, 0, 0))],
        out_specs=[pl.BlockSpec((tm, d), lambda i: (i, 0)),
                   pl.BlockSpec((tm, d), lambda i: (i, 0))],
        out_shape=[jax.ShapeDtypeStruct((t, d), F32), jax.ShapeDtypeStruct((t, d), BF16)],
        compiler_params=_cparams(1),
        name="mix_out_proj_norm2",
    )(ya, yb, wa, wb, x, g2.reshape(1, d), mod)


def _ffn_in_kernel(h_ref, wg_ref, wu_ref, g_ref, u_ref):
    h = h_ref[...]
    g_ref[...] = jnp.dot(h, wg_ref[...], preferred_element_type=F32).astype(BF16)
    u_ref[...] = jnp.dot(h, wu_ref[...], preferred_element_type=F32).astype(BF16)


def _ffn_in_call(h, wg, wu):
    t, d = h.shape
    fp = wg.shape[1]
    tm, tn = ROW_TILE, 512
    spec_w = pl.BlockSpec((d, tn), lambda i, j: (0, j))
    spec_o = pl.BlockSpec((tm, tn), lambda i, j: (i, j))
    return pl.pallas_call(
        _ffn_in_kernel,
        grid=(t // tm, fp // tn),
        in_specs=[pl.BlockSpec((tm, d), lambda i, j: (i, 0)), spec_w, spec_w],
        out_specs=[spec_o, spec_o],
        out_shape=[jax.ShapeDtypeStruct((t, fp), BF16)] * 2,
        compiler_params=_cparams(2),
        name="ffn_in_proj",
    )(h, wg, wu)


def _ffn_out_kernel(gm_ref, gp_ref, gn_ref, u_ref, dw_ref, db_ref, wo_ref, x_ref, mod_ref, fg_ref, o_ref, acc_scr,
                    *, tm, tk, n_ctx, t_total, final_norm):
    i = pl.program_id(0)
    k = pl.program_id(1)
    w = GRID_W

    @pl.when(k == 0)
    def _():
        acc_scr[...] = jnp.zeros_like(acc_scr)

    g_all = jnp.concatenate([gp_ref[...], gm_ref[...], gn_ref[...]], axis=0).astype(F32)
    n_all = tm + 2 * w
    rows = (i * tm - w) + lax.broadcasted_iota(jnp.int32, (n_all, tk), 0)
    is_ctx = rows < n_ctx
    col = jnp.bitwise_and(rows, w - 1)
    left_ok = jnp.where(is_ctx, rows, col) > 0
    right_ok = jnp.where(is_ctx, rows - (n_ctx - 1), col - (w - 1)) < 0
    g_left = jnp.where(left_ok, pltpu.roll(g_all, 1, axis=0), 0.0)
    g_right = jnp.where(right_ok, pltpu.roll(g_all, n_all - 1, axis=0), 0.0)
    dw = dw_ref[...]

    def taps(kh, lo):
        sl = slice(lo, lo + tm)
        return (g_left[sl] * dw[3 * kh:3 * kh + 1] + g_all[sl] * dw[3 * kh + 1:3 * kh + 2]
                + g_right[sl] * dw[3 * kh + 2:3 * kh + 3])

    rows_m = i * tm + lax.broadcasted_iota(jnp.int32, (tm, tk), 0)
    up_ok = rows_m >= n_ctx + w
    down_ok = jnp.where(rows_m >= n_ctx, rows_m, t_total) < t_total - w
    conv = taps(1, w) + jnp.where(up_ok, taps(0, 0), 0.0) + jnp.where(down_ok, taps(2, 2 * w), 0.0)
    gate = conv + db_ref[...]
    act = 0.5 * gate * (1.0 + lax.erf(gate * (2.0 ** -0.5)))
    act = act * u_ref[...].astype(F32)
    acc_scr[...] += jnp.dot(act.astype(BF16), wo_ref[...], preferred_element_type=F32)

    @pl.when(k == pl.num_programs(1) - 1)
    def _():
        rws = i * tm + lax.broadcasted_iota(jnp.int32, (tm, 1), 0)
        xn = x_ref[...] + _mod_rows(mod_ref[0], 5, rws, n_ctx) * acc_scr[...]
        if final_norm:
            xn = xn * lax.rsqrt(jnp.mean(xn * xn, axis=-1, keepdims=True) + EPS) * fg_ref[...]
        o_ref[...] = xn


def _ffn_out_call(gate, up, dw9, dwb, wo, x, mod, layer, final_g, n_ctx, final_norm):
    t, d = x.shape
    fp = gate.shape[1]
    tm, tk, w = ROW_TILE, 512, GRID_W
    rpt = tm // w
    n_rows = t // w
    return pl.pallas_call(
        functools.partial(_ffn_out_kernel, tm=tm, tk=tk, n_ctx=n_ctx, t_total=t, final_norm=final_norm),
        grid=(t // tm, fp // tk),
        in_specs=[pl.BlockSpec((tm, tk), lambda i, k: (i, k)),
                  pl.BlockSpec((w, tk), lambda i, k: (jnp.maximum(i * rpt - 1, 0), k)),
                  pl.BlockSpec((w, tk), lambda i, k: (jnp.minimum((i + 1) * rpt, n_rows - 1), k)),
                  pl.BlockSpec((tm, tk), lambda i, k: (i, k)),
                  pl.BlockSpec((9, tk), lambda i, k: (0, k)),
                  pl.BlockSpec((1, tk), lambda i, k: (0, k)),
                  pl.BlockSpec((tk, d), lambda i, k: (k, 0)),
                  pl.BlockSpec((tm, d), lambda i, k: (i, 0)),
                  pl.BlockSpec((1, SUBLANES, 6 * d), lambda i, k: (layer, 0, 0)),
                  pl.BlockSpec((1, d), lambda i, k: (0, 0))],
        out_specs=pl.BlockSpec((tm, d), lambda i, k: (i, 0)),
        out_shape=jax.ShapeDtypeStruct((t, d), F32),
        scratch_shapes=[pltpu.VMEM((tm, d), F32)],
        compiler_params=_cparams(2),
        name="ffn_conv_glu_out",
    )(gate, gate, gate, up, dw9, dwb, wo, x, mod, final_g.reshape(1, d))


def _retention_kernel(*refs, rev, final, c):
    if final:
        q_ref, k_ref, v_ref, lg_ref, gate_ref, ob_ref, o_ref, r_scr, di_scr, dq_scr, dk_scr, dc_scr = refs
    else:
        q_ref, k_ref, v_ref, lg_ref, o_ref, r_scr, di_scr, dq_scr, dk_scr, dc_scr = refs
    d = 1 if rev else 0
    dk = RET_DK

    @pl.when(pl.program_id(0) == 0)
    def _():
        r_scr[...] = jnp.zeros_like(r_scr)
        i = lax.broadcasted_iota(jnp.int32, (c, c), 0)
        j = lax.broadcasted_iota(jnp.int32, (c, c), 1)
        diff = ((j - i) if rev else (i - j)).astype(F32)
        row = lax.broadcasted_iota(jnp.int32, (c, dk), 0)
        pos = ((c - 1 - row) if rev else row).astype(F32)
        for h in range(RET_HEADS):
            lg = _log_sigmoid(lg_ref[d:d + 1, h:h + 1])
            di_scr[h] = jnp.where(diff >= 0, jnp.exp(lg * jnp.maximum(diff, 0.0)), 0.0)
            dq_scr[h] = jnp.exp(lg * (pos + 1.0))
            dk_scr[h] = jnp.exp(lg * (c - 1.0 - pos)) * (dk ** -0.5)
            dc_scr[h] = jnp.exp(jnp.broadcast_to(lg, (1, dk)) * c)

    for h in range(RET_HEADS):
        sl = slice(h * dk, (h + 1) * dk)
        qh = q_ref[:, sl]
        kh = k_ref[:, sl]
        vh = v_ref[:, sl].astype(BF16)
        r_state = r_scr[h]
        scores = _bdot_nt(qh, kh) * (di_scr[h] * (dk ** -0.5))
        out = _bdot(scores, vh) + _bdot(qh * dq_scr[h], r_state)
        r_scr[h] = dc_scr[h] * r_state + _bdot_tn(kh * dk_scr[h], vh)
        if final:
            tot = out + ob_ref[:, sl]
            y = tot * lax.rsqrt(jnp.mean(tot * tot, axis=-1, keepdims=True) + EPS) * _silu(gate_ref[:, sl])
            o_ref[:, sl] = y.astype(o_ref.dtype)
        else:
            o_ref[:, sl] = out


def _retention_call(p, logit, other, rev, n_ctx):
    t = p.shape[0]
    c = RET_CHUNK
    n, nc = t // c, n_ctx // c
    width = RET_HEADS * RET_DK
    final = other is not None
    cm = lambda s: _chunk_index(s, rev, nc, n)
    in_specs = [pl.BlockSpec((c, width), lambda s: (cm(s), 0)),
                pl.BlockSpec((c, width), lambda s: (cm(s), 1)),
                pl.BlockSpec((c, width), lambda s: (cm(s), 2)),
                pl.BlockSpec((2, RET_HEADS), lambda s: (0, 0))]
    args = [p, p, p, logit]
    if final:
        in_specs += [pl.BlockSpec((c, width), lambda s: (cm(s), 3)), pl.BlockSpec((c, width), lambda s: (cm(s), 0))]
        args += [p, other]
    return pl.pallas_call(
        functools.partial(_retention_kernel, rev=rev, final=final, c=c),
        grid=(n,),
        in_specs=in_specs,
        out_specs=pl.BlockSpec((c, width), lambda s: (cm(s), 0)),
        out_shape=jax.ShapeDtypeStruct((t, width), BF16 if final else F32),
        scratch_shapes=[pltpu.VMEM((RET_HEADS, RET_DK, RET_DK), F32),
                        pltpu.VMEM((RET_HEADS, c, c), F32),
                        pltpu.VMEM((RET_HEADS, c, RET_DK), F32),
                        pltpu.VMEM((RET_HEADS, c, RET_DK), F32),
                        pltpu.VMEM((RET_HEADS, 1, RET_DK), F32)],
        compiler_params=_cparams(1),
        name="retention_bwd" if rev else "retention_fwd_merge",
    )(*args)


def _rwkv_kernel(*refs, rev, final, c, nc):
    if final:
        (r_ref, k_ref, v_ref, wd_ref, ad_ref, mu_ref, mus_ref, vec_ref, wup_ref, aup_ref,
         gl_ref, gup_ref, yb_ref, bb_ref, o_ref, st_scr, carry_scr) = refs
    else:
        (r_ref, k_ref, v_ref, wd_ref, ad_ref, mu_ref, mus_ref, vec_ref, wup_ref, aup_ref,
         y_ref, bonus_ref, st_scr, carry_scr) = refs
    d = 1 if rev else 0
    s = pl.program_id(0)
    n = RWKV_N
    wdt = RWKV_W

    @pl.when(s == 0)
    def _():
        st_scr[...] = jnp.zeros_like(st_scr)

    @pl.when(jnp.logical_or(s == 0, s == nc))
    def _():
        carry_scr[...] = jnp.zeros_like(carry_scr)

    row = lax.broadcasted_iota(jnp.int32, (c, 1), 0)
    edge = (row == c - 1) if rev else (row == 0)
    keep = c - 1 if not rev else 0

    def shifted(x, lo):
        width = x.shape[1]
        prev = pltpu.roll(x, (c - 1) if rev else 1, axis=0)
        prev = jnp.where(edge, carry_scr[0:1, lo:lo + width], prev)
        carry_scr[0:1, lo:lo + width] = x[keep:keep + 1, :]
        return prev

    def mix(x, lo, mu):
        prev = shifted(x, lo)
        return x + (prev - x) * mu

    r = mix(r_ref[...], 0, mu_ref[0:1, 0:wdt])
    k = mix(k_ref[...], wdt, mu_ref[0:1, wdt:2 * wdt])
    v = mix(v_ref[...], 2 * wdt, mu_ref[0:1, 2 * wdt:3 * wdt])
    wd = mix(wd_ref[...], 3 * wdt, mus_ref[0:1, :])[:, d * RWKV_LORA:(d + 1) * RWKV_LORA]
    ad = mix(ad_ref[...], 3 * wdt + LANES, mus_ref[1:2, :])[:, d * RWKV_LORA:(d + 1) * RWKV_LORA]

    w0, a0 = vec_ref[0:1, :], vec_ref[1:2, :]
    k_k, k_a, r_k = vec_ref[2:3, :], vec_ref[3:4, :], vec_ref[4:5, :]
    w_log = -_softplus(-(w0 + _bdot(jnp.tanh(wd), wup_ref[...]))) - 0.5
    lw = -jnp.exp(w_log)
    a = _sigmoid(a0 + _bdot(ad, aup_ref[...]))
    kk = k * k_k
    kk = kk * lax.rsqrt(_seg_sum(kk * kk, n) + L2_EPS)
    k = k * (1.0 + (a - 1.0) * k_a)
    bonus = _seg_sum(r * k * r_k, n) * v
    b = kk * a

    incl, strict = _order_masks(c, rev)
    cum = _dot_mask_lhs(jnp.where(incl, 1.0, 0.0).astype(BF16), lw)
    cum_end = cum[keep:keep + 1, :]
    e_pos = jnp.exp(cum)
    e_neg = jnp.exp(-cum)
    r_t = r * e_pos
    a_t = -kk * jnp.exp(cum - lw)
    k_t = k * e_neg
    b_t = b * e_neg
    w_end = jnp.exp(cum_end)
    kw = k_t * w_end
    bw = b_t * w_end
    eye = lax.broadcasted_iota(jnp.int32, (n, n), 0) == lax.broadcasted_iota(jnp.int32, (n, n), 1)

    ys = []
    for h in range(RWKV_HEADS):
        sl = slice(h * n, (h + 1) * n)
        vh = v[:, sl]
        s1 = _bdot_nt(jnp.concatenate([a_t[:, sl], r_t[:, sl]], axis=0),
                      jnp.concatenate([b_t[:, sl], k_t[:, sl]], axis=0))
        a_ab = jnp.where(strict, s1[:c, :c], 0.0)
        a_ak = jnp.where(strict, s1[:c, c:], 0.0)
        a_rb = jnp.where(incl, s1[c:, :c], 0.0)
        a_rk = jnp.where(incl, s1[c:, c:], 0.0)
        tm = _neumann_inverse(a_ab, c, _dot3)
        av = _bdot(jnp.concatenate([a_ak, a_rk], axis=0), vh)
        pp = _dot3(tm, jnp.concatenate([a_t[:, sl], av[:c]], axis=1))
        rq_y0 = _bdot(a_rb, pp)
        m_g = _bdot_tn(bw[:, sl], pp)
        rq = r_t[:, sl] + rq_y0[:, :n]
        y0 = rq_y0[:, n:] + av[c:]
        m = m_g[:, :n] + jnp.where(eye, jnp.broadcast_to(w_end[:, sl], (n, n)), 0.0)
        g = m_g[:, n:] + _bdot_tn(kw[:, sl], vh)
        st = st_scr[h]
        so = _dot3(jnp.concatenate([rq, m], axis=0), st)
        st_scr[h] = so[c:] + g
        ys.append(so[:c] + y0)
    y = jnp.concatenate(ys, axis=1)

    if final:
        ysum = y + yb_ref[...]
        mu_h = _seg_sum(ysum, n) * (1.0 / n)
        yc = ysum - mu_h
        var = _seg_sum(yc * yc, n) * (1.0 / n)
        ln_g, ln_b = vec_ref[5:6, :], vec_ref[6:7, :]
        yn = yc * lax.rsqrt(var + GN_EPS) * ln_g + ln_b + bonus + bb_ref[...]
        gate = _bdot(_sigmoid(gl_ref[...]), gup_ref[...])
        o_ref[...] = (yn * gate).astype(o_ref.dtype)
    else:
        y_ref[...] = y
        bonus_ref[...] = bonus


def _rwkv_call(p, prm, other, rev, n_ctx):
    t = p.shape[0]
    c = RWKV_CHUNK
    n, nc = t // c, n_ctx // c
    wdt = RWKV_W
    d = 1 if rev else 0
    final = other is not None
    cm = lambda s: _chunk_index(s, rev, nc, n)
    base = 4096 // wdt
    full = lambda shape: pl.BlockSpec(shape, lambda s: (0,) * len(shape))
    in_specs = [pl.BlockSpec((c, wdt), lambda s: (cm(s), base)),
                pl.BlockSpec((c, wdt), lambda s: (cm(s), base + 1)),
                pl.BlockSpec((c, wdt), lambda s: (cm(s), base + 2)),
                pl.BlockSpec((c, LANES), lambda s: (cm(s), 7168 // LANES)),
                pl.BlockSpec((c, LANES), lambda s: (cm(s), 7168 // LANES + 1)),
                full((1, 3 * wdt)), full((2, LANES)), full((SUBLANES, wdt)),
                full((RWKV_LORA, wdt)), full((RWKV_LORA, wdt))]
    args = [p, p, p, p, p, prm["mu"][d], prm["mu_small"][d], prm["vecs"][d], prm["w_up"][d], prm["a_up"][d]]
    if final:
        in_specs += [pl.BlockSpec((c, 2 * LANES), lambda s: (cm(s), 7424 // (2 * LANES))),
                     full((2 * LANES, wdt)),
                     pl.BlockSpec((c, wdt), lambda s: (cm(s), 0)),
                     pl.BlockSpec((c, wdt), lambda s: (cm(s), 0))]
        args += [p, prm["g_up"], other[0], other[1]]
        out_specs = pl.BlockSpec((c, wdt), lambda s: (cm(s), 0))
        out_shape = jax.ShapeDtypeStruct((t, wdt), BF16)
    else:
        out_specs = [pl.BlockSpec((c, wdt), lambda s: (cm(s), 0))] * 2
        out_shape = [jax.ShapeDtypeStruct((t, wdt), F32)] * 2
    return pl.pallas_call(
        functools.partial(_rwkv_kernel, rev=rev, final=final, c=c, nc=nc),
        grid=(n,),
        in_specs=in_specs,
        out_specs=out_specs,
        out_shape=out_shape,
        scratch_shapes=[pltpu.VMEM((RWKV_HEADS, RWKV_N, RWKV_N), F32),
                        pltpu.VMEM((SUBLANES, 3 * wdt + 2 * LANES), F32)],
        compiler_params=_cparams(1),
        name="rwkv7_bwd" if rev else "rwkv7_fwd_merge",
    )(*args)


def _mlstm_kernel(*refs, rev, final, c):
    if final:
        q_ref, k_ref, v_ref, sm_ref, bias_ref, og_ref, ng_ref, hb_ref, o_ref, ct_scr, m_scr = refs
    else:
        q_ref, k_ref, v_ref, sm_ref, bias_ref, o_ref, ct_scr, m_scr = refs
    d = 1 if rev else 0
    dk, dv = MLSTM_DK, MLSTM_DV
    nh = MLSTM_HEADS

    @pl.when(pl.program_id(0) == 0)
    def _():
        ct_scr[...] = jnp.zeros_like(ct_scr)
        m_scr[...] = jnp.zeros_like(m_scr)

    incl, _ = _order_masks(c, rev)
    keep = 0 if rev else c - 1
    sm = sm_ref[...]
    li = sm + bias_ref[0:1, :]
    lf = _log_sigmoid(sm + bias_ref[1:2, :])
    bcum = _dot_mask_lhs(jnp.where(incl, 1.0, 0.0).astype(BF16), lf)
    bcum_t = bcum.T
    li_t = li.T
    ones_col = jnp.where(lax.broadcasted_iota(jnp.int32, (c, LANES), 1) == 0, 1.0, 0.0).astype(BF16)

    for h in range(nh):
        ci = d * 2 * nh + h
        cf = ci + nh
        b_col = bcum[:, cf:cf + 1]
        b_row = bcum_t[cf:cf + 1, :]
        i_col = li[:, ci:ci + 1]
        i_row = li_t[ci:ci + 1, :]
        m_prev = m_scr[h:h + 1, 0:1]
        qh = q_ref[:, h * dk:(h + 1) * dk] * (dk ** -0.5)
        kh = k_ref[:, h * dk:(h + 1) * dk]
        v_aug = jnp.concatenate([v_ref[:, h * dv:(h + 1) * dv].astype(BF16), ones_col], axis=1)
        d_log = jnp.where(incl, b_col - b_row + i_row, -jnp.inf)
        inter = b_col + m_prev
        m_t = jnp.maximum(jnp.max(d_log, axis=-1, keepdims=True), inter)
        wgt = jnp.exp(d_log - m_t)
        s_inter = jnp.exp(inter - m_t)
        qk = _bdot_nt(qh, kh) * wgt
        ct = ct_scr[h]
        num_aug = _bdot(qk, v_aug) + s_inter * _bdot(qh, ct)
        den = num_aug[:, dv:dv + 1]
        hout = num_aug[:, :dv] / jnp.maximum(jnp.abs(den), jnp.exp(-m_t))
        b_end = b_col[keep:keep + 1, :]
        w_log = b_end - b_col + i_col
        m_new = jnp.maximum(b_end + m_prev, jnp.max(w_log, axis=0, keepdims=True))
        w_end = jnp.exp(w_log - m_new)
        ct_scr[h] = jnp.exp(b_end + m_prev - m_new) * ct + _bdot_tn(kh * w_end, v_aug)
        m_scr[h:h + 1, :] = jnp.broadcast_to(m_new, (1, LANES))
        sl = slice(h * dv, (h + 1) * dv)
        if final:
            tot = hout + hb_ref[:, sl]
            y = tot * lax.rsqrt(jnp.mean(tot * tot, axis=-1, keepdims=True) + EPS)
            o_ref[:, sl] = (y * ng_ref[0:1, sl] * _sigmoid(og_ref[:, sl])).astype(o_ref.dtype)
        else:
            o_ref[:, sl] = hout


def _mlstm_call(p, bias_rows, norm_g, other, rev, n_ctx):
    t = p.shape[0]
    c = MLSTM_CHUNK
    n, nc = t // c, n_ctx // c
    dk, dv, nh = MLSTM_DK, MLSTM_DV, MLSTM_HEADS
    final = other is not None
    cm = lambda s: _chunk_index(s, rev, nc, n)
    full = lambda shape: pl.BlockSpec(shape, lambda s: (0,) * len(shape))
    in_specs = [pl.BlockSpec((c, nh * dk), lambda s: (cm(s), 0)),
                pl.BlockSpec((c, nh * dk), lambda s: (cm(s), 1)),
                pl.BlockSpec((c, nh * dv), lambda s: (cm(s), 1)),
                pl.BlockSpec((c, LANES), lambda s: (cm(s), 7168 // LANES)),
                full((2, LANES))]
    args = [p, p, p, p, bias_rows]
    if final:
        in_specs += [pl.BlockSpec((c, nh * dv), lambda s: (cm(s), 2)), full((1, nh * dv)),
                     pl.BlockSpec((c, nh * dv), lambda s: (cm(s), 0))]
        args += [p, norm_g.reshape(1, nh * dv), other]
    return pl.pallas_call(
        functools.partial(_mlstm_kernel, rev=rev, final=final, c=c),
        grid=(n,),
        in_specs=in_specs,
        out_specs=pl.BlockSpec((c, nh * dv), lambda s: (cm(s), 0)),
        out_shape=jax.ShapeDtypeStruct((t, nh * dv), BF16 if final else F32),
        scratch_shapes=[pltpu.VMEM((nh, dk, dv + LANES), F32), pltpu.VMEM((SUBLANES, LANES), F32)],
        compiler_params=_cparams(1),
        name="mlstm_bwd" if rev else "mlstm_fwd_merge",
    )(*args)


def _gdn_prep_kernel(xm_ref, xp_ref, xn_ref, cw_ref, sm_ref, gp_ref, q_ref, k_ref, v_ref, gb_ref, *, tm, n_ctx, t_total):
    i = pl.program_id(0)
    x = xm_ref[...]
    rows = i * tm + lax.broadcasted_iota(jnp.int32, (tm, 1), 0)
    first = rows == i * tm
    last = rows == i * tm + tm - 1
    prev = jnp.where(first, xp_ref[SUBLANES - 1:SUBLANES, :], pltpu.roll(x, 1, axis=0))
    nxt = jnp.where(last, xn_ref[0:1, :], pltpu.roll(x, tm - 1, axis=0))
    prev = jnp.where(jnp.logical_or(rows == 0, rows == n_ctx), 0.0, prev)
    nxt = jnp.where(jnp.logical_or(rows == n_ctx - 1, rows == t_total - 1), 0.0, nxt)
    y = _silu(prev * cw_ref[0:1, :] + x * cw_ref[1:2, :] + nxt * cw_ref[2:3, :])
    w = GDN_HEADS * GDN_DK
    q, k, v = y[:, :w], y[:, w:2 * w], y[:, 2 * w:]
    q_ref[...] = (q * lax.rsqrt(_seg_sum(q * q, GDN_DK) + L2_EPS) * (GDN_DK ** -0.5)).astype(BF16)
    k_ref[...] = (k * lax.rsqrt(_seg_sum(k * k, GDN_DK) + L2_EPS)).astype(BF16)
    v_ref[...] = v.astype(BF16)
    sm = sm_ref[...]
    lane = lax.broadcasted_iota(jnp.int32, sm.shape, 1)
    log_alpha = -jnp.exp(gp_ref[0:1, :]) * _softplus(sm + gp_ref[1:2, :])
    gb_ref[...] = jnp.where(lane < 32, log_alpha, _sigmoid(sm))


def _gdn_prep_call(p, conv_w, gate_params, n_ctx):
    t = p.shape[0]
    tm = 256
    w = GDN_HEADS * GDN_DK
    qkv_blk = 3072 // GDN_QKV
    nb8 = t // SUBLANES
    r8 = tm // SUBLANES
    spec_o = pl.BlockSpec((tm, w), lambda i: (i, 0))
    return pl.pallas_call(
        functools.partial(_gdn_prep_kernel, tm=tm, n_ctx=n_ctx, t_total=t),
        grid=(t // tm,),
        in_specs=[pl.BlockSpec((tm, GDN_QKV), lambda i: (i, qkv_blk)),
                  pl.BlockSpec((SUBLANES, GDN_QKV), lambda i: (jnp.maximum(i * r8 - 1, 0), qkv_blk)),
                  pl.BlockSpec((SUBLANES, GDN_QKV), lambda i: (jnp.minimum((i + 1) * r8, nb8 - 1), qkv_blk)),
                  pl.BlockSpec((3, GDN_QKV), lambda i: (0, 0)),
                  pl.BlockSpec((tm, LANES), lambda i: (i, 7168 // LANES)),
                  pl.BlockSpec((2, LANES), lambda i: (0, 0))],
        out_specs=[spec_o, spec_o, spec_o, pl.BlockSpec((tm, LANES), lambda i: (i, 0))],
        out_shape=[jax.ShapeDtypeStruct((t, w), BF16)] * 3 + [jax.ShapeDtypeStruct((t, LANES), F32)],
        compiler_params=_cparams(1),
        name="gdn_conv_norm_gates",
    )(p, p, p, conv_w, p, gate_params)


def _gdn_kernel(*refs, rev, final, c):
    if final:
        q_ref, k_ref, v_ref, gb_ref, gate_ref, ng_ref, ob_ref, o_ref, st_scr = refs
    else:
        q_ref, k_ref, v_ref, gb_ref, o_ref, st_scr = refs
    d = 1 if rev else 0
    dk = GDN_DK
    nh = GDN_HEADS

    @pl.when(pl.program_id(0) == 0)
    def _():
        st_scr[...] = jnp.zeros_like(st_scr)

    incl, strict = _order_masks(c, rev)
    keep = 0 if rev else c - 1
    gb = gb_ref[...]
    gc = _dot_mask_lhs(jnp.where(incl, 1.0, 0.0).astype(BF16), gb)
    gc_t = gc.T

    for h in range(nh):
        col = 16 + d * nh + h
        g_col = gc[:, col:col + 1]
        g_row = gc_t[col:col + 1, :]
        beta = gb[:, col + 16:col + 17]
        g_end = g_col[keep:keep + 1, :]
        sl = slice(h * dk, (h + 1) * dk)
        qh = q_ref[:, sl].astype(F32)
        kh = k_ref[:, sl].astype(F32)
        vh = v_ref[:, sl].astype(F32)
        decay = jnp.exp(jnp.where(incl, g_col - g_row, -jnp.inf))
        kq = _bdot_nt(jnp.concatenate([kh * beta, qh], axis=0), kh)
        lower = jnp.where(strict, kq[:c] * decay, 0.0)
        a_qk = kq[c:] * decay
        tinv = _neumann_inverse(-lower, c, _dot3)
        e_g = jnp.exp(g_col)
        uw = _dot3(tinv, jnp.concatenate([vh * beta, kh * (beta * e_g)], axis=1))
        kd = kh * jnp.exp(g_end - g_col)
        o_part = _bdot(a_qk, uw)
        s_part = _bdot_tn(kd, uw)
        rq = qh * e_g - o_part[:, dk:]
        st = st_scr[h]
        so = _dot3(jnp.concatenate([rq, s_part[:, dk:]], axis=0), st)
        out = so[:c] + o_part[:, :dk]
        st_scr[h] = jnp.exp(g_end) * st - so[c:] + s_part[:, :dk]
        if final:
            tot = out + ob_ref[:, sl]
            y = tot * lax.rsqrt(jnp.mean(tot * tot, axis=-1, keepdims=True) + EPS)
            o_ref[:, sl] = (y * ng_ref[0:1, sl] * _silu(gate_ref[:, sl])).astype(o_ref.dtype)
        else:
            o_ref[:, sl] = out


def _gdn_call(p, q, k, v, gb, norm_g, other, rev, n_ctx):
    t = q.shape[0]
    c = GDN_CHUNK
    n, nc = t // c, n_ctx // c
    w = GDN_HEADS * GDN_DK
    final = other is not None
    cm = lambda s: _chunk_index(s, rev, nc, n)
    blk = pl.BlockSpec((c, w), lambda s: (cm(s), 0))
    in_specs = [blk, blk, blk, pl.BlockSpec((c, LANES), lambda s: (cm(s), 0))]
    args = [q, k, v, gb]
    if final:
        in_specs += [pl.BlockSpec((c, w), lambda s: (cm(s), 6144 // w)), pl.BlockSpec((1, w), lambda s: (0, 0)), blk]
        args += [p, norm_g.reshape(1, w), other]
    return pl.pallas_call(
        functools.partial(_gdn_kernel, rev=rev, final=final, c=c),
        grid=(n,),
        in_specs=in_specs,
        out_specs=blk,
        out_shape=jax.ShapeDtypeStruct((t, w), BF16 if final else F32),
        scratch_shapes=[pltpu.VMEM((GDN_HEADS, GDN_DK, GDN_DK), F32)],
        compiler_params=_cparams(1),
        name="gdn_bwd" if rev else "gdn_fwd_merge",
    )(*args)


def _pad_cols(w, width):
    return jnp.pad(w, ((0, 0), (0, width - w.shape[1])))


def _lane_row(pieces):
    row = jnp.zeros((LANES,), F32)
    for off, vec in pieces:
        row = row.at[off:off + vec.shape[0]].set(vec.astype(F32))
    return row


def _odd_weight(w):
    cols = [w[:, 0:3072], w[:, 3088:6160], w[:, 6160:7184], w[:, 3072:3088], w[:, 7184:7216]]
    return _pad_cols(jnp.concatenate(cols, axis=1), F_PAD).astype(BF16)


def _rwkv_params(e, rwkv_mu, rwkv_w0, rwkv_w_up, rwkv_a0, rwkv_a_up, rwkv_k_k, rwkv_k_a, rwkv_r_k,
                 rwkv_ln_g, rwkv_ln_b, rwkv_g_up):
    wdt = RWKV_W
    mu = rwkv_mu[e]
    zeros64 = jnp.zeros((RWKV_LORA,), F32)
    mu_small, vecs = [], []
    for d in range(2):
        m_wd = mu[d, 3 * wdt:3 * wdt + RWKV_LORA]
        m_ad = mu[d, 3 * wdt + RWKV_LORA:]
        lo = [m_wd, zeros64] if d == 0 else [zeros64, m_wd]
        la = [m_ad, zeros64] if d == 0 else [zeros64, m_ad]
        mu_small.append(jnp.stack([jnp.concatenate(lo), jnp.concatenate(la)]))
        vecs.append(jnp.stack([rwkv_w0[e, d], rwkv_a0[e, d], rwkv_k_k[e], rwkv_k_a[e], rwkv_r_k[e].reshape(wdt),
                               rwkv_ln_g[e], rwkv_ln_b[e], jnp.zeros((wdt,), F32)]))
    return {
        "mu": mu[:, None, :3 * wdt],
        "mu_small": jnp.stack(mu_small),
        "vecs": jnp.stack(vecs),
        "w_up": rwkv_w_up[e].astype(BF16),
        "a_up": rwkv_a_up[e].astype(BF16),
        "g_up": jnp.pad(rwkv_g_up[e], ((0, 2 * LANES - RWKV_GATE_LORA), (0, 0))).astype(BF16),
    }


def _even_mixer(p, e, n_ctx, ret_decay_logit, rwkv_prm):
    logit = ret_decay_logit[e].astype(F32)
    ret_b = _retention_call(p, logit, None, True, n_ctx)
    y_ret = _retention_call(p, logit, ret_b, False, n_ctx)
    rw_b = _rwkv_call(p, rwkv_prm, None, True, n_ctx)
    y_rwkv = _rwkv_call(p, rwkv_prm, rw_b, False, n_ctx)
    return y_ret, y_rwkv


def _odd_mixer(p, o, n_ctx, mlstm_gate_b, mlstm_norm_g, gdn_conv, gdn_a_log, gdn_dt_bias, gdn_norm_g):
    gate_b = mlstm_gate_b[o]
    nh = MLSTM_HEADS
    bias_i = _lane_row([(d * 2 * nh, gate_b[d, 0]) for d in range(2)])
    bias_f = _lane_row([(d * 2 * nh + nh, gate_b[d, 1]) for d in range(2)])
    bias_rows = jnp.stack([bias_i, bias_f])
    ml_b = _mlstm_call(p, bias_rows, mlstm_norm_g[o], None, True, n_ctx)
    y_ml = _mlstm_call(p, bias_rows, mlstm_norm_g[o], ml_b, False, n_ctx)

    gate_params = jnp.stack([_lane_row([(16, gdn_a_log[o].reshape(-1))]), _lane_row([(16, gdn_dt_bias[o].reshape(-1))])])
    q, k, v, gb = _gdn_prep_call(p, gdn_conv[o], gate_params, n_ctx)
    gd_b = _gdn_call(p, q, k, v, gb, gdn_norm_g[o], None, True, n_ctx)
    y_gd = _gdn_call(p, q, k, v, gb, gdn_norm_g[o], gd_b, False, n_ctx)
    return y_ml, y_gd


def kernel(x, c, ctx, c_ctx, ada_w, ada_b, norm1_g, norm2_g, mix_w_out, ffn_w_in, ffn_dw, ffn_dw_b, ffn_w_out, final_norm_g, ev_w_in, ret_decay_logit, rwkv_mu, rwkv_w0, rwkv_w_up, rwkv_a0, rwkv_a_up, rwkv_k_k, rwkv_k_a, rwkv_r_k, rwkv_ln_g, rwkv_ln_b, rwkv_g_up, od_w_in, mlstm_gate_b, mlstm_norm_g, gdn_conv, gdn_a_log, gdn_dt_bias, gdn_norm_g):
    assert x.shape[0] == 1 and ctx.shape[0] == 1
    n_ctx = ctx.shape[1]
    n_lat = x.shape[1]
    depth = ada_w.shape[0]
    xt = jnp.concatenate([ctx[0], x[0]], axis=0)
    cvecs = jnp.stack([c_ctx, c[0]], axis=1)
    mod = _ada_call(cvecs, ada_w, ada_b)

    for l in range(depth):
        last = l == depth - 1
        half = D_MODEL // 2
        if l % 2 == 0:
            e = l // 2
            w_in = _pad_cols(ev_w_in[e], F_PAD).astype(BF16)
            p = _proj_call(xt, norm1_g[l], mod, l, w_in, n_ctx)
            prm = _rwkv_params(e, rwkv_mu, rwkv_w0, rwkv_w_up, rwkv_a0, rwkv_a_up, rwkv_k_k, rwkv_k_a, rwkv_r_k,
                               rwkv_ln_g, rwkv_ln_b, rwkv_g_up)
            ya, yb = _even_mixer(p, e, n_ctx, ret_decay_logit, prm)
        else:
            o = l // 2
            p = _proj_call(xt, norm1_g[l], mod, l, _odd_weight(od_w_in[o]), n_ctx)
            ya, yb = _odd_mixer(p, o, n_ctx, mlstm_gate_b, mlstm_norm_g, gdn_conv, gdn_a_log, gdn_dt_bias, gdn_norm_g)
        w_mix = mix_w_out[l].astype(BF16)
        xt, h2 = _outproj_call(ya, yb, w_mix[:half], w_mix[half:], xt, norm2_g[l], mod, l, n_ctx)
        wg = _pad_cols(ffn_w_in[l][:, :D_FF], FF_PAD).astype(BF16)
        wu = _pad_cols(ffn_w_in[l][:, D_FF:], FF_PAD).astype(BF16)
        gate, up = _ffn_in_call(h2, wg, wu)
        dw9 = _pad_cols(ffn_dw[l].reshape(9, D_FF), FF_PAD)
        dwb = _pad_cols(ffn_dw_b[l].reshape(1, D_FF), FF_PAD)
        wo = jnp.pad(ffn_w_out[l], ((0, FF_PAD - D_FF), (0, 0))).astype(BF16)
        xt = _ffn_out_call(gate, up, dw9, dwb, wo, xt, mod, l, final_norm_g, n_ctx, last)
    return xt[n_ctx:][None]
```

```python
import functools
import math

import jax
import jax.numpy as jnp
from jax import lax
from jax.experimental import pallas as pl
from jax.experimental.pallas import tpu as pltpu

F32 = jnp.float32
BF16 = jnp.bfloat16

D_MODEL = 2048
DEPTH = 2
GRID_W = 64
EPS = 1e-6
GN_EPS = 64e-5
L2_EPS = 1e-12
D_FF = 5504

RET_HEADS = 8
RET_DK = 128
RWKV_HEADS = 16
RWKV_N = 64
RWKV_W = RWKV_HEADS * RWKV_N
RWKV_LORA = 64
RWKV_GATE_LORA = 160
MLSTM_HEADS = 4
MLSTM_DK = 128
MLSTM_DV = 256
GDN_HEADS = 8
GDN_DK = 128
GDN_QKV = 3072

LANES = 128
SUBLANES = 8
VMEM_LIMIT = 56 * 1024 * 1024

F_PAD = 7680
FF_PAD = 5632
ROW_TILE = 768
RET_CHUNK = 256
MLSTM_CHUNK = 64
GDN_CHUNK = 64
RWKV_CHUNK = 64
RWKV_GROUP = 8
GDN_GROUP = 8


def _cparams(n_axes):
    return pltpu.CompilerParams(dimension_semantics=("arbitrary",) * n_axes, vmem_limit_bytes=VMEM_LIMIT)


def _bdot(a, b):
    return jnp.dot(a.astype(BF16), b.astype(BF16), preferred_element_type=F32)


def _bdot_nt(a, b):
    return lax.dot_general(a.astype(BF16), b.astype(BF16), (((1,), (1,)), ((), ())), preferred_element_type=F32)


def _bdot_tn(a, b):
    return lax.dot_general(a.astype(BF16), b.astype(BF16), (((0,), (0,)), ((), ())), preferred_element_type=F32)


def _split3(x):
    x1 = x.astype(BF16)
    r1 = x - x1.astype(F32)
    x2 = r1.astype(BF16)
    x3 = (r1 - x2.astype(F32)).astype(BF16)
    return x1, x2, x3


def _dot_mask_lhs(m_bf16, x):
    x1, x2, x3 = _split3(x)
    d = lambda t: jnp.dot(m_bf16, t, preferred_element_type=F32)
    return (d(x3) + d(x2)) + d(x1)


def _dot3(a, b):
    a1 = a.astype(BF16)
    a2 = (a - a1.astype(F32)).astype(BF16)
    b1 = b.astype(BF16)
    b2 = (b - b1.astype(F32)).astype(BF16)
    d = lambda u, v: jnp.dot(u, v, preferred_element_type=F32)
    return (d(a2, b1) + d(a1, b2)) + d(a1, b1)


def _sigmoid(x):
    return 1.0 / (1.0 + jnp.exp(-x))


def _silu(x):
    return x * _sigmoid(x)


def _softplus(x):
    return jnp.maximum(x, 0.0) + jnp.log1p(jnp.exp(-jnp.abs(x)))


def _log_sigmoid(x):
    return -_softplus(-x)


def _order_masks(c, rev):
    i = lax.broadcasted_iota(jnp.int32, (c, c), 0)
    j = lax.broadcasted_iota(jnp.int32, (c, c), 1)
    if rev:
        return j >= i, j > i
    return j <= i, j < i


def _neumann_inverse(n, c, dotf):
    i = lax.broadcasted_iota(jnp.int32, (c, c), 0)
    j = lax.broadcasted_iota(jnp.int32, (c, c), 1)
    x = jnp.where(i == j, 1.0, 0.0).astype(F32) + n
    p = dotf(n, n)
    for _ in range(int(math.log2(c)) - 2):
        r = dotf(jnp.concatenate([x, p], axis=0), p)
        x = x + r[:c]
        p = r[c:]
    return x + dotf(x, p)


def _neumann_inverse_multi(ns, c, dotf):
    i = lax.broadcasted_iota(jnp.int32, (c, c), 0)
    j = lax.broadcasted_iota(jnp.int32, (c, c), 1)
    eye = jnp.where(i == j, 1.0, 0.0).astype(F32)
    xs = [eye + n for n in ns]
    ps = [dotf(n, n) for n in ns]
    for _ in range(int(math.log2(c)) - 2):
        rs = [dotf(jnp.concatenate([x, p], axis=0), p) for x, p in zip(xs, ps)]
        xs = [x + r[:c] for x, r in zip(xs, rs)]
        ps = [r[c:] for r in rs]
    return [x + dotf(x, p) for x, p in zip(xs, ps)]


def _dot_inv(a, b):
    return _dot3(a, b)


def _dot_state(a, b):
    return _dot3(a, b)


def _chunk_index(s, rev, n_ctx_chunks, n_chunks):
    if not rev:
        return s
    return jnp.where(s < n_ctx_chunks, n_ctx_chunks - 1 - s, n_chunks + n_ctx_chunks - 1 - s)


def _seg_sum(x, seg):
    c, w = x.shape
    if seg == LANES:
        parts = [jnp.broadcast_to(jnp.sum(x[:, b * LANES:(b + 1) * LANES], axis=-1, keepdims=True), (c, LANES))
                 for b in range(w // LANES)]
        return jnp.concatenate(parts, axis=-1)
    assert seg * 2 == LANES
    lane = lax.broadcasted_iota(jnp.int32, (c, LANES), 1)
    low = lane < seg
    parts = []
    for b in range(w // LANES):
        xb = x[:, b * LANES:(b + 1) * LANES]
        s_lo = jnp.sum(jnp.where(low, xb, 0.0), axis=-1, keepdims=True)
        s_hi = jnp.sum(jnp.where(low, 0.0, xb), axis=-1, keepdims=True)
        parts.append(jnp.where(low, s_lo, s_hi))
    return jnp.concatenate(parts, axis=-1)


def _ada_kernel(c_ref, w_ref, b_ref, o_ref):
    cv = c_ref[...]
    s = _silu(cv)
    w = w_ref[0]
    bias = b_ref[0]
    r0 = jnp.sum(w * s[:, 0:1], axis=0, keepdims=True) + bias
    r1 = jnp.sum(w * s[:, 1:2], axis=0, keepdims=True) + bias
    row = lax.broadcasted_iota(jnp.int32, (SUBLANES, w.shape[1]), 0)
    o_ref[0] = jnp.where(row == 0, r0, jnp.where(row == 1, r1, 0.0))


def _ada_call(cvecs, ada_w, ada_b):
    depth, d, n = ada_w.shape
    tn = 1024
    return pl.pallas_call(
        _ada_kernel,
        grid=(depth, n // tn),
        in_specs=[pl.BlockSpec((d, 2), lambda l, j: (0, 0)),
                  pl.BlockSpec((1, d, tn), lambda l, j: (l, 0, j)),
                  pl.BlockSpec((1, 1, tn), lambda l, j: (l, 0, j))],
        out_specs=pl.BlockSpec((1, SUBLANES, tn), lambda l, j: (l, 0, j)),
        out_shape=jax.ShapeDtypeStruct((depth, SUBLANES, n), F32),
        compiler_params=_cparams(2),
        name="ada_modulation",
    )(cvecs, ada_w, ada_b.reshape(depth, 1, n))


def _mod_rows(mod, k, rows, n_ctx):
    d = D_MODEL
    vc = mod[0:1, k * d:(k + 1) * d]
    vl = mod[1:2, k * d:(k + 1) * d]
    return jnp.where(rows < n_ctx, vc, vl)


def _norm_mod(x, g, shift, scale):
    y = x * lax.rsqrt(jnp.mean(x * x, axis=-1, keepdims=True) + EPS) * g
    return y * (1.0 + scale) + shift


def _proj_kernel(x_ref, g_ref, mod_ref, w_ref, o_ref, h_scr, *, tm, n_ctx):
    i = pl.program_id(0)

    @pl.when(pl.program_id(1) == 0)
    def _():
        rows = i * tm + lax.broadcasted_iota(jnp.int32, (tm, 1), 0)
        mod = mod_ref[0]
        h = _norm_mod(x_ref[...], g_ref[...], _mod_rows(mod, 0, rows, n_ctx), _mod_rows(mod, 1, rows, n_ctx))
        h_scr[...] = h.astype(BF16)

    o_ref[...] = jnp.dot(h_scr[...], w_ref[...], preferred_element_type=F32)


def _proj_call(x, g, mod, layer, w_bf16, n_ctx):
    t, d = x.shape
    f = w_bf16.shape[1]
    tm, tn = ROW_TILE, 768
    return pl.pallas_call(
        functools.partial(_proj_kernel, tm=tm, n_ctx=n_ctx),
        grid=(t // tm, f // tn),
        in_specs=[pl.BlockSpec((tm, d), lambda i, j: (i, 0)),
                  pl.BlockSpec((1, d), lambda i, j: (0, 0)),
                  pl.BlockSpec((1, SUBLANES, 6 * d), lambda i, j: (layer, 0, 0)),
                  pl.BlockSpec((d, tn), lambda i, j: (0, j))],
        out_specs=pl.BlockSpec((tm, tn), lambda i, j: (i, j)),
        out_shape=jax.ShapeDtypeStruct((t, f), F32),
        scratch_shapes=[pltpu.VMEM((tm, d), BF16)],
        compiler_params=_cparams(2),
        name="norm_mod_in_proj",
    )(x, g.reshape(1, d), mod, w_bf16)


def _outproj_kernel(ya_ref, yb_ref, wa_ref, wb_ref, x_ref, g_ref, mod_ref, xo_ref, h_ref, *, tm, n_ctx):
    i = pl.program_id(0)
    rows = i * tm + lax.broadcasted_iota(jnp.int32, (tm, 1), 0)
    mod = mod_ref[0]
    acc = jnp.dot(ya_ref[...], wa_ref[...], preferred_element_type=F32)
    acc = acc + jnp.dot(yb_ref[...], wb_ref[...], preferred_element_type=F32)
    xn = x_ref[...] + _mod_rows(mod, 2, rows, n_ctx) * acc
    xo_ref[...] = xn
    h = _norm_mod(xn, g_ref[...], _mod_rows(mod, 3, rows, n_ctx), _mod_rows(mod, 4, rows, n_ctx))
    h_ref[...] = h.astype(BF16)


def _outproj_call(ya, yb, wa, wb, x, g2, mod, layer, n_ctx):
    t, d = x.shape
    half = ya.shape[1]
    tm = 384
    return pl.pallas_call(
        functools.partial(_outproj_kernel, tm=tm, n_ctx=n_ctx),
        grid=(t // tm,),
        in_specs=[pl.BlockSpec((tm, half), lambda i: (i, 0)),
                  pl.BlockSpec((tm, half), lambda i: (i, 0)),
                  pl.BlockSpec((half, d), lambda i: (0, 0)),
                  pl.BlockSpec((half, d), lambda i: (0, 0)),
                  pl.BlockSpec((tm, d), lambda i: (i, 0)),
                  pl.BlockSpec((1, d), lambda i: (0, 0)),
                  pl.BlockSpec((1, SUBLANES, 6 * d), lambda i: (layer, 0, 0))],
        out_specs=[pl.BlockSpec((tm, d), lambda i: (i, 0)),
                   pl.BlockSpec((tm, d), lambda i: (i, 0))],
        out_shape=[jax.ShapeDtypeStruct((t, d), F32), jax.ShapeDtypeStruct((t, d), BF16)],
        compiler_params=_cparams(1),
        name="mix_out_proj_norm2",
    )(ya, yb, wa, wb, x, g2.reshape(1, d), mod)


def _ffn_in_kernel(h_ref, wg_ref, wu_ref, g_ref, u_ref):
    h = h_ref[...]
    g_ref[...] = jnp.dot(h, wg_ref[...], preferred_element_type=F32).astype(BF16)
    u_ref[...] = jnp.dot(h, wu_ref[...], preferred_element_type=F32).astype(BF16)


def _ffn_in_call(h, wg, wu):
    t, d = h.shape
    fp = wg.shape[1]
    tm, tn = ROW_TILE, 512
    spec_w = pl.BlockSpec((d, tn), lambda i, j: (0, j))
    spec_o = pl.BlockSpec((tm, tn), lambda i, j: (i, j))
    return pl.pallas_call(
        _ffn_in_kernel,
        grid=(t // tm, fp // tn),
        in_specs=[pl.BlockSpec((tm, d), lambda i, j: (i, 0)), spec_w, spec_w],
        out_specs=[spec_o, spec_o],
        out_shape=[jax.ShapeDtypeStruct((t, fp), BF16)] * 2,
        compiler_params=_cparams(2),
        name="ffn_in_proj",
    )(h, wg, wu)


def _ffn_out_kernel(gm_ref, gp_ref, gn_ref, u_ref, dw_ref, db_ref, wo_ref, x_ref, mod_ref, fg_ref, o_ref, acc_scr,
                    *, tm, tk, n_ctx, t_total, final_norm):
    i = pl.program_id(0)
    k = pl.program_id(1)
    w = GRID_W

    @pl.when(k == 0)
    def _():
        acc_scr[...] = jnp.zeros_like(acc_scr)

    g_all = jnp.concatenate([gp_ref[...], gm_ref[...], gn_ref[...]], axis=0).astype(F32)
    n_all = tm + 2 * w
    rows = (i * tm - w) + lax.broadcasted_iota(jnp.int32, (n_all, tk), 0)
    is_ctx = rows < n_ctx
    col = jnp.bitwise_and(rows, w - 1)
    left_ok = jnp.where(is_ctx, rows, col) > 0
    right_ok = jnp.where(is_ctx, rows - (n_ctx - 1), col - (w - 1)) < 0
    g_left = jnp.where(left_ok, pltpu.roll(g_all, 1, axis=0), 0.0)
    g_right = jnp.where(right_ok, pltpu.roll(g_all, n_all - 1, axis=0), 0.0)
    dw = dw_ref[...]

    def taps(kh, lo):
        sl = slice(lo, lo + tm)
        return (g_left[sl] * dw[3 * kh:3 * kh + 1] + g_all[sl] * dw[3 * kh + 1:3 * kh + 2]
                + g_right[sl] * dw[3 * kh + 2:3 * kh + 3])

    rows_m = i * tm + lax.broadcasted_iota(jnp.int32, (tm, tk), 0)
    up_ok = rows_m >= n_ctx + w
    down_ok = jnp.where(rows_m >= n_ctx, rows_m, t_total) < t_total - w
    conv = taps(1, w) + jnp.where(up_ok, taps(0, 0), 0.0) + jnp.where(down_ok, taps(2, 2 * w), 0.0)
    gate = conv + db_ref[...]
    act = 0.5 * gate * (1.0 + lax.erf(gate * (2.0 ** -0.5)))
    act = act * u_ref[...].astype(F32)
    acc_scr[...] += jnp.dot(act.astype(BF16), wo_ref[...], preferred_element_type=F32)

    @pl.when(k == pl.num_programs(1) - 1)
    def _():
        rws = i * tm + lax.broadcasted_iota(jnp.int32, (tm, 1), 0)
        xn = x_ref[...] + _mod_rows(mod_ref[0], 5, rws, n_ctx) * acc_scr[...]
        if final_norm:
            xn = xn * lax.rsqrt(jnp.mean(xn * xn, axis=-1, keepdims=True) + EPS) * fg_ref[...]
        o_ref[...] = xn


def _ffn_out_call(gate, up, dw9, dwb, wo, x, mod, layer, final_g, n_ctx, final_norm):
    t, d = x.shape
    fp = gate.shape[1]
    tm, tk, w = ROW_TILE, 512, GRID_W
    rpt = tm // w
    n_rows = t // w
    return pl.pallas_call(
        functools.partial(_ffn_out_kernel, tm=tm, tk=tk, n_ctx=n_ctx, t_total=t, final_norm=final_norm),
        grid=(t // tm, fp // tk),
        in_specs=[pl.BlockSpec((tm, tk), lambda i, k: (i, k)),
                  pl.BlockSpec((w, tk), lambda i, k: (jnp.maximum(i * rpt - 1, 0), k)),
                  pl.BlockSpec((w, tk), lambda i, k: (jnp.minimum((i + 1) * rpt, n_rows - 1), k)),
                  pl.BlockSpec((tm, tk), lambda i, k: (i, k)),
                  pl.BlockSpec((9, tk), lambda i, k: (0, k)),
                  pl.BlockSpec((1, tk), lambda i, k: (0, k)),
                  pl.BlockSpec((tk, d), lambda i, k: (k, 0)),
                  pl.BlockSpec((tm, d), lambda i, k: (i, 0)),
                  pl.BlockSpec((1, SUBLANES, 6 * d), lambda i, k: (layer, 0, 0)),
                  pl.BlockSpec((1, d), lambda i, k: (0, 0))],
        out_specs=pl.BlockSpec((tm, d), lambda i, k: (i, 0)),
        out_shape=jax.ShapeDtypeStruct((t, d), F32),
        scratch_shapes=[pltpu.VMEM((tm, d), F32)],
        compiler_params=_cparams(2),
        name="ffn_conv_glu_out",
    )(gate, gate, gate, up, dw9, dwb, wo, x, mod, final_g.reshape(1, d))


def _retention_kernel(*refs, rev, final, c):
    if final:
        q_ref, k_ref, v_ref, lg_ref, gate_ref, ob_ref, o_ref, r_scr, di_scr, dq_scr, dk_scr, dc_scr = refs
    else:
        q_ref, k_ref, v_ref, lg_ref, o_ref, r_scr, di_scr, dq_scr, dk_scr, dc_scr = refs
    d = 1 if rev else 0
    dk = RET_DK

    @pl.when(pl.program_id(0) == 0)
    def _():
        r_scr[...] = jnp.zeros_like(r_scr)
        i = lax.broadcasted_iota(jnp.int32, (c, c), 0)
        j = lax.broadcasted_iota(jnp.int32, (c, c), 1)
        diff = ((j - i) if rev else (i - j)).astype(F32)
        row = lax.broadcasted_iota(jnp.int32, (c, dk), 0)
        pos = ((c - 1 - row) if rev else row).astype(F32)
        for h in range(RET_HEADS):
            lg = _log_sigmoid(lg_ref[d:d + 1, h:h + 1])
            di_scr[h] = jnp.where(diff >= 0, jnp.exp(lg * jnp.maximum(diff, 0.0)), 0.0)
            dq_scr[h] = jnp.exp(lg * (pos + 1.0))
            dk_scr[h] = jnp.exp(lg * (c - 1.0 - pos)) * (dk ** -0.5)
            dc_scr[h] = jnp.exp(jnp.broadcast_to(lg, (1, dk)) * c)

    for h in range(RET_HEADS):
        sl = slice(h * dk, (h + 1) * dk)
        qh = q_ref[:, sl]
        kh = k_ref[:, sl]
        vh = v_ref[:, sl].astype(BF16)
        r_state = r_scr[h]
        scores = _bdot_nt(qh, kh) * (di_scr[h] * (dk ** -0.5))
        out = _bdot(scores, vh) + _bdot(qh * dq_scr[h], r_state)
        r_scr[h] = dc_scr[h] * r_state + _bdot_tn(kh * dk_scr[h], vh)
        if final:
            tot = out + ob_ref[:, sl]
            y = tot * lax.rsqrt(jnp.mean(tot * tot, axis=-1, keepdims=True) + EPS) * _silu(gate_ref[:, sl])
            o_ref[:, sl] = y.astype(o_ref.dtype)
        else:
            o_ref[:, sl] = out


def _retention_call(p, logit, other, rev, n_ctx):
    t = p.shape[0]
    c = RET_CHUNK
    n, nc = t // c, n_ctx // c
    width = RET_HEADS * RET_DK
    final = other is not None
    cm = lambda s: _chunk_index(s, rev, nc, n)
    in_specs = [pl.BlockSpec((c, width), lambda s: (cm(s), 0)),
                pl.BlockSpec((c, width), lambda s: (cm(s), 1)),
                pl.BlockSpec((c, width), lambda s: (cm(s), 2)),
                pl.BlockSpec((2, RET_HEADS), lambda s: (0, 0))]
    args = [p, p, p, logit]
    if final:
        in_specs += [pl.BlockSpec((c, width), lambda s: (cm(s), 3)), pl.BlockSpec((c, width), lambda s: (cm(s), 0))]
        args += [p, other]
    return pl.pallas_call(
        functools.partial(_retention_kernel, rev=rev, final=final, c=c),
        grid=(n,),
        in_specs=in_specs,
        out_specs=pl.BlockSpec((c, width), lambda s: (cm(s), 0)),
        out_shape=jax.ShapeDtypeStruct((t, width), BF16 if final else F32),
        scratch_shapes=[pltpu.VMEM((RET_HEADS, RET_DK, RET_DK), F32),
                        pltpu.VMEM((RET_HEADS, c, c), F32),
                        pltpu.VMEM((RET_HEADS, c, RET_DK), F32),
                        pltpu.VMEM((RET_HEADS, c, RET_DK), F32),
                        pltpu.VMEM((RET_HEADS, 1, RET_DK), F32)],
        compiler_params=_cparams(1),
        name="retention_bwd" if rev else "retention_fwd_merge",
    )(*args)


def _rwkv_kernel(*refs, rev, final, c, nc):
    if final:
        (r_ref, k_ref, v_ref, wd_ref, ad_ref, mu_ref, mus_ref, vec_ref, wup_ref, aup_ref,
         gl_ref, gup_ref, yb_ref, bb_ref, o_ref, st_scr, carry_scr) = refs
    else:
        (r_ref, k_ref, v_ref, wd_ref, ad_ref, mu_ref, mus_ref, vec_ref, wup_ref, aup_ref,
         y_ref, bonus_ref, st_scr, carry_scr) = refs
    d = 1 if rev else 0
    s = pl.program_id(0)
    n = RWKV_N
    wdt = RWKV_W

    @pl.when(s == 0)
    def _():
        st_scr[...] = jnp.zeros_like(st_scr)

    @pl.when(jnp.logical_or(s == 0, s == nc))
    def _():
        carry_scr[...] = jnp.zeros_like(carry_scr)

    row = lax.broadcasted_iota(jnp.int32, (c, 1), 0)
    edge = (row == c - 1) if rev else (row == 0)
    keep = c - 1 if not rev else 0

    def shifted(x, lo):
        width = x.shape[1]
        prev = pltpu.roll(x, (c - 1) if rev else 1, axis=0)
        prev = jnp.where(edge, carry_scr[0:1, lo:lo + width], prev)
        carry_scr[0:1, lo:lo + width] = x[keep:keep + 1, :]
        return prev

    def mix(x, lo, mu):
        prev = shifted(x, lo)
        return x + (prev - x) * mu

    r = mix(r_ref[...], 0, mu_ref[0:1, 0:wdt])
    k = mix(k_ref[...], wdt, mu_ref[0:1, wdt:2 * wdt])
    v = mix(v_ref[...], 2 * wdt, mu_ref[0:1, 2 * wdt:3 * wdt])
    wd = mix(wd_ref[...], 3 * wdt, mus_ref[0:1, :])[:, d * RWKV_LORA:(d + 1) * RWKV_LORA]
    ad = mix(ad_ref[...], 3 * wdt + LANES, mus_ref[1:2, :])[:, d * RWKV_LORA:(d + 1) * RWKV_LORA]

    w0, a0 = vec_ref[0:1, :], vec_ref[1:2, :]
    k_k, k_a, r_k = vec_ref[2:3, :], vec_ref[3:4, :], vec_ref[4:5, :]
    w_log = -_softplus(-(w0 + _bdot(jnp.tanh(wd), wup_ref[...]))) - 0.5
    lw = -jnp.exp(w_log)
    a = _sigmoid(a0 + _bdot(ad, aup_ref[...]))
    kk = k * k_k
    kk = kk * lax.rsqrt(_seg_sum(kk * kk, n) + L2_EPS)
    k = k * (1.0 + (a - 1.0) * k_a)
    bonus = _seg_sum(r * k * r_k, n) * v
    b = kk * a

    incl, strict = _order_masks(c, rev)
    cum = _dot_mask_lhs(jnp.where(incl, 1.0, 0.0).astype(BF16), lw)
    cum_end = cum[keep:keep + 1, :]
    e_pos = jnp.exp(cum)
    e_neg = jnp.exp(-cum)
    r_t = r * e_pos
    a_t = -kk * jnp.exp(cum - lw)
    k_t = k * e_neg
    b_t = b * e_neg
    w_end = jnp.exp(cum_end)
    kw = k_t * w_end
    bw = b_t * w_end
    eye = lax.broadcasted_iota(jnp.int32, (n, n), 0) == lax.broadcasted_iota(jnp.int32, (n, n), 1)

    ys = []
    for h0 in range(0, RWKV_HEADS, RWKV_GROUP):
        hs = range(h0, h0 + RWKV_GROUP)
        sls = [slice(h * n, (h + 1) * n) for h in hs]
        s1 = [_bdot_nt(jnp.concatenate([a_t[:, sl], r_t[:, sl]], axis=0),
                       jnp.concatenate([b_t[:, sl], k_t[:, sl]], axis=0)) for sl in sls]
        a_ab = [jnp.where(strict, x[:c, :c], 0.0) for x in s1]
        a_lo = [jnp.concatenate([jnp.where(strict, x[:c, c:], 0.0), jnp.where(incl, x[c:, c:], 0.0)], axis=0) for x in s1]
        a_rb = [jnp.where(incl, x[c:, :c], 0.0) for x in s1]
        tms = _neumann_inverse_multi(a_ab, c, _dot_inv)
        av = [_bdot(a, v[:, sl]) for a, sl in zip(a_lo, sls)]
        pp = [_dot_inv(t, jnp.concatenate([a_t[:, sl], x[:c]], axis=1)) for t, sl, x in zip(tms, sls, av)]
        rq_y0 = [_bdot(a, p) for a, p in zip(a_rb, pp)]
        m_g = [_bdot_tn(bw[:, sl], p) for sl, p in zip(sls, pp)]
        kv = [_bdot_tn(kw[:, sl], v[:, sl]) for sl in sls]
        so = []
        for i, h in enumerate(hs):
            rq = r_t[:, sls[i]] + rq_y0[i][:, :n]
            m = m_g[i][:, :n] + jnp.where(eye, jnp.broadcast_to(w_end[:, sls[i]], (n, n)), 0.0)
            so.append(_dot_state(jnp.concatenate([rq, m], axis=0), st_scr[h]))
        for i, h in enumerate(hs):
            st_scr[h] = so[i][c:] + (m_g[i][:, n:] + kv[i])
            ys.append(so[i][:c] + (rq_y0[i][:, n:] + av[i][c:]))
    y = jnp.concatenate(ys, axis=1)

    if final:
        ysum = y + yb_ref[...]
        mu_h = _seg_sum(ysum, n) * (1.0 / n)
        yc = ysum - mu_h
        var = _seg_sum(yc * yc, n) * (1.0 / n)
        ln_g, ln_b = vec_ref[5:6, :], vec_ref[6:7, :]
        yn = yc * lax.rsqrt(var + GN_EPS) * ln_g + ln_b + bonus + bb_ref[...]
        gate = _bdot(_sigmoid(gl_ref[...]), gup_ref[...])
        o_ref[...] = (yn * gate).astype(o_ref.dtype)
    else:
        y_ref[...] = y
        bonus_ref[...] = bonus


def _rwkv_call(p, prm, other, rev, n_ctx):
    t = p.shape[0]
    c = RWKV_CHUNK
    n, nc = t // c, n_ctx // c
    wdt = RWKV_W
    d = 1 if rev else 0
    final = other is not None
    cm = lambda s: _chunk_index(s, rev, nc, n)
    base = 4096 // wdt
    full = lambda shape: pl.BlockSpec(shape, lambda s: (0,) * len(shape))
    in_specs = [pl.BlockSpec((c, wdt), lambda s: (cm(s), base)),
                pl.BlockSpec((c, wdt), lambda s: (cm(s), base + 1)),
                pl.BlockSpec((c, wdt), lambda s: (cm(s), base + 2)),
                pl.BlockSpec((c, LANES), lambda s: (cm(s), 7168 // LANES)),
                pl.BlockSpec((c, LANES), lambda s: (cm(s), 7168 // LANES + 1)),
                full((1, 3 * wdt)), full((2, LANES)), full((SUBLANES, wdt)),
                full((RWKV_LORA, wdt)), full((RWKV_LORA, wdt))]
    args = [p, p, p, p, p, prm["mu"][d], prm["mu_small"][d], prm["vecs"][d], prm["w_up"][d], prm["a_up"][d]]
    if final:
        in_specs += [pl.BlockSpec((c, 2 * LANES), lambda s: (cm(s), 7424 // (2 * LANES))),
                     full((2 * LANES, wdt)),
                     pl.BlockSpec((c, wdt), lambda s: (cm(s), 0)),
                     pl.BlockSpec((c, wdt), lambda s: (cm(s), 0))]
        args += [p, prm["g_up"], other[0], other[1]]
        out_specs = pl.BlockSpec((c, wdt), lambda s: (cm(s), 0))
        out_shape = jax.ShapeDtypeStruct((t, wdt), BF16)
    else:
        out_specs = [pl.BlockSpec((c, wdt), lambda s: (cm(s), 0))] * 2
        out_shape = [jax.ShapeDtypeStruct((t, wdt), F32)] * 2
    return pl.pallas_call(
        functools.partial(_rwkv_kernel, rev=rev, final=final, c=c, nc=nc),
        grid=(n,),
        in_specs=in_specs,
        out_specs=out_specs,
        out_shape=out_shape,
        scratch_shapes=[pltpu.VMEM((RWKV_HEADS, RWKV_N, RWKV_N), F32),
                        pltpu.VMEM((SUBLANES, 3 * wdt + 2 * LANES), F32)],
        compiler_params=_cparams(1),
        name="rwkv7_bwd" if rev else "rwkv7_fwd_merge",
    )(*args)


def _mlstm_kernel(*refs, rev, final, c):
    if final:
        q_ref, k_ref, v_ref, sm_ref, bias_ref, og_ref, ng_ref, hb_ref, o_ref, ct_scr, m_scr = refs
    else:
        q_ref, k_ref, v_ref, sm_ref, bias_ref, o_ref, ct_scr, m_scr = refs
    d = 1 if rev else 0
    dk, dv = MLSTM_DK, MLSTM_DV
    nh = MLSTM_HEADS

    @pl.when(pl.program_id(0) == 0)
    def _():
        ct_scr[...] = jnp.zeros_like(ct_scr)
        m_scr[...] = jnp.zeros_like(m_scr)

    incl, _ = _order_masks(c, rev)
    keep = 0 if rev else c - 1
    sm = sm_ref[...]
    li = sm + bias_ref[0:1, :]
    lf = _log_sigmoid(sm + bias_ref[1:2, :])
    bcum = _dot_mask_lhs(jnp.where(incl, 1.0, 0.0).astype(BF16), lf)
    bcum_t = bcum.T
    li_t = li.T
    ones_col = jnp.where(lax.broadcasted_iota(jnp.int32, (c, LANES), 1) == 0, 1.0, 0.0).astype(BF16)

    for h in range(nh):
        ci = d * 2 * nh + h
        cf = ci + nh
        b_col = bcum[:, cf:cf + 1]
        b_row = bcum_t[cf:cf + 1, :]
        i_col = li[:, ci:ci + 1]
        i_row = li_t[ci:ci + 1, :]
        m_prev = m_scr[h:h + 1, 0:1]
        qh = q_ref[:, h * dk:(h + 1) * dk] * (dk ** -0.5)
        kh = k_ref[:, h * dk:(h + 1) * dk]
        v_aug = jnp.concatenate([v_ref[:, h * dv:(h + 1) * dv].astype(BF16), ones_col], axis=1)
        d_log = jnp.where(incl, b_col - b_row + i_row, -jnp.inf)
        inter = b_col + m_prev
        m_t = jnp.maximum(jnp.max(d_log, axis=-1, keepdims=True), inter)
        wgt = jnp.exp(d_log - m_t)
        s_inter = jnp.exp(inter - m_t)
        qk = _bdot_nt(qh, kh) * wgt
        ct = ct_scr[h]
        num_aug = _bdot(qk, v_aug) + s_inter * _bdot(qh, ct)
        den = num_aug[:, dv:dv + 1]
        hout = num_aug[:, :dv] / jnp.maximum(jnp.abs(den), jnp.exp(-m_t))
        b_end = b_col[keep:keep + 1, :]
        w_log = b_end - b_col + i_col
        m_new = jnp.maximum(b_end + m_prev, jnp.max(w_log, axis=0, keepdims=True))
        w_end = jnp.exp(w_log - m_new)
        ct_scr[h] = jnp.exp(b_end + m_prev - m_new) * ct + _bdot_tn(kh * w_end, v_aug)
        m_scr[h:h + 1, :] = jnp.broadcast_to(m_new, (1, LANES))
        sl = slice(h * dv, (h + 1) * dv)
        if final:
            tot = hout + hb_ref[:, sl]
            y = tot * lax.rsqrt(jnp.mean(tot * tot, axis=-1, keepdims=True) + EPS)
            o_ref[:, sl] = (y * ng_ref[0:1, sl] * _sigmoid(og_ref[:, sl])).astype(o_ref.dtype)
        else:
            o_ref[:, sl] = hout


def _mlstm_call(p, bias_rows, norm_g, other, rev, n_ctx):
    t = p.shape[0]
    c = MLSTM_CHUNK
    n, nc = t // c, n_ctx // c
    dk, dv, nh = MLSTM_DK, MLSTM_DV, MLSTM_HEADS
    final = other is not None
    cm = lambda s: _chunk_index(s, rev, nc, n)
    full = lambda shape: pl.BlockSpec(shape, lambda s: (0,) * len(shape))
    in_specs = [pl.BlockSpec((c, nh * dk), lambda s: (cm(s), 0)),
                pl.BlockSpec((c, nh * dk), lambda s: (cm(s), 1)),
                pl.BlockSpec((c, nh * dv), lambda s: (cm(s), 1)),
                pl.BlockSpec((c, LANES), lambda s: (cm(s), 7168 // LANES)),
                full((2, LANES))]
    args = [p, p, p, p, bias_rows]
    if final:
        in_specs += [pl.BlockSpec((c, nh * dv), lambda s: (cm(s), 2)), full((1, nh * dv)),
                     pl.BlockSpec((c, nh * dv), lambda s: (cm(s), 0))]
        args += [p, norm_g.reshape(1, nh * dv), other]
    return pl.pallas_call(
        functools.partial(_mlstm_kernel, rev=rev, final=final, c=c),
        grid=(n,),
        in_specs=in_specs,
        out_specs=pl.BlockSpec((c, nh * dv), lambda s: (cm(s), 0)),
        out_shape=jax.ShapeDtypeStruct((t, nh * dv), BF16 if final else F32),
        scratch_shapes=[pltpu.VMEM((nh, dk, dv + LANES), F32), pltpu.VMEM((SUBLANES, LANES), F32)],
        compiler_params=_cparams(1),
        name="mlstm_bwd" if rev else "mlstm_fwd_merge",
    )(*args)


def _gdn_prep_kernel(xm_ref, xp_ref, xn_ref, cw_ref, sm_ref, gp_ref, q_ref, k_ref, v_ref, gb_ref, *, tm, n_ctx, t_total):
    i = pl.program_id(0)
    x = xm_ref[...]
    rows = i * tm + lax.broadcasted_iota(jnp.int32, (tm, 1), 0)
    first = rows == i * tm
    last = rows == i * tm + tm - 1
    prev = jnp.where(first, xp_ref[SUBLANES - 1:SUBLANES, :], pltpu.roll(x, 1, axis=0))
    nxt = jnp.where(last, xn_ref[0:1, :], pltpu.roll(x, tm - 1, axis=0))
    prev = jnp.where(jnp.logical_or(rows == 0, rows == n_ctx), 0.0, prev)
    nxt = jnp.where(jnp.logical_or(rows == n_ctx - 1, rows == t_total - 1), 0.0, nxt)
    y = _silu(prev * cw_ref[0:1, :] + x * cw_ref[1:2, :] + nxt * cw_ref[2:3, :])
    w = GDN_HEADS * GDN_DK
    q, k, v = y[:, :w], y[:, w:2 * w], y[:, 2 * w:]
    q_ref[...] = (q * lax.rsqrt(_seg_sum(q * q, GDN_DK) + L2_EPS) * (GDN_DK ** -0.5)).astype(BF16)
    k_ref[...] = (k * lax.rsqrt(_seg_sum(k * k, GDN_DK) + L2_EPS)).astype(BF16)
    v_ref[...] = v.astype(BF16)
    sm = sm_ref[...]
    lane = lax.broadcasted_iota(jnp.int32, sm.shape, 1)
    log_alpha = -jnp.exp(gp_ref[0:1, :]) * _softplus(sm + gp_ref[1:2, :])
    gb_ref[...] = jnp.where(lane < 32, log_alpha, _sigmoid(sm))


def _gdn_prep_call(p, conv_w, gate_params, n_ctx):
    t = p.shape[0]
    tm = 256
    w = GDN_HEADS * GDN_DK
    qkv_blk = 3072 // GDN_QKV
    nb8 = t // SUBLANES
    r8 = tm // SUBLANES
    spec_o = pl.BlockSpec((tm, w), lambda i: (i, 0))
    return pl.pallas_call(
        functools.partial(_gdn_prep_kernel, tm=tm, n_ctx=n_ctx, t_total=t),
        grid=(t // tm,),
        in_specs=[pl.BlockSpec((tm, GDN_QKV), lambda i: (i, qkv_blk)),
                  pl.BlockSpec((SUBLANES, GDN_QKV), lambda i: (jnp.maximum(i * r8 - 1, 0), qkv_blk)),
                  pl.BlockSpec((SUBLANES, GDN_QKV), lambda i: (jnp.minimum((i + 1) * r8, nb8 - 1), qkv_blk)),
                  pl.BlockSpec((3, GDN_QKV), lambda i: (0, 0)),
                  pl.BlockSpec((tm, LANES), lambda i: (i, 7168 // LANES)),
                  pl.BlockSpec((2, LANES), lambda i: (0, 0))],
        out_specs=[spec_o, spec_o, spec_o, pl.BlockSpec((tm, LANES), lambda i: (i, 0))],
        out_shape=[jax.ShapeDtypeStruct((t, w), BF16)] * 3 + [jax.ShapeDtypeStruct((t, LANES), F32)],
        compiler_params=_cparams(1),
        name="gdn_conv_norm_gates",
    )(p, p, p, conv_w, p, gate_params)


def _gdn_kernel(*refs, rev, final, c):
    if final:
        q_ref, k_ref, v_ref, gb_ref, gate_ref, ng_ref, ob_ref, o_ref, st_scr = refs
    else:
        q_ref, k_ref, v_ref, gb_ref, o_ref, st_scr = refs
    d = 1 if rev else 0
    dk = GDN_DK
    nh = GDN_HEADS

    @pl.when(pl.program_id(0) == 0)
    def _():
        st_scr[...] = jnp.zeros_like(st_scr)

    incl, strict = _order_masks(c, rev)
    keep = 0 if rev else c - 1
    gb = gb_ref[...]
    gc = _dot_mask_lhs(jnp.where(incl, 1.0, 0.0).astype(BF16), gb)
    gc_t = gc.T

    for h0 in range(0, nh, GDN_GROUP):
        hs = list(range(h0, h0 + GDN_GROUP))
        sls = [slice(h * dk, (h + 1) * dk) for h in hs]
        cols = [16 + d * nh + h for h in hs]
        g_col = [gc[:, cc:cc + 1] for cc in cols]
        beta = [gb[:, cc + 16:cc + 17] for cc in cols]
        g_end = [g[keep:keep + 1, :] for g in g_col]
        e_g = [jnp.exp(g) for g in g_col]
        decay = [jnp.exp(jnp.where(incl, g - gc_t[cc:cc + 1, :], -jnp.inf)) for g, cc in zip(g_col, cols)]
        qs = [q_ref[:, sl].astype(F32) for sl in sls]
        ks = [k_ref[:, sl].astype(F32) for sl in sls]
        vs = [v_ref[:, sl].astype(F32) for sl in sls]
        kq = [_bdot_nt(jnp.concatenate([kh * b, qh], axis=0), kh) for kh, qh, b in zip(ks, qs, beta)]
        a_qk = [x[c:] * dc for x, dc in zip(kq, decay)]
        tinv = _neumann_inverse_multi([jnp.where(strict, -(x[:c] * dc), 0.0) for x, dc in zip(kq, decay)], c, _dot_inv)
        uw = [_dot_inv(t, jnp.concatenate([vh * b, kh * (b * e)], axis=1))
              for t, vh, kh, b, e in zip(tinv, vs, ks, beta, e_g)]
        o_part = [_bdot(a, u) for a, u in zip(a_qk, uw)]
        s_part = [_bdot_tn(kh * jnp.exp(ge - g), u) for kh, ge, g, u in zip(ks, g_end, g_col, uw)]
        so = [_dot_state(jnp.concatenate([qh * e - op[:, dk:], sp[:, dk:]], axis=0), st_scr[h])
              for qh, e, op, sp, h in zip(qs, e_g, o_part, s_part, hs)]
        for i, h in enumerate(hs):
            sl = sls[i]
            out = so[i][:c] + o_part[i][:, :dk]
            st_scr[h] = jnp.exp(g_end[i]) * st_scr[h] - so[i][c:] + s_part[i][:, :dk]
            if final:
                tot = out + ob_ref[:, sl]
                y = tot * lax.rsqrt(jnp.mean(tot * tot, axis=-1, keepdims=True) + EPS)
                o_ref[:, sl] = (y * ng_ref[0:1, sl] * _silu(gate_ref[:, sl])).astype(o_ref.dtype)
            else:
                o_ref[:, sl] = out


def _gdn_call(p, q, k, v, gb, norm_g, other, rev, n_ctx):
    t = q.shape[0]
    c = GDN_CHUNK
    n, nc = t // c, n_ctx // c
    w = GDN_HEADS * GDN_DK
    final = other is not None
    cm = lambda s: _chunk_index(s, rev, nc, n)
    blk = pl.BlockSpec((c, w), lambda s: (cm(s), 0))
    in_specs = [blk, blk, blk, pl.BlockSpec((c, LANES), lambda s: (cm(s), 0))]
    args = [q, k, v, gb]
    if final:
        in_specs += [pl.BlockSpec((c, w), lambda s: (cm(s), 6144 // w)), pl.BlockSpec((1, w), lambda s: (0, 0)), blk]
        args += [p, norm_g.reshape(1, w), other]
    return pl.pallas_call(
        functools.partial(_gdn_kernel, rev=rev, final=final, c=c),
        grid=(n,),
        in_specs=in_specs,
        out_specs=blk,
        out_shape=jax.ShapeDtypeStruct((t, w), BF16 if final else F32),
        scratch_shapes=[pltpu.VMEM((GDN_HEADS, GDN_DK, GDN_DK), F32)],
        compiler_params=_cparams(1),
        name="gdn_bwd" if rev else "gdn_fwd_merge",
    )(*args)


def _pad_cols(w, width):
    return jnp.pad(w, ((0, 0), (0, width - w.shape[1])))


def _lane_row(pieces):
    row = jnp.zeros((LANES,), F32)
    for off, vec in pieces:
        row = row.at[off:off + vec.shape[0]].set(vec.astype(F32))
    return row


def _odd_weight(w):
    cols = [w[:, 0:3072], w[:, 3088:6160], w[:, 6160:7184], w[:, 3072:3088], w[:, 7184:7216]]
    return _pad_cols(jnp.concatenate(cols, axis=1), F_PAD).astype(BF16)


def _rwkv_params(e, rwkv_mu, rwkv_w0, rwkv_w_up, rwkv_a0, rwkv_a_up, rwkv_k_k, rwkv_k_a, rwkv_r_k,
                 rwkv_ln_g, rwkv_ln_b, rwkv_g_up):
    wdt = RWKV_W
    mu = rwkv_mu[e]
    zeros64 = jnp.zeros((RWKV_LORA,), F32)
    mu_small, vecs = [], []
    for d in range(2):
        m_wd = mu[d, 3 * wdt:3 * wdt + RWKV_LORA]
        m_ad = mu[d, 3 * wdt + RWKV_LORA:]
        lo = [m_wd, zeros64] if d == 0 else [zeros64, m_wd]
        la = [m_ad, zeros64] if d == 0 else [zeros64, m_ad]
        mu_small.append(jnp.stack([jnp.concatenate(lo), jnp.concatenate(la)]))
        vecs.append(jnp.stack([rwkv_w0[e, d], rwkv_a0[e, d], rwkv_k_k[e], rwkv_k_a[e], rwkv_r_k[e].reshape(wdt),
                               rwkv_ln_g[e], rwkv_ln_b[e], jnp.zeros((wdt,), F32)]))
    return {
        "mu": mu[:, None, :3 * wdt],
        "mu_small": jnp.stack(mu_small),
        "vecs": jnp.stack(vecs),
        "w_up": rwkv_w_up[e].astype(BF16),
        "a_up": rwkv_a_up[e].astype(BF16),
        "g_up": jnp.pad(rwkv_g_up[e], ((0, 2 * LANES - RWKV_GATE_LORA), (0, 0))).astype(BF16),
    }


def _even_mixer(p, e, n_ctx, ret_decay_logit, rwkv_prm):
    logit = ret_decay_logit[e].astype(F32)
    ret_b = _retention_call(p, logit, None, True, n_ctx)
    y_ret = _retention_call(p, logit, ret_b, False, n_ctx)
    rw_b = _rwkv_call(p, rwkv_prm, None, True, n_ctx)
    y_rwkv = _rwkv_call(p, rwkv_prm, rw_b, False, n_ctx)
    return y_ret, y_rwkv


def _odd_mixer(p, o, n_ctx, mlstm_gate_b, mlstm_norm_g, gdn_conv, gdn_a_log, gdn_dt_bias, gdn_norm_g):
    gate_b = mlstm_gate_b[o]
    nh = MLSTM_HEADS
    bias_i = _lane_row([(d * 2 * nh, gate_b[d, 0]) for d in range(2)])
    bias_f = _lane_row([(d * 2 * nh + nh, gate_b[d, 1]) for d in range(2)])
    bias_rows = jnp.stack([bias_i, bias_f])
    ml_b = _mlstm_call(p, bias_rows, mlstm_norm_g[o], None, True, n_ctx)
    y_ml = _mlstm_call(p, bias_rows, mlstm_norm_g[o], ml_b, False, n_ctx)

    gate_params = jnp.stack([_lane_row([(16, gdn_a_log[o].reshape(-1))]), _lane_row([(16, gdn_dt_bias[o].reshape(-1))])])
    q, k, v, gb = _gdn_prep_call(p, gdn_conv[o], gate_params, n_ctx)
    gd_b = _gdn_call(p, q, k, v, gb, gdn_norm_g[o], None, True, n_ctx)
    y_gd = _gdn_call(p, q, k, v, gb, gdn_norm_g[o], gd_b, False, n_ctx)
    return y_ml, y_gd


def kernel(x, c, ctx, c_ctx, ada_w, ada_b, norm1_g, norm2_g, mix_w_out, ffn_w_in, ffn_dw, ffn_dw_b, ffn_w_out, final_norm_g, ev_w_in, ret_decay_logit, rwkv_mu, rwkv_w0, rwkv_w_up, rwkv_a0, rwkv_a_up, rwkv_k_k, rwkv_k_a, rwkv_r_k, rwkv_ln_g, rwkv_ln_b, rwkv_g_up, od_w_in, mlstm_gate_b, mlstm_norm_g, gdn_conv, gdn_a_log, gdn_dt_bias, gdn_norm_g):
    assert x.shape[0] == 1 and ctx.shape[0] == 1
    n_ctx = ctx.shape[1]
    n_lat = x.shape[1]
    depth = ada_w.shape[0]
    xt = jnp.concatenate([ctx[0], x[0]], axis=0)
    cvecs = jnp.stack([c_ctx, c[0]], axis=1)
    mod = _ada_call(cvecs, ada_w, ada_b)

    for l in range(depth):
        last = l == depth - 1
        half = D_MODEL // 2
        if l % 2 == 0:
            e = l // 2
            w_in = _pad_cols(ev_w_in[e], F_PAD).astype(BF16)
            p = _proj_call(xt, norm1_g[l], mod, l, w_in, n_ctx)
            prm = _rwkv_params(e, rwkv_mu, rwkv_w0, rwkv_w_up, rwkv_a0, rwkv_a_up, rwkv_k_k, rwkv_k_a, rwkv_r_k,
                               rwkv_ln_g, rwkv_ln_b, rwkv_g_up)
            ya, yb = _even_mixer(p, e, n_ctx, ret_decay_logit, prm)
        else:
            o = l // 2
            p = _proj_call(xt, norm1_g[l], mod, l, _odd_weight(od_w_in[o]), n_ctx)
            ya, yb = _odd_mixer(p, o, n_ctx, mlstm_gate_b, mlstm_norm_g, gdn_conv, gdn_a_log, gdn_dt_bias, gdn_norm_g)
        w_mix = mix_w_out[l].astype(BF16)
        xt, h2 = _outproj_call(ya, yb, w_mix[:half], w_mix[half:], xt, norm2_g[l], mod, l, n_ctx)
        wg = _pad_cols(ffn_w_in[l][:, :D_FF], FF_PAD).astype(BF16)
        wu = _pad_cols(ffn_w_in[l][:, D_FF:], FF_PAD).astype(BF16)
        gate, up = _ffn_in_call(h2, wg, wu)
        dw9 = _pad_cols(ffn_dw[l].reshape(9, D_FF), FF_PAD)
        dwb = _pad_cols(ffn_dw_b[l].reshape(1, D_FF), FF_PAD)
        wo = jnp.pad(ffn_w_out[l], ((0, FF_PAD - D_FF), (0, 0))).astype(BF16)
        xt = _ffn_out_call(gate, up, dw9, dwb, wo, xt, mod, l, final_norm_g, n_ctx, last)
    return xt[n_ctx:][None]
```

```python
import functools
import math

import jax
import jax.numpy as jnp
from jax import lax
from jax.experimental import pallas as pl
from jax.experimental.pallas import tpu as pltpu

F32 = jnp.float32
BF16 = jnp.bfloat16

D_MODEL = 2048
DEPTH = 2
GRID_W = 64
EPS = 1e-6
GN_EPS = 64e-5
L2_EPS = 1e-12
D_FF = 5504

RET_HEADS = 8
RET_DK = 128
RWKV_HEADS = 16
RWKV_N = 64
RWKV_W = RWKV_HEADS * RWKV_N
RWKV_LORA = 64
RWKV_GATE_LORA = 160
MLSTM_HEADS = 4
MLSTM_DK = 128
MLSTM_DV = 256
GDN_HEADS = 8
GDN_DK = 128
GDN_QKV = 3072

LANES = 128
SUBLANES = 8
VMEM_LIMIT = 56 * 1024 * 1024

F_PAD = 7680
FF_PAD = 5632
ROW_TILE = 768
RET_CHUNK = 256
MLSTM_CHUNK = 256
GDN_CHUNK = 64
RWKV_CHUNK = 64
RWKV_GROUP = 8
GDN_GROUP = 8


def _cparams(n_axes):
    return pltpu.CompilerParams(dimension_semantics=("arbitrary",) * n_axes, vmem_limit_bytes=VMEM_LIMIT)


def _bdot(a, b):
    return jnp.dot(a.astype(BF16), b.astype(BF16), preferred_element_type=F32)


def _bdot_nt(a, b):
    return lax.dot_general(a.astype(BF16), b.astype(BF16), (((1,), (1,)), ((), ())), preferred_element_type=F32)


def _bdot_tn(a, b):
    return lax.dot_general(a.astype(BF16), b.astype(BF16), (((0,), (0,)), ((), ())), preferred_element_type=F32)


def _split3(x):
    x1 = x.astype(BF16)
    r1 = x - x1.astype(F32)
    x2 = r1.astype(BF16)
    x3 = (r1 - x2.astype(F32)).astype(BF16)
    return x1, x2, x3


def _dot_mask_lhs(m_bf16, x):
    x1, x2, x3 = _split3(x)
    d = lambda t: jnp.dot(m_bf16, t, preferred_element_type=F32)
    return (d(x3) + d(x2)) + d(x1)


def _dot3(a, b):
    a1 = a.astype(BF16)
    a2 = (a - a1.astype(F32)).astype(BF16)
    b1 = b.astype(BF16)
    b2 = (b - b1.astype(F32)).astype(BF16)
    d = lambda u, v: jnp.dot(u, v, preferred_element_type=F32)
    return (d(a2, b1) + d(a1, b2)) + d(a1, b1)


def _sigmoid(x):
    return 1.0 / (1.0 + jnp.exp(-x))


def _silu(x):
    return x * _sigmoid(x)


def _softplus(x):
    return jnp.maximum(x, 0.0) + jnp.log1p(jnp.exp(-jnp.abs(x)))


def _log_sigmoid(x):
    return -_softplus(-x)


def _order_masks(c, rev):
    i = lax.broadcasted_iota(jnp.int32, (c, c), 0)
    j = lax.broadcasted_iota(jnp.int32, (c, c), 1)
    if rev:
        return j >= i, j > i
    return j <= i, j < i


def _neumann_inverse(n, c, dotf):
    i = lax.broadcasted_iota(jnp.int32, (c, c), 0)
    j = lax.broadcasted_iota(jnp.int32, (c, c), 1)
    x = jnp.where(i == j, 1.0, 0.0).astype(F32) + n
    p = dotf(n, n)
    for _ in range(int(math.log2(c)) - 2):
        r = dotf(jnp.concatenate([x, p], axis=0), p)
        x = x + r[:c]
        p = r[c:]
    return x + dotf(x, p)


def _neumann_inverse_multi(ns, c, dotf):
    i = lax.broadcasted_iota(jnp.int32, (c, c), 0)
    j = lax.broadcasted_iota(jnp.int32, (c, c), 1)
    eye = jnp.where(i == j, 1.0, 0.0).astype(F32)
    xs = [eye + n for n in ns]
    ps = [dotf(n, n) for n in ns]
    for _ in range(int(math.log2(c)) - 2):
        rs = [dotf(jnp.concatenate([x, p], axis=0), p) for x, p in zip(xs, ps)]
        xs = [x + r[:c] for x, r in zip(xs, rs)]
        ps = [r[c:] for r in rs]
    return [x + dotf(x, p) for x, p in zip(xs, ps)]


def _dot_inv(a, b):
    return _dot3(a, b)


def _dot_state(a, b):
    return _bdot(a, b)


def _chunk_index(s, rev, n_ctx_chunks, n_chunks):
    if not rev:
        return s
    return jnp.where(s < n_ctx_chunks, n_ctx_chunks - 1 - s, n_chunks + n_ctx_chunks - 1 - s)


def _seg_sum(x, seg):
    c, w = x.shape
    if seg == LANES:
        parts = [jnp.broadcast_to(jnp.sum(x[:, b * LANES:(b + 1) * LANES], axis=-1, keepdims=True), (c, LANES))
                 for b in range(w // LANES)]
        return jnp.concatenate(parts, axis=-1)
    assert seg * 2 == LANES
    lane = lax.broadcasted_iota(jnp.int32, (c, LANES), 1)
    low = lane < seg
    parts = []
    for b in range(w // LANES):
        xb = x[:, b * LANES:(b + 1) * LANES]
        s_lo = jnp.sum(jnp.where(low, xb, 0.0), axis=-1, keepdims=True)
        s_hi = jnp.sum(jnp.where(low, 0.0, xb), axis=-1, keepdims=True)
        parts.append(jnp.where(low, s_lo, s_hi))
    return jnp.concatenate(parts, axis=-1)


def _ada_kernel(c_ref, w_ref, b_ref, o_ref):
    cv = c_ref[...]
    s = _silu(cv)
    w = w_ref[0]
    bias = b_ref[0]
    r0 = jnp.sum(w * s[:, 0:1], axis=0, keepdims=True) + bias
    r1 = jnp.sum(w * s[:, 1:2], axis=0, keepdims=True) + bias
    row = lax.broadcasted_iota(jnp.int32, (SUBLANES, w.shape[1]), 0)
    o_ref[0] = jnp.where(row == 0, r0, jnp.where(row == 1, r1, 0.0))


def _ada_call(cvecs, ada_w, ada_b):
    depth, d, n = ada_w.shape
    tn = 1024
    return pl.pallas_call(
        _ada_kernel,
        grid=(depth, n // tn),
        in_specs=[pl.BlockSpec((d, 2), lambda l, j: (0, 0)),
                  pl.BlockSpec((1, d, tn), lambda l, j: (l, 0, j)),
                  pl.BlockSpec((1, 1, tn), lambda l, j: (l, 0, j))],
        out_specs=pl.BlockSpec((1, SUBLANES, tn), lambda l, j: (l, 0, j)),
        out_shape=jax.ShapeDtypeStruct((depth, SUBLANES, n), F32),
        compiler_params=_cparams(2),
        name="ada_modulation",
    )(cvecs, ada_w, ada_b.reshape(depth, 1, n))


def _mod_rows(mod, k, rows, n_ctx):
    d = D_MODEL
    vc = mod[0:1, k * d:(k + 1) * d]
    vl = mod[1:2, k * d:(k + 1) * d]
    return jnp.where(rows < n_ctx, vc, vl)


def _norm_mod(x, g, shift, scale):
    y = x * lax.rsqrt(jnp.mean(x * x, axis=-1, keepdims=True) + EPS) * g
    return y * (1.0 + scale) + shift


def _assemble_kernel(ctx_ref, x_ref, g_ref, mod_ref, xt_ref, h_ref):
    i = pl.program_id(0)
    d = D_MODEL
    mod = mod_ref[0]

    def emit(src, row):
        xt_ref[...] = src
        h = _norm_mod(src, g_ref[...], mod[row:row + 1, 0:d], mod[row:row + 1, d:2 * d])
        h_ref[...] = h.astype(BF16)

    @pl.when(i == 0)
    def _():
        emit(ctx_ref[0], 0)

    @pl.when(i > 0)
    def _():
        emit(x_ref[0], 1)


def _assemble_call(ctx, x, g, mod):
    n_ctx, d = ctx.shape[1], ctx.shape[2]
    t = n_ctx + x.shape[1]
    assert x.shape[1] % n_ctx == 0
    spec_o = pl.BlockSpec((n_ctx, d), lambda i: (i, 0))
    return pl.pallas_call(
        _assemble_kernel,
        grid=(t // n_ctx,),
        in_specs=[pl.BlockSpec((1, n_ctx, d), lambda i: (0, 0, 0)),
                  pl.BlockSpec((1, n_ctx, d), lambda i: (0, jnp.maximum(i - 1, 0), 0)),
                  pl.BlockSpec((1, d), lambda i: (0, 0)),
                  pl.BlockSpec((1, SUBLANES, 6 * d), lambda i: (0, 0, 0))],
        out_specs=[spec_o, spec_o],
        out_shape=[jax.ShapeDtypeStruct((t, d), F32), jax.ShapeDtypeStruct((t, d), BF16)],
        compiler_params=_cparams(1),
        name="assemble_norm1",
    )(ctx, x, g.reshape(1, d), mod)


def _proj_kernel(h_ref, w_ref, o_ref, w_scr, *, tn, n_valid):
    @pl.when(pl.program_id(1) == 0)
    def _():
        w = w_ref[...]
        col = pl.program_id(0) * tn + lax.broadcasted_iota(jnp.int32, w.shape, 1)
        w_scr[...] = jnp.where(col < n_valid, w, 0).astype(BF16)

    o_ref[...] = jnp.dot(h_ref[...], w_scr[...], preferred_element_type=F32)


def _proj_call(h, w, f_pad):
    t, d = h.shape
    tm, tn = ROW_TILE, 768
    return pl.pallas_call(
        functools.partial(_proj_kernel, tn=tn, n_valid=w.shape[1]),
        grid=(f_pad // tn, t // tm),
        in_specs=[pl.BlockSpec((tm, d), lambda j, i: (i, 0)),
                  pl.BlockSpec((d, tn), lambda j, i: (0, j))],
        out_specs=pl.BlockSpec((tm, tn), lambda j, i: (i, j)),
        out_shape=jax.ShapeDtypeStruct((t, f_pad), F32),
        scratch_shapes=[pltpu.VMEM((d, tn), BF16)],
        compiler_params=_cparams(2),
        name="in_proj",
    )(h, w)


def _outproj_kernel(ya_ref, yb_ref, wa_ref, wb_ref, x_ref, g_ref, mod_ref, xo_ref, h_ref, *, tm, n_ctx):
    i = pl.program_id(0)
    rows = i * tm + lax.broadcasted_iota(jnp.int32, (tm, 1), 0)
    mod = mod_ref[0]
    acc = jnp.dot(ya_ref[...], wa_ref[...], preferred_element_type=F32)
    acc = acc + jnp.dot(yb_ref[...], wb_ref[...], preferred_element_type=F32)
    xn = x_ref[...] + _mod_rows(mod, 2, rows, n_ctx) * acc
    xo_ref[...] = xn
    h = _norm_mod(xn, g_ref[...], _mod_rows(mod, 3, rows, n_ctx), _mod_rows(mod, 4, rows, n_ctx))
    h_ref[...] = h.astype(BF16)


def _outproj_call(ya, yb, wa, wb, x, g2, mod, layer, n_ctx):
    t, d = x.shape
    half = ya.shape[1]
    tm = 384
    return pl.pallas_call(
        functools.partial(_outproj_kernel, tm=tm, n_ctx=n_ctx),
        grid=(t // tm,),
        in_specs=[pl.BlockSpec((tm, half), lambda i: (i, 0)),
                  pl.BlockSpec((tm, half), lambda i: (i, 0)),
                  pl.BlockSpec((half, d), lambda i: (0, 0)),
                  pl.BlockSpec((half, d), lambda i: (0, 0)),
                  pl.BlockSpec((tm, d), lambda i: (i, 0)),
                  pl.BlockSpec((1, d), lambda i: (0, 0)),
                  pl.BlockSpec((1, SUBLANES, 6 * d), lambda i: (layer, 0, 0))],
        out_specs=[pl.BlockSpec((tm, d), lambda i: (i, 0)),
                   pl.BlockSpec((tm, d), lambda i: (i, 0))],
        out_shape=[jax.ShapeDtypeStruct((t, d), F32), jax.ShapeDtypeStruct((t, d), BF16)],
        compiler_params=_cparams(1),
        name="mix_out_proj_norm2",
    )(ya, yb, wa, wb, x, g2.reshape(1, d), mod)


def _ffn_in_kernel(h_ref, wg_ref, wu0_ref, wu1_ref, wu2_ref, wu3_ref, g_ref, u_ref, w_scr, *, tn):
    @pl.when(pl.program_id(1) == 0)
    def _():
        wu = jnp.concatenate([wu0_ref[0], wu1_ref[0], wu2_ref[0], wu3_ref[0]], axis=1)
        col = pl.program_id(0) * tn + lax.broadcasted_iota(jnp.int32, wu.shape, 1)
        w_scr[0] = jnp.where(col < D_FF, wg_ref[0], 0.0).astype(BF16)
        w_scr[1] = jnp.where(col < D_FF, wu, 0.0).astype(BF16)

    h = h_ref[...]
    g_ref[...] = jnp.dot(h, w_scr[0], preferred_element_type=F32).astype(BF16)
    u_ref[...] = jnp.dot(h, w_scr[1], preferred_element_type=F32).astype(BF16)


def _ffn_in_call(h, ffn_w_in, layer):
    t, d = h.shape
    tm, tn = ROW_TILE, 512
    q = tn // LANES
    assert D_FF % LANES == 0 and q == 4
    up0 = D_FF // LANES
    last = 2 * D_FF // LANES - 1
    spec_o = pl.BlockSpec((tm, tn), lambda j, i: (i, j))
    up_specs = [pl.BlockSpec((1, d, LANES), lambda j, i, r=r: (layer, 0, jnp.minimum(up0 + q * j + r, last)))
                for r in range(q)]
    return pl.pallas_call(
        functools.partial(_ffn_in_kernel, tn=tn),
        grid=(FF_PAD // tn, t // tm),
        in_specs=[pl.BlockSpec((tm, d), lambda j, i: (i, 0)),
                  pl.BlockSpec((1, d, tn), lambda j, i: (layer, 0, j))] + up_specs,
        out_specs=[spec_o, spec_o],
        out_shape=[jax.ShapeDtypeStruct((t, FF_PAD), BF16)] * 2,
        scratch_shapes=[pltpu.VMEM((2, d, tn), BF16)],
        compiler_params=_cparams(2),
        name="ffn_in_proj",
    )(h, ffn_w_in, ffn_w_in, ffn_w_in, ffn_w_in, ffn_w_in)


def _ffn_out_kernel(*refs, tm, tk, n_ctx, t_total, final_norm):
    if final_norm:
        gm_ref, gp_ref, gn_ref, u_ref, dw_ref, db_ref, wo_ref, x_ref, mod_ref, fg_ref, o_ref, acc_scr = refs
    else:
        (gm_ref, gp_ref, gn_ref, u_ref, dw_ref, db_ref, wo_ref, x_ref, mod_ref, fg_ref, nmod_ref,
         o_ref, hn_ref, acc_scr) = refs
    i = pl.program_id(0)
    k = pl.program_id(1)
    w = GRID_W

    @pl.when(k == 0)
    def _():
        acc_scr[...] = jnp.zeros_like(acc_scr)

    g_all = jnp.concatenate([gp_ref[...], gm_ref[...], gn_ref[...]], axis=0).astype(F32)
    n_all = tm + 2 * w
    rows = (i * tm - w) + lax.broadcasted_iota(jnp.int32, (n_all, tk), 0)
    is_ctx = rows < n_ctx
    col = jnp.bitwise_and(rows, w - 1)
    left_ok = jnp.where(is_ctx, rows, col) > 0
    right_ok = jnp.where(is_ctx, rows - (n_ctx - 1), col - (w - 1)) < 0
    g_left = jnp.where(left_ok, pltpu.roll(g_all, 1, axis=0), 0.0)
    g_right = jnp.where(right_ok, pltpu.roll(g_all, n_all - 1, axis=0), 0.0)
    dw = dw_ref[...]

    def taps(kh, lo):
        sl = slice(lo, lo + tm)
        return (g_left[sl] * dw[3 * kh:3 * kh + 1] + g_all[sl] * dw[3 * kh + 1:3 * kh + 2]
                + g_right[sl] * dw[3 * kh + 2:3 * kh + 3])

    rows_m = i * tm + lax.broadcasted_iota(jnp.int32, (tm, tk), 0)
    up_ok = rows_m >= n_ctx + w
    down_ok = jnp.where(rows_m >= n_ctx, rows_m, t_total) < t_total - w
    conv = taps(1, w) + jnp.where(up_ok, taps(0, 0), 0.0) + jnp.where(down_ok, taps(2, 2 * w), 0.0)
    gate = conv + db_ref[...]
    act = 0.5 * gate * (1.0 + lax.erf(gate * (2.0 ** -0.5)))
    act = act * u_ref[...].astype(F32)
    wo = wo_ref[0]
    wrow = lax.broadcasted_iota(jnp.int32, wo.shape, 0)
    wo = jnp.where(wrow < D_FF - k * tk, wo, 0.0).astype(BF16)
    acc_scr[...] += jnp.dot(act.astype(BF16), wo, preferred_element_type=F32)

    @pl.when(k == pl.num_programs(1) - 1)
    def _():
        rws = i * tm + lax.broadcasted_iota(jnp.int32, (tm, 1), 0)
        xn = x_ref[...] + _mod_rows(mod_ref[0], 5, rws, n_ctx) * acc_scr[...]
        if final_norm:
            xn = xn * lax.rsqrt(jnp.mean(xn * xn, axis=-1, keepdims=True) + EPS) * fg_ref[...]
        else:
            nmod = nmod_ref[0]
            hn = _norm_mod(xn, fg_ref[...], _mod_rows(nmod, 0, rws, n_ctx), _mod_rows(nmod, 1, rws, n_ctx))
            hn_ref[...] = hn.astype(BF16)
        o_ref[...] = xn


def _ffn_out_call(gate, up, dw9, dwb, ffn_w_out, x, mod, layer, norm_g, n_ctx, final_norm):
    t, d = x.shape
    fp = gate.shape[1]
    tm, tk, w = ROW_TILE // 2, 512, GRID_W
    rpt = tm // w
    n_rows = t // w
    spec_x = pl.BlockSpec((tm, d), lambda i, k: (i, 0))
    extra_in, extra_args = [], []
    out_specs, out_shape = spec_x, jax.ShapeDtypeStruct((t, d), F32)
    if not final_norm:
        extra_in = [pl.BlockSpec((1, SUBLANES, 6 * d), lambda i, k: (layer + 1, 0, 0))]
        extra_args = [mod]
        out_specs = [spec_x, spec_x]
        out_shape = [out_shape, jax.ShapeDtypeStruct((t, d), BF16)]
    return pl.pallas_call(
        functools.partial(_ffn_out_kernel, tm=tm, tk=tk, n_ctx=n_ctx, t_total=t, final_norm=final_norm),
        grid=(t // tm, fp // tk),
        in_specs=[pl.BlockSpec((tm, tk), lambda i, k: (i, k)),
                  pl.BlockSpec((w, tk), lambda i, k: (jnp.maximum(i * rpt - 1, 0), k)),
                  pl.BlockSpec((w, tk), lambda i, k: (jnp.minimum((i + 1) * rpt, n_rows - 1), k)),
                  pl.BlockSpec((tm, tk), lambda i, k: (i, k)),
                  pl.BlockSpec((9, tk), lambda i, k: (0, k)),
                  pl.BlockSpec((1, tk), lambda i, k: (0, k)),
                  pl.BlockSpec((1, tk, d), lambda i, k: (layer, k, 0)),
                  spec_x,
                  pl.BlockSpec((1, SUBLANES, 6 * d), lambda i, k: (layer, 0, 0)),
                  pl.BlockSpec((1, d), lambda i, k: (0, 0))] + extra_in,
        out_specs=out_specs,
        out_shape=out_shape,
        scratch_shapes=[pltpu.VMEM((tm, d), F32)],
        compiler_params=_cparams(2),
        name="ffn_conv_glu_out",
    )(gate, gate, gate, up, dw9, dwb, ffn_w_out, x, mod, norm_g.reshape(1, d), *extra_args)


def _retention_kernel(*refs, rev, final, c):
    if final:
        q_ref, k_ref, v_ref, lg_ref, gate_ref, ob_ref, o_ref, r_scr, di_scr, dq_scr, dk_scr, dc_scr = refs
    else:
        q_ref, k_ref, v_ref, lg_ref, o_ref, r_scr, di_scr, dq_scr, dk_scr, dc_scr = refs
    d = 1 if rev else 0
    dk = RET_DK

    @pl.when(pl.program_id(0) == 0)
    def _():
        r_scr[...] = jnp.zeros_like(r_scr)
        i = lax.broadcasted_iota(jnp.int32, (c, c), 0)
        j = lax.broadcasted_iota(jnp.int32, (c, c), 1)
        diff = ((j - i) if rev else (i - j)).astype(F32)
        row = lax.broadcasted_iota(jnp.int32, (c, dk), 0)
        pos = ((c - 1 - row) if rev else row).astype(F32)
        for h in range(RET_HEADS):
            lg = _log_sigmoid(lg_ref[d:d + 1, h:h + 1])
            di_scr[h] = jnp.where(diff >= 0, jnp.exp(lg * jnp.maximum(diff, 0.0)), 0.0)
            dq_scr[h] = jnp.exp(lg * (pos + 1.0))
            dk_scr[h] = jnp.exp(lg * (c - 1.0 - pos)) * (dk ** -0.5)
            dc_scr[h] = jnp.exp(jnp.broadcast_to(lg, (1, dk)) * c)

    for h in range(RET_HEADS):
        sl = slice(h * dk, (h + 1) * dk)
        qh = q_ref[:, sl]
        kh = k_ref[:, sl]
        vh = v_ref[:, sl].astype(BF16)
        r_state = r_scr[h]
        scores = _bdot_nt(qh, kh) * (di_scr[h] * (dk ** -0.5))
        out = _bdot(scores, vh) + _bdot(qh * dq_scr[h], r_state)
        r_scr[h] = dc_scr[h] * r_state + _bdot_tn(kh * dk_scr[h], vh)
        if final:
            tot = out + ob_ref[:, sl]
            y = tot * lax.rsqrt(jnp.mean(tot * tot, axis=-1, keepdims=True) + EPS) * _silu(gate_ref[:, sl])
            o_ref[:, sl] = y.astype(o_ref.dtype)
        else:
            o_ref[:, sl] = out


def _retention_call(p, logit, other, rev, n_ctx):
    t = p.shape[0]
    c = RET_CHUNK
    n, nc = t // c, n_ctx // c
    width = RET_HEADS * RET_DK
    final = other is not None
    cm = lambda s: _chunk_index(s, rev, nc, n)
    in_specs = [pl.BlockSpec((c, width), lambda s: (cm(s), 0)),
                pl.BlockSpec((c, width), lambda s: (cm(s), 1)),
                pl.BlockSpec((c, width), lambda s: (cm(s), 2)),
                pl.BlockSpec((2, RET_HEADS), lambda s: (0, 0))]
    args = [p, p, p, logit]
    if final:
        in_specs += [pl.BlockSpec((c, width), lambda s: (cm(s), 3)), pl.BlockSpec((c, width), lambda s: (cm(s), 0))]
        args += [p, other]
    return pl.pallas_call(
        functools.partial(_retention_kernel, rev=rev, final=final, c=c),
        grid=(n,),
        in_specs=in_specs,
        out_specs=pl.BlockSpec((c, width), lambda s: (cm(s), 0)),
        out_shape=jax.ShapeDtypeStruct((t, width), BF16 if final else F32),
        scratch_shapes=[pltpu.VMEM((RET_HEADS, RET_DK, RET_DK), F32),
                        pltpu.VMEM((RET_HEADS, c, c), F32),
                        pltpu.VMEM((RET_HEADS, c, RET_DK), F32),
                        pltpu.VMEM((RET_HEADS, c, RET_DK), F32),
                        pltpu.VMEM((RET_HEADS, 1, RET_DK), F32)],
        compiler_params=_cparams(1),
        name="retention_bwd" if rev else "retention_fwd_merge",
    )(*args)


def _rwkv_kernel(*refs, rev, final, c, nc):
    if final:
        (r_ref, k_ref, v_ref, wd_ref, ad_ref, mu_ref, mus_ref, vec_ref, wup_ref, aup_ref,
         gl_ref, gup_ref, yb_ref, bb_ref, o_ref, st_scr, carry_scr) = refs
    else:
        (r_ref, k_ref, v_ref, wd_ref, ad_ref, mu_ref, mus_ref, vec_ref, wup_ref, aup_ref,
         y_ref, bonus_ref, st_scr, carry_scr) = refs
    d = 1 if rev else 0
    s = pl.program_id(0)
    n = RWKV_N
    wdt = RWKV_W

    @pl.when(s == 0)
    def _():
        st_scr[...] = jnp.zeros_like(st_scr)

    @pl.when(jnp.logical_or(s == 0, s == nc))
    def _():
        carry_scr[...] = jnp.zeros_like(carry_scr)

    row = lax.broadcasted_iota(jnp.int32, (c, 1), 0)
    edge = (row == c - 1) if rev else (row == 0)
    keep = c - 1 if not rev else 0

    def shifted(x, lo):
        width = x.shape[1]
        prev = pltpu.roll(x, (c - 1) if rev else 1, axis=0)
        prev = jnp.where(edge, carry_scr[0:1, lo:lo + width], prev)
        carry_scr[0:1, lo:lo + width] = x[keep:keep + 1, :]
        return prev

    def mix(x, lo, mu):
        prev = shifted(x, lo)
        return x + (prev - x) * mu

    r = mix(r_ref[...], 0, mu_ref[0:1, 0:wdt])
    k = mix(k_ref[...], wdt, mu_ref[0:1, wdt:2 * wdt])
    v = mix(v_ref[...], 2 * wdt, mu_ref[0:1, 2 * wdt:3 * wdt])
    wd = mix(wd_ref[...], 3 * wdt, mus_ref[0:1, :])[:, d * RWKV_LORA:(d + 1) * RWKV_LORA]
    ad = mix(ad_ref[...], 3 * wdt + LANES, mus_ref[1:2, :])[:, d * RWKV_LORA:(d + 1) * RWKV_LORA]

    w0, a0 = vec_ref[0:1, :], vec_ref[1:2, :]
    k_k, k_a, r_k = vec_ref[2:3, :], vec_ref[3:4, :], vec_ref[4:5, :]
    w_log = -_softplus(-(w0 + _bdot(jnp.tanh(wd), wup_ref[...]))) - 0.5
    lw = -jnp.exp(w_log)
    a = _sigmoid(a0 + _bdot(ad, aup_ref[...]))
    kk = k * k_k
    kk = kk * lax.rsqrt(_seg_sum(kk * kk, n) + L2_EPS)
    k = k * (1.0 + (a - 1.0) * k_a)
    bonus = _seg_sum(r * k * r_k, n) * v
    b = kk * a

    incl, strict = _order_masks(c, rev)
    cum = _dot_mask_lhs(jnp.where(incl, 1.0, 0.0).astype(BF16), lw)
    cum_end = cum[keep:keep + 1, :]
    e_pos = jnp.exp(cum)
    e_neg = jnp.exp(-cum)
    r_t = r * e_pos
    a_t = -kk * jnp.exp(cum - lw)
    k_t = k * e_neg
    b_t = b * e_neg
    w_end = jnp.exp(cum_end)
    kw = k_t * w_end
    bw = b_t * w_end
    eye = lax.broadcasted_iota(jnp.int32, (n, n), 0) == lax.broadcasted_iota(jnp.int32, (n, n), 1)

    ys = []
    for h0 in range(0, RWKV_HEADS, RWKV_GROUP):
        hs = range(h0, h0 + RWKV_GROUP)
        sls = [slice(h * n, (h + 1) * n) for h in hs]
        s1 = [_bdot_nt(jnp.concatenate([a_t[:, sl], r_t[:, sl]], axis=0),
                       jnp.concatenate([b_t[:, sl], k_t[:, sl]], axis=0)) for sl in sls]
        a_ab = [jnp.where(strict, x[:c, :c], 0.0) for x in s1]
        a_lo = [jnp.concatenate([jnp.where(strict, x[:c, c:], 0.0), jnp.where(incl, x[c:, c:], 0.0)], axis=0) for x in s1]
        a_rb = [jnp.where(incl, x[c:, :c], 0.0) for x in s1]
        tms = _neumann_inverse_multi(a_ab, c, _dot_inv)
        av = [_bdot(a, v[:, sl]) for a, sl in zip(a_lo, sls)]
        pp = [_dot_inv(t, jnp.concatenate([a_t[:, sl], x[:c]], axis=1)) for t, sl, x in zip(tms, sls, av)]
        rq_y0 = [_bdot(a, p) for a, p in zip(a_rb, pp)]
        m_g = [_bdot_tn(bw[:, sl], p) for sl, p in zip(sls, pp)]
        kv = [_bdot_tn(kw[:, sl], v[:, sl]) for sl in sls]
        so = []
        for i, h in enumerate(hs):
            rq = r_t[:, sls[i]] + rq_y0[i][:, :n]
            m = m_g[i][:, :n] + jnp.where(eye, jnp.broadcast_to(w_end[:, sls[i]], (n, n)), 0.0)
            so.append(_dot_state(jnp.concatenate([rq, m], axis=0), st_scr[h]))
        for i, h in enumerate(hs):
            st_scr[h] = so[i][c:] + (m_g[i][:, n:] + kv[i])
            ys.append(so[i][:c] + (rq_y0[i][:, n:] + av[i][c:]))
    y = jnp.concatenate(ys, axis=1)

    if final:
        ysum = y + yb_ref[...]
        mu_h = _seg_sum(ysum, n) * (1.0 / n)
        yc = ysum - mu_h
        var = _seg_sum(yc * yc, n) * (1.0 / n)
        ln_g, ln_b = vec_ref[5:6, :], vec_ref[6:7, :]
        yn = yc * lax.rsqrt(var + GN_EPS) * ln_g + ln_b + bonus + bb_ref[...]
        gate = _bdot(_sigmoid(gl_ref[...]), gup_ref[...])
        o_ref[...] = (yn * gate).astype(o_ref.dtype)
    else:
        y_ref[...] = y
        bonus_ref[...] = bonus


def _rwkv_call(p, prm, other, rev, n_ctx):
    t = p.shape[0]
    c = RWKV_CHUNK
    n, nc = t // c, n_ctx // c
    wdt = RWKV_W
    d = 1 if rev else 0
    final = other is not None
    cm = lambda s: _chunk_index(s, rev, nc, n)
    base = 4096 // wdt
    full = lambda shape: pl.BlockSpec(shape, lambda s: (0,) * len(shape))
    in_specs = [pl.BlockSpec((c, wdt), lambda s: (cm(s), base)),
                pl.BlockSpec((c, wdt), lambda s: (cm(s), base + 1)),
                pl.BlockSpec((c, wdt), lambda s: (cm(s), base + 2)),
                pl.BlockSpec((c, LANES), lambda s: (cm(s), 7168 // LANES)),
                pl.BlockSpec((c, LANES), lambda s: (cm(s), 7168 // LANES + 1)),
                full((1, 3 * wdt)), full((2, LANES)), full((SUBLANES, wdt)),
                full((RWKV_LORA, wdt)), full((RWKV_LORA, wdt))]
    args = [p, p, p, p, p, prm["mu"][d], prm["mu_small"][d], prm["vecs"][d], prm["w_up"][d], prm["a_up"][d]]
    if final:
        in_specs += [pl.BlockSpec((c, 2 * LANES), lambda s: (cm(s), 7424 // (2 * LANES))),
                     full((2 * LANES, wdt)),
                     pl.BlockSpec((c, wdt), lambda s: (cm(s), 0)),
                     pl.BlockSpec((c, wdt), lambda s: (cm(s), 0))]
        args += [p, prm["g_up"], other[0], other[1]]
        out_specs = pl.BlockSpec((c, wdt), lambda s: (cm(s), 0))
        out_shape = jax.ShapeDtypeStruct((t, wdt), BF16)
    else:
        out_specs = [pl.BlockSpec((c, wdt), lambda s: (cm(s), 0))] * 2
        out_shape = [jax.ShapeDtypeStruct((t, wdt), F32)] * 2
    return pl.pallas_call(
        functools.partial(_rwkv_kernel, rev=rev, final=final, c=c, nc=nc),
        grid=(n,),
        in_specs=in_specs,
        out_specs=out_specs,
        out_shape=out_shape,
        scratch_shapes=[pltpu.VMEM((RWKV_HEADS, RWKV_N, RWKV_N), F32),
                        pltpu.VMEM((SUBLANES, 3 * wdt + 2 * LANES), F32)],
        compiler_params=_cparams(1),
        name="rwkv7_bwd" if rev else "rwkv7_fwd_merge",
    )(*args)


def _mlstm_kernel(*refs, rev, final, c):
    if final:
        q_ref, k_ref, v_ref, sm_ref, bias_ref, og_ref, ng_ref, hb_ref, o_ref, ct_scr, m_scr = refs
    else:
        q_ref, k_ref, v_ref, sm_ref, bias_ref, o_ref, ct_scr, m_scr = refs
    d = 1 if rev else 0
    dk, dv = MLSTM_DK, MLSTM_DV
    nh = MLSTM_HEADS

    @pl.when(pl.program_id(0) == 0)
    def _():
        ct_scr[...] = jnp.zeros_like(ct_scr)
        m_scr[...] = jnp.zeros_like(m_scr)

    incl, _ = _order_masks(c, rev)
    keep = 0 if rev else c - 1
    sm = sm_ref[...]
    li = sm + bias_ref[0:1, :]
    lf = _log_sigmoid(sm + bias_ref[1:2, :])
    bcum = _dot_mask_lhs(jnp.where(incl, 1.0, 0.0).astype(BF16), lf)
    bcum_t = bcum.T
    li_t = li.T
    ones_col = jnp.where(lax.broadcasted_iota(jnp.int32, (c, LANES), 1) == 0, 1.0, 0.0).astype(BF16)

    heads = range(nh)
    cis = [d * 2 * nh + h for h in heads]
    b_col = [bcum[:, ci + nh:ci + nh + 1] for ci in cis]
    i_col = [li[:, ci:ci + 1] for ci in cis]
    m_prev = [m_scr[h:h + 1, 0:1] for h in heads]
    qs = [q_ref[:, h * dk:(h + 1) * dk] * (dk ** -0.5) for h in heads]
    ks = [k_ref[:, h * dk:(h + 1) * dk] for h in heads]
    v_aug = [jnp.concatenate([v_ref[:, h * dv:(h + 1) * dv].astype(BF16), ones_col], axis=1) for h in heads]
    qk_raw = [_bdot_nt(q, k) for q, k in zip(qs, ks)]
    q_ct = [_bdot(q, ct_scr[h]) for q, h in zip(qs, heads)]
    d_log = [jnp.where(incl, bc - bcum_t[ci + nh:ci + nh + 1, :] + li_t[ci:ci + 1, :], -jnp.inf)
             for bc, ci in zip(b_col, cis)]
    inter = [bc + mp for bc, mp in zip(b_col, m_prev)]
    m_t = [jnp.maximum(jnp.max(dl, axis=-1, keepdims=True), it) for dl, it in zip(d_log, inter)]
    qk = [x * jnp.exp(dl - mt) for x, dl, mt in zip(qk_raw, d_log, m_t)]
    num_aug = [_bdot(x, va) + jnp.exp(it - mt) * qc for x, va, it, mt, qc in zip(qk, v_aug, inter, m_t, q_ct)]
    b_end = [bc[keep:keep + 1, :] for bc in b_col]
    w_log = [be - bc + ic for be, bc, ic in zip(b_end, b_col, i_col)]
    m_new = [jnp.maximum(be + mp, jnp.max(wl, axis=0, keepdims=True)) for be, mp, wl in zip(b_end, m_prev, w_log)]
    kv = [_bdot_tn(k * jnp.exp(wl - mn), va) for k, wl, mn, va in zip(ks, w_log, m_new, v_aug)]
    for h in heads:
        ct_scr[h] = jnp.exp(b_end[h] + m_prev[h] - m_new[h]) * ct_scr[h] + kv[h]
        m_scr[h:h + 1, :] = jnp.broadcast_to(m_new[h], (1, LANES))
        den = num_aug[h][:, dv:dv + 1]
        hout = num_aug[h][:, :dv] / jnp.maximum(jnp.abs(den), jnp.exp(-m_t[h]))
        sl = slice(h * dv, (h + 1) * dv)
        if final:
            tot = hout + hb_ref[:, sl]
            y = tot * lax.rsqrt(jnp.mean(tot * tot, axis=-1, keepdims=True) + EPS)
            o_ref[:, sl] = (y * ng_ref[0:1, sl] * _sigmoid(og_ref[:, sl])).astype(o_ref.dtype)
        else:
            o_ref[:, sl] = hout


def _mlstm_call(p, bias_rows, norm_g, other, rev, n_ctx):
    t = p.shape[0]
    c = MLSTM_CHUNK
    n, nc = t // c, n_ctx // c
    dk, dv, nh = MLSTM_DK, MLSTM_DV, MLSTM_HEADS
    final = other is not None
    cm = lambda s: _chunk_index(s, rev, nc, n)
    full = lambda shape: pl.BlockSpec(shape, lambda s: (0,) * len(shape))
    in_specs = [pl.BlockSpec((c, nh * dk), lambda s: (cm(s), 0)),
                pl.BlockSpec((c, nh * dk), lambda s: (cm(s), 1)),
                pl.BlockSpec((c, nh * dv), lambda s: (cm(s), 1)),
                pl.BlockSpec((c, LANES), lambda s: (cm(s), 7168 // LANES)),
                full((2, LANES))]
    args = [p, p, p, p, bias_rows]
    if final:
        in_specs += [pl.BlockSpec((c, nh * dv), lambda s: (cm(s), 2)), full((1, nh * dv)),
                     pl.BlockSpec((c, nh * dv), lambda s: (cm(s), 0))]
        args += [p, norm_g.reshape(1, nh * dv), other]
    return pl.pallas_call(
        functools.partial(_mlstm_kernel, rev=rev, final=final, c=c),
        grid=(n,),
        in_specs=in_specs,
        out_specs=pl.BlockSpec((c, nh * dv), lambda s: (cm(s), 0)),
        out_shape=jax.ShapeDtypeStruct((t, nh * dv), BF16 if final else F32),
        scratch_shapes=[pltpu.VMEM((nh, dk, dv + LANES), F32), pltpu.VMEM((SUBLANES, LANES), F32)],
        compiler_params=_cparams(1),
        name="mlstm_bwd" if rev else "mlstm_fwd_merge",
    )(*args)


def _gdn_prep_kernel(xm_ref, xp_ref, xn_ref, cw_ref, sm_ref, gp_ref, q_ref, k_ref, v_ref, gb_ref, *, tm, n_ctx, t_total):
    i = pl.program_id(0)
    x = xm_ref[...]
    rows = i * tm + lax.broadcasted_iota(jnp.int32, (tm, 1), 0)
    first = rows == i * tm
    last = rows == i * tm + tm - 1
    prev = jnp.where(first, xp_ref[SUBLANES - 1:SUBLANES, :], pltpu.roll(x, 1, axis=0))
    nxt = jnp.where(last, xn_ref[0:1, :], pltpu.roll(x, tm - 1, axis=0))
    prev = jnp.where(jnp.logical_or(rows == 0, rows == n_ctx), 0.0, prev)
    nxt = jnp.where(jnp.logical_or(rows == n_ctx - 1, rows == t_total - 1), 0.0, nxt)
    y = _silu(prev * cw_ref[0:1, :] + x * cw_ref[1:2, :] + nxt * cw_ref[2:3, :])
    w = GDN_HEADS * GDN_DK
    q, k, v = y[:, :w], y[:, w:2 * w], y[:, 2 * w:]
    q_ref[...] = (q * lax.rsqrt(_seg_sum(q * q, GDN_DK) + L2_EPS) * (GDN_DK ** -0.5)).astype(BF16)
    k_ref[...] = (k * lax.rsqrt(_seg_sum(k * k, GDN_DK) + L2_EPS)).astype(BF16)
    v_ref[...] = v.astype(BF16)
    sm = sm_ref[...]
    lane = lax.broadcasted_iota(jnp.int32, sm.shape, 1)
    log_alpha = -jnp.exp(gp_ref[0:1, :]) * _softplus(sm + gp_ref[1:2, :])
    gb_ref[...] = jnp.where(lane < 32, log_alpha, _sigmoid(sm))


def _gdn_prep_call(p, conv_w, gate_params, n_ctx):
    t = p.shape[0]
    tm = 256
    w = GDN_HEADS * GDN_DK
    qkv_blk = 3072 // GDN_QKV
    nb8 = t // SUBLANES
    r8 = tm // SUBLANES
    spec_o = pl.BlockSpec((tm, w), lambda i: (i, 0))
    return pl.pallas_call(
        functools.partial(_gdn_prep_kernel, tm=tm, n_ctx=n_ctx, t_total=t),
        grid=(t // tm,),
        in_specs=[pl.BlockSpec((tm, GDN_QKV), lambda i: (i, qkv_blk)),
                  pl.BlockSpec((SUBLANES, GDN_QKV), lambda i: (jnp.maximum(i * r8 - 1, 0), qkv_blk)),
                  pl.BlockSpec((SUBLANES, GDN_QKV), lambda i: (jnp.minimum((i + 1) * r8, nb8 - 1), qkv_blk)),
                  pl.BlockSpec((3, GDN_QKV), lambda i: (0, 0)),
                  pl.BlockSpec((tm, LANES), lambda i: (i, 7168 // LANES)),
                  pl.BlockSpec((2, LANES), lambda i: (0, 0))],
        out_specs=[spec_o, spec_o, spec_o, pl.BlockSpec((tm, LANES), lambda i: (i, 0))],
        out_shape=[jax.ShapeDtypeStruct((t, w), BF16)] * 3 + [jax.ShapeDtypeStruct((t, LANES), F32)],
        compiler_params=_cparams(1),
        name="gdn_conv_norm_gates",
    )(p, p, p, conv_w, p, gate_params)


def _gdn_kernel(*refs, rev, final, c):
    if final:
        q_ref, k_ref, v_ref, gb_ref, gate_ref, ng_ref, ob_ref, o_ref, st_scr = refs
    else:
        q_ref, k_ref, v_ref, gb_ref, o_ref, st_scr = refs
    d = 1 if rev else 0
    dk = GDN_DK
    nh = GDN_HEADS

    @pl.when(pl.program_id(0) == 0)
    def _():
        st_scr[...] = jnp.zeros_like(st_scr)

    incl, strict = _order_masks(c, rev)
    keep = 0 if rev else c - 1
    gb = gb_ref[...]
    gc = _dot_mask_lhs(jnp.where(incl, 1.0, 0.0).astype(BF16), gb)
    gc_t = gc.T

    for h0 in range(0, nh, GDN_GROUP):
        hs = list(range(h0, h0 + GDN_GROUP))
        sls = [slice(h * dk, (h + 1) * dk) for h in hs]
        cols = [16 + d * nh + h for h in hs]
        g_col = [gc[:, cc:cc + 1] for cc in cols]
        beta = [gb[:, cc + 16:cc + 17] for cc in cols]
        g_end = [g[keep:keep + 1, :] for g in g_col]
        e_g = [jnp.exp(g) for g in g_col]
        decay = [jnp.exp(jnp.where(incl, g - gc_t[cc:cc + 1, :], -jnp.inf)) for g, cc in zip(g_col, cols)]
        qs = [q_ref[:, sl].astype(F32) for sl in sls]
        ks = [k_ref[:, sl].astype(F32) for sl in sls]
        vs = [v_ref[:, sl].astype(F32) for sl in sls]
        kq = [_bdot_nt(jnp.concatenate([kh * b, qh], axis=0), kh) for kh, qh, b in zip(ks, qs, beta)]
        a_qk = [x[c:] * dc for x, dc in zip(kq, decay)]
        tinv = _neumann_inverse_multi([jnp.where(strict, -(x[:c] * dc), 0.0) for x, dc in zip(kq, decay)], c, _dot_inv)
        uw = [_dot_inv(t, jnp.concatenate([vh * b, kh * (b * e)], axis=1))
              for t, vh, kh, b, e in zip(tinv, vs, ks, beta, e_g)]
        o_part = [_bdot(a, u) for a, u in zip(a_qk, uw)]
        s_part = [_bdot_tn(kh * jnp.exp(ge - g), u) for kh, ge, g, u in zip(ks, g_end, g_col, uw)]
        so = [_dot_state(jnp.concatenate([qh * e - op[:, dk:], sp[:, dk:]], axis=0), st_scr[h])
              for qh, e, op, sp, h in zip(qs, e_g, o_part, s_part, hs)]
        for i, h in enumerate(hs):
            sl = sls[i]
            out = so[i][:c] + o_part[i][:, :dk]
            st_scr[h] = jnp.exp(g_end[i]) * st_scr[h] - so[i][c:] + s_part[i][:, :dk]
            if final:
                tot = out + ob_ref[:, sl]
                y = tot * lax.rsqrt(jnp.mean(tot * tot, axis=-1, keepdims=True) + EPS)
                o_ref[:, sl] = (y * ng_ref[0:1, sl] * _silu(gate_ref[:, sl])).astype(o_ref.dtype)
            else:
                o_ref[:, sl] = out


def _gdn_call(p, q, k, v, gb, norm_g, other, rev, n_ctx):
    t = q.shape[0]
    c = GDN_CHUNK
    n, nc = t // c, n_ctx // c
    w = GDN_HEADS * GDN_DK
    final = other is not None
    cm = lambda s: _chunk_index(s, rev, nc, n)
    blk = pl.BlockSpec((c, w), lambda s: (cm(s), 0))
    in_specs = [blk, blk, blk, pl.BlockSpec((c, LANES), lambda s: (cm(s), 0))]
    args = [q, k, v, gb]
    if final:
        in_specs += [pl.BlockSpec((c, w), lambda s: (cm(s), 6144 // w)), pl.BlockSpec((1, w), lambda s: (0, 0)), blk]
        args += [p, norm_g.reshape(1, w), other]
    return pl.pallas_call(
        functools.partial(_gdn_kernel, rev=rev, final=final, c=c),
        grid=(n,),
        in_specs=in_specs,
        out_specs=blk,
        out_shape=jax.ShapeDtypeStruct((t, w), BF16 if final else F32),
        scratch_shapes=[pltpu.VMEM((GDN_HEADS, GDN_DK, GDN_DK), F32)],
        compiler_params=_cparams(1),
        name="gdn_bwd" if rev else "gdn_fwd_merge",
    )(*args)


def _pad_cols(w, width):
    return jnp.pad(w, ((0, 0), (0, width - w.shape[1])))


def _lane_row(pieces):
    row = jnp.zeros((LANES,), F32)
    for off, vec in pieces:
        row = row.at[off:off + vec.shape[0]].set(vec.astype(F32))
    return row


def _odd_weight(w):
    cols = [w[:, 0:3072], w[:, 3088:6160], w[:, 6160:7184], w[:, 3072:3088], w[:, 7184:7216]]
    return _pad_cols(jnp.concatenate(cols, axis=1), F_PAD).astype(BF16)


def _rwkv_params(e, rwkv_mu, rwkv_w0, rwkv_w_up, rwkv_a0, rwkv_a_up, rwkv_k_k, rwkv_k_a, rwkv_r_k,
                 rwkv_ln_g, rwkv_ln_b, rwkv_g_up):
    wdt = RWKV_W
    mu = rwkv_mu[e]
    zeros64 = jnp.zeros((RWKV_LORA,), F32)
    mu_small, vecs = [], []
    for d in range(2):
        m_wd = mu[d, 3 * wdt:3 * wdt + RWKV_LORA]
        m_ad = mu[d, 3 * wdt + RWKV_LORA:]
        lo = [m_wd, zeros64] if d == 0 else [zeros64, m_wd]
        la = [m_ad, zeros64] if d == 0 else [zeros64, m_ad]
        mu_small.append(jnp.stack([jnp.concatenate(lo), jnp.concatenate(la)]))
        vecs.append(jnp.stack([rwkv_w0[e, d], rwkv_a0[e, d], rwkv_k_k[e], rwkv_k_a[e], rwkv_r_k[e].reshape(wdt),
                               rwkv_ln_g[e], rwkv_ln_b[e], jnp.zeros((wdt,), F32)]))
    return {
        "mu": mu[:, None, :3 * wdt],
        "mu_small": jnp.stack(mu_small),
        "vecs": jnp.stack(vecs),
        "w_up": rwkv_w_up[e].astype(BF16),
        "a_up": rwkv_a_up[e].astype(BF16),
        "g_up": jnp.pad(rwkv_g_up[e], ((0, 2 * LANES - RWKV_GATE_LORA), (0, 0))).astype(BF16),
    }


def _even_mixer(p, e, n_ctx, ret_decay_logit, rwkv_prm):
    logit = ret_decay_logit[e].astype(F32)
    ret_b = _retention_call(p, logit, None, True, n_ctx)
    y_ret = _retention_call(p, logit, ret_b, False, n_ctx)
    rw_b = _rwkv_call(p, rwkv_prm, None, True, n_ctx)
    y_rwkv = _rwkv_call(p, rwkv_prm, rw_b, False, n_ctx)
    return y_ret, y_rwkv


def _odd_mixer(p, o, n_ctx, mlstm_gate_b, mlstm_norm_g, gdn_conv, gdn_a_log, gdn_dt_bias, gdn_norm_g):
    gate_b = mlstm_gate_b[o]
    nh = MLSTM_HEADS
    bias_i = _lane_row([(d * 2 * nh, gate_b[d, 0]) for d in range(2)])
    bias_f = _lane_row([(d * 2 * nh + nh, gate_b[d, 1]) for d in range(2)])
    bias_rows = jnp.stack([bias_i, bias_f])
    ml_b = _mlstm_call(p, bias_rows, mlstm_norm_g[o], None, True, n_ctx)
    y_ml = _mlstm_call(p, bias_rows, mlstm_norm_g[o], ml_b, False, n_ctx)

    gate_params = jnp.stack([_lane_row([(16, gdn_a_log[o].reshape(-1))]), _lane_row([(16, gdn_dt_bias[o].reshape(-1))])])
    q, k, v, gb = _gdn_prep_call(p, gdn_conv[o], gate_params, n_ctx)
    gd_b = _gdn_call(p, q, k, v, gb, gdn_norm_g[o], None, True, n_ctx)
    y_gd = _gdn_call(p, q, k, v, gb, gdn_norm_g[o], gd_b, False, n_ctx)
    return y_ml, y_gd


def kernel(x, c, ctx, c_ctx, ada_w, ada_b, norm1_g, norm2_g, mix_w_out, ffn_w_in, ffn_dw, ffn_dw_b, ffn_w_out, final_norm_g, ev_w_in, ret_decay_logit, rwkv_mu, rwkv_w0, rwkv_w_up, rwkv_a0, rwkv_a_up, rwkv_k_k, rwkv_k_a, rwkv_r_k, rwkv_ln_g, rwkv_ln_b, rwkv_g_up, od_w_in, mlstm_gate_b, mlstm_norm_g, gdn_conv, gdn_a_log, gdn_dt_bias, gdn_norm_g):
    assert x.shape[0] == 1 and ctx.shape[0] == 1
    n_ctx = ctx.shape[1]
    n_lat = x.shape[1]
    depth = ada_w.shape[0]
    cvecs = jnp.stack([c_ctx, c[0]], axis=1)
    mod = _ada_call(cvecs, ada_w, ada_b)
    xt, h1 = _assemble_call(ctx, x, norm1_g[0], mod)

    for l in range(depth):
        last = l == depth - 1
        half = D_MODEL // 2
        if l % 2 == 0:
            e = l // 2
            p = _proj_call(h1, ev_w_in[e], F_PAD)
            prm = _rwkv_params(e, rwkv_mu, rwkv_w0, rwkv_w_up, rwkv_a0, rwkv_a_up, rwkv_k_k, rwkv_k_a, rwkv_r_k,
                               rwkv_ln_g, rwkv_ln_b, rwkv_g_up)
            ya, yb = _even_mixer(p, e, n_ctx, ret_decay_logit, prm)
        else:
            o = l // 2
            p = _proj_call(h1, _odd_weight(od_w_in[o]), F_PAD)
            ya, yb = _odd_mixer(p, o, n_ctx, mlstm_gate_b, mlstm_norm_g, gdn_conv, gdn_a_log, gdn_dt_bias, gdn_norm_g)
        w_mix = mix_w_out[l].astype(BF16)
        xt, h2 = _outproj_call(ya, yb, w_mix[:half], w_mix[half:], xt, norm2_g[l], mod, l, n_ctx)
        gate, up = _ffn_in_call(h2, ffn_w_in, l)
        dw9 = _pad_cols(ffn_dw[l].reshape(9, D_FF), FF_PAD)
        dwb = _pad_cols(ffn_dw_b[l].reshape(1, D_FF), FF_PAD)
        if last:
            xt = _ffn_out_call(gate, up, dw9, dwb, ffn_w_out, xt, mod, l, final_norm_g, n_ctx, True)
        else:
            xt, h1 = _ffn_out_call(gate, up, dw9, dwb, ffn_w_out, xt, mod, l, norm1_g[l + 1], n_ctx, False)
    return xt[n_ctx:][None]
```

```python
import functools
import math

import jax
import jax.numpy as jnp
from jax import lax
from jax.experimental import pallas as pl
from jax.experimental.pallas import tpu as pltpu

F32 = jnp.float32
BF16 = jnp.bfloat16

D_MODEL = 2048
DEPTH = 2
GRID_W = 64
EPS = 1e-6
GN_EPS = 64e-5
L2_EPS = 1e-12
D_FF = 5504

RET_HEADS = 8
RET_DK = 128
RWKV_HEADS = 16
RWKV_N = 64
RWKV_W = RWKV_HEADS * RWKV_N
RWKV_LORA = 64
RWKV_GATE_LORA = 160
MLSTM_HEADS = 4
MLSTM_DK = 128
MLSTM_DV = 256
GDN_HEADS = 8
GDN_DK = 128
GDN_QKV = 3072

LANES = 128
SUBLANES = 8
MXU_N = 256
VMEM_LIMIT = 56 * 1024 * 1024

F_PAD = 7680
FF_PAD = 5632
ROW_TILE = 768
RET_CHUNK = 256
MLSTM_CHUNK = 256
GDN_CHUNK = 64
RWKV_CHUNK = 64
RWKV_GROUP = 8
GDN_GROUP = 8


def _cparams(n_axes):
    return pltpu.CompilerParams(dimension_semantics=("arbitrary",) * n_axes, vmem_limit_bytes=VMEM_LIMIT)


def _bdot(a, b):
    return jnp.dot(a.astype(BF16), b.astype(BF16), preferred_element_type=F32)


def _bdot_nt(a, b):
    return lax.dot_general(a.astype(BF16), b.astype(BF16), (((1,), (1,)), ((), ())), preferred_element_type=F32)


def _bdot_tn(a, b):
    return lax.dot_general(a.astype(BF16), b.astype(BF16), (((0,), (0,)), ((), ())), preferred_element_type=F32)


def _split3(x):
    x1 = x.astype(BF16)
    r1 = x - x1.astype(F32)
    x2 = r1.astype(BF16)
    x3 = (r1 - x2.astype(F32)).astype(BF16)
    return x1, x2, x3


def _dot_mask_lhs(m_bf16, x):
    x1, x2, x3 = _split3(x)
    d = lambda t: jnp.dot(m_bf16, t, preferred_element_type=F32)
    return (d(x3) + d(x2)) + d(x1)


def _dot3(a, b):
    a1 = a.astype(BF16)
    a2 = (a - a1.astype(F32)).astype(BF16)
    b1 = b.astype(BF16)
    b2 = (b - b1.astype(F32)).astype(BF16)
    d = lambda u, v: jnp.dot(u, v, preferred_element_type=F32)
    return (d(a2, b1) + d(a1, b2)) + d(a1, b1)


def _sigmoid(x):
    return 1.0 / (1.0 + jnp.exp(-x))


def _silu(x):
    return x * _sigmoid(x)


def _softplus(x):
    return jnp.maximum(x, 0.0) + jnp.log1p(jnp.exp(-jnp.abs(x)))


def _log_sigmoid(x):
    return -_softplus(-x)


def _order_masks(c, rev):
    i = lax.broadcasted_iota(jnp.int32, (c, c), 0)
    j = lax.broadcasted_iota(jnp.int32, (c, c), 1)
    if rev:
        return j >= i, j > i
    return j <= i, j < i


def _neumann_inverse(n, c, dotf):
    i = lax.broadcasted_iota(jnp.int32, (c, c), 0)
    j = lax.broadcasted_iota(jnp.int32, (c, c), 1)
    x = jnp.where(i == j, 1.0, 0.0).astype(F32) + n
    p = dotf(n, n)
    for _ in range(int(math.log2(c)) - 2):
        r = dotf(jnp.concatenate([x, p], axis=0), p)
        x = x + r[:c]
        p = r[c:]
    return x + dotf(x, p)


def _neumann_inverse_multi(ns, c, dotf):
    i = lax.broadcasted_iota(jnp.int32, (c, c), 0)
    j = lax.broadcasted_iota(jnp.int32, (c, c), 1)
    eye = jnp.where(i == j, 1.0, 0.0).astype(F32)
    xs = [eye + n for n in ns]
    ps = [dotf(n, n) for n in ns]
    for _ in range(int(math.log2(c)) - 2):
        rs = [dotf(jnp.concatenate([x, p], axis=0), p) for x, p in zip(xs, ps)]
        xs = [x + r[:c] for x, r in zip(xs, rs)]
        ps = [r[c:] for r in rs]
    return [x + dotf(x, p) for x, p in zip(xs, ps)]


def _dot_inv(a, b):
    return _dot3(a, b)


def _dot_state(a, b):
    return _bdot(a, b)


def _chunk_index(s, rev, n_ctx_chunks, n_chunks):
    if not rev:
        return s
    return jnp.where(s < n_ctx_chunks, n_ctx_chunks - 1 - s, n_chunks + n_ctx_chunks - 1 - s)


def _seg_sum(x, seg):
    c, w = x.shape
    if seg == LANES:
        parts = [jnp.broadcast_to(jnp.sum(x[:, b * LANES:(b + 1) * LANES], axis=-1, keepdims=True), (c, LANES))
                 for b in range(w // LANES)]
        return jnp.concatenate(parts, axis=-1)
    assert seg * 2 == LANES
    lane = lax.broadcasted_iota(jnp.int32, (c, LANES), 1)
    low = lane < seg
    parts = []
    for b in range(w // LANES):
        xb = x[:, b * LANES:(b + 1) * LANES]
        s_lo = jnp.sum(jnp.where(low, xb, 0.0), axis=-1, keepdims=True)
        s_hi = jnp.sum(jnp.where(low, 0.0, xb), axis=-1, keepdims=True)
        parts.append(jnp.where(low, s_lo, s_hi))
    return jnp.concatenate(parts, axis=-1)


def _ada_kernel(c_ref, w_ref, b_ref, o_ref):
    cv = c_ref[...]
    s = _silu(cv)
    w = w_ref[0]
    bias = b_ref[0]
    r0 = jnp.sum(w * s[:, 0:1], axis=0, keepdims=True) + bias
    r1 = jnp.sum(w * s[:, 1:2], axis=0, keepdims=True) + bias
    row = lax.broadcasted_iota(jnp.int32, (SUBLANES, w.shape[1]), 0)
    o_ref[0] = jnp.where(row == 0, r0, jnp.where(row == 1, r1, 0.0))


def _ada_call(cvecs, ada_w, ada_b):
    depth, d, n = ada_w.shape
    tn = 1024
    return pl.pallas_call(
        _ada_kernel,
        grid=(depth, n // tn),
        in_specs=[pl.BlockSpec((d, 2), lambda l, j: (0, 0)),
                  pl.BlockSpec((1, d, tn), lambda l, j: (l, 0, j)),
                  pl.BlockSpec((1, 1, tn), lambda l, j: (l, 0, j))],
        out_specs=pl.BlockSpec((1, SUBLANES, tn), lambda l, j: (l, 0, j)),
        out_shape=jax.ShapeDtypeStruct((depth, SUBLANES, n), F32),
        compiler_params=_cparams(2),
        name="ada_modulation",
    )(cvecs, ada_w, ada_b.reshape(depth, 1, n))


def _mod_rows(mod, k, rows, n_ctx):
    d = D_MODEL
    vc = mod[0:1, k * d:(k + 1) * d]
    vl = mod[1:2, k * d:(k + 1) * d]
    return jnp.where(rows < n_ctx, vc, vl)


def _norm_mod(x, g, shift, scale):
    y = x * lax.rsqrt(jnp.mean(x * x, axis=-1, keepdims=True) + EPS) * g
    return y * (1.0 + scale) + shift


def _assemble_kernel(ctx_ref, x_ref, g_ref, mod_ref, xt_ref, h_ref):
    i = pl.program_id(0)
    d = D_MODEL
    mod = mod_ref[0]

    def emit(src, row):
        xt_ref[...] = src
        h = _norm_mod(src, g_ref[...], mod[row:row + 1, 0:d], mod[row:row + 1, d:2 * d])
        h_ref[...] = h.astype(BF16)

    @pl.when(i == 0)
    def _():
        emit(ctx_ref[0], 0)

    @pl.when(i > 0)
    def _():
        emit(x_ref[0], 1)


def _assemble_call(ctx, x, g, mod):
    n_ctx, d = ctx.shape[1], ctx.shape[2]
    t = n_ctx + x.shape[1]
    assert x.shape[1] % n_ctx == 0
    spec_o = pl.BlockSpec((n_ctx, d), lambda i: (i, 0))
    return pl.pallas_call(
        _assemble_kernel,
        grid=(t // n_ctx,),
        in_specs=[pl.BlockSpec((1, n_ctx, d), lambda i: (0, 0, 0)),
                  pl.BlockSpec((1, n_ctx, d), lambda i: (0, jnp.maximum(i - 1, 0), 0)),
                  pl.BlockSpec((1, d), lambda i: (0, 0)),
                  pl.BlockSpec((1, SUBLANES, 6 * d), lambda i: (0, 0, 0))],
        out_specs=[spec_o, spec_o],
        out_shape=[jax.ShapeDtypeStruct((t, d), F32), jax.ShapeDtypeStruct((t, d), BF16)],
        compiler_params=_cparams(1),
        name="assemble_norm1",
    )(ctx, x, g.reshape(1, d), mod)


def _proj_kernel(h_ref, w_ref, o_ref, w_scr, *, tn, n_valid):
    @pl.when(pl.program_id(1) == 0)
    def _():
        w = w_ref[...]
        col = pl.program_id(0) * tn + lax.broadcasted_iota(jnp.int32, w.shape, 1)
        w_scr[...] = jnp.where(col < n_valid, w, 0).astype(BF16)

    o_ref[...] = jnp.dot(h_ref[...], w_scr[...], preferred_element_type=F32)


def _proj_call(h, w, f_pad):
    t, d = h.shape
    tm, tn = ROW_TILE, 768
    return pl.pallas_call(
        functools.partial(_proj_kernel, tn=tn, n_valid=w.shape[1]),
        grid=(f_pad // tn, t // tm),
        in_specs=[pl.BlockSpec((tm, d), lambda j, i: (i, 0)),
                  pl.BlockSpec((d, tn), lambda j, i: (0, j))],
        out_specs=pl.BlockSpec((tm, tn), lambda j, i: (i, j)),
        out_shape=jax.ShapeDtypeStruct((t, f_pad), F32),
        scratch_shapes=[pltpu.VMEM((d, tn), BF16)],
        compiler_params=_cparams(2),
        name="in_proj",
    )(h, w)


def _outproj_kernel(ya_ref, yb_ref, wa_ref, wb_ref, x_ref, g_ref, mod_ref, xo_ref, h_ref, *, tm, n_ctx):
    i = pl.program_id(0)
    rows = i * tm + lax.broadcasted_iota(jnp.int32, (tm, 1), 0)
    mod = mod_ref[0]
    acc = jnp.dot(ya_ref[...], wa_ref[...], preferred_element_type=F32)
    acc = acc + jnp.dot(yb_ref[...], wb_ref[...], preferred_element_type=F32)
    xn = x_ref[...] + _mod_rows(mod, 2, rows, n_ctx) * acc
    xo_ref[...] = xn
    h = _norm_mod(xn, g_ref[...], _mod_rows(mod, 3, rows, n_ctx), _mod_rows(mod, 4, rows, n_ctx))
    h_ref[...] = h.astype(BF16)


def _outproj_call(ya, yb, wa, wb, x, g2, mod, layer, n_ctx):
    t, d = x.shape
    half = ya.shape[1]
    tm = 384
    return pl.pallas_call(
        functools.partial(_outproj_kernel, tm=tm, n_ctx=n_ctx),
        grid=(t // tm,),
        in_specs=[pl.BlockSpec((tm, half), lambda i: (i, 0)),
                  pl.BlockSpec((tm, half), lambda i: (i, 0)),
                  pl.BlockSpec((half, d), lambda i: (0, 0)),
                  pl.BlockSpec((half, d), lambda i: (0, 0)),
                  pl.BlockSpec((tm, d), lambda i: (i, 0)),
                  pl.BlockSpec((1, d), lambda i: (0, 0)),
                  pl.BlockSpec((1, SUBLANES, 6 * d), lambda i: (layer, 0, 0))],
        out_specs=[pl.BlockSpec((tm, d), lambda i: (i, 0)),
                   pl.BlockSpec((tm, d), lambda i: (i, 0))],
        out_shape=[jax.ShapeDtypeStruct((t, d), F32), jax.ShapeDtypeStruct((t, d), BF16)],
        compiler_params=_cparams(1),
        name="mix_out_proj_norm2",
    )(ya, yb, wa, wb, x, g2.reshape(1, d), mod)


def _ffn_in_kernel(h_ref, wg_ref, wu0_ref, wu1_ref, wu2_ref, wu3_ref, wo_ref, g_ref, u_ref, wob_ref, w_scr, *, tn):
    @pl.when(pl.program_id(1) == 0)
    def _():
        j = pl.program_id(0)
        wu = jnp.concatenate([wu0_ref[0], wu1_ref[0], wu2_ref[0], wu3_ref[0]], axis=1)
        col = j * tn + lax.broadcasted_iota(jnp.int32, wu.shape, 1)
        w_scr[0] = jnp.where(col < D_FF, wg_ref[0], 0.0).astype(BF16)
        w_scr[1] = jnp.where(col < D_FF, wu, 0.0).astype(BF16)
        wo = wo_ref[0]
        row = j * tn + lax.broadcasted_iota(jnp.int32, wo.shape, 0)
        wob_ref[...] = jnp.where(row < D_FF, wo, 0.0).astype(BF16)

    h = h_ref[...]
    g_ref[...] = jnp.dot(h, w_scr[0], preferred_element_type=F32).astype(BF16)
    u_ref[...] = jnp.dot(h, w_scr[1], preferred_element_type=F32).astype(BF16)


def _ffn_in_call(h, ffn_w_in, ffn_w_out, layer):
    t, d = h.shape
    tm, tn = ROW_TILE, 512
    q = tn // LANES
    assert D_FF % LANES == 0 and q == 4
    up0 = D_FF // LANES
    last = 2 * D_FF // LANES - 1
    spec_o = pl.BlockSpec((tm, tn), lambda j, i: (i, j))
    up_specs = [pl.BlockSpec((1, d, LANES), lambda j, i, r=r: (layer, 0, jnp.minimum(up0 + q * j + r, last)))
                for r in range(q)]
    return pl.pallas_call(
        functools.partial(_ffn_in_kernel, tn=tn),
        grid=(FF_PAD // tn, t // tm),
        in_specs=[pl.BlockSpec((tm, d), lambda j, i: (i, 0)),
                  pl.BlockSpec((1, d, tn), lambda j, i: (layer, 0, j))] + up_specs
                 + [pl.BlockSpec((1, tn, d), lambda j, i: (layer, j, 0))],
        out_specs=[spec_o, spec_o, pl.BlockSpec((tn, d), lambda j, i: (j, 0))],
        out_shape=[jax.ShapeDtypeStruct((t, FF_PAD), BF16)] * 2 + [jax.ShapeDtypeStruct((FF_PAD, d), BF16)],
        scratch_shapes=[pltpu.VMEM((2, d, tn), BF16)],
        compiler_params=_cparams(2),
        name="ffn_in_proj",
    )(h, ffn_w_in, ffn_w_in, ffn_w_in, ffn_w_in, ffn_w_in, ffn_w_out)


def _glu_act(conv, bias, up):
    gate = conv + bias
    return (0.5 * gate * (1.0 + lax.erf(gate * (2.0 ** -0.5))) * up.astype(F32)).astype(BF16)


def _ffn_out_kernel(*refs, tm, tk, n_ctx, t_total, final_norm):
    if final_norm:
        (gm_ref, gp_ref, gn_ref, u_ref, dw_ref, db_ref, wo_ref, x_ref, mod_ref, fg_ref,
         o_ref, acc_scr, act_a, act_b) = refs
    else:
        (gm_ref, gp_ref, gn_ref, u_ref, dw_ref, db_ref, wo_ref, x_ref, mod_ref, fg_ref, nmod_ref,
         o_ref, hn_ref, acc_scr, act_a, act_b) = refs
    i = pl.program_id(0)
    k = pl.program_id(1)
    nk = pl.num_programs(1) - 1
    w = GRID_W
    nrow = tm // w
    blk0 = i * nrow
    nb_ctx, nb_tot = n_ctx // w, t_total // w
    sub = lax.broadcasted_iota(jnp.int32, (SUBLANES, LANES), 0)

    def neighbours(g, n):
        gl = pltpu.roll(g, 1, axis=0)
        gr = pltpu.roll(g, n - 1, axis=0)
        gl = jnp.concatenate([jnp.where(sub == 0, 0.0, gl[:SUBLANES]), gl[SUBLANES:]], axis=0)
        gr = jnp.concatenate([gr[:n - SUBLANES], jnp.where(sub == SUBLANES - 1, 0.0, gr[n - SUBLANES:])], axis=0)
        return gl, gr

    def step(dst, src):
        d_out = acc_scr.shape[1]
        row_halves = 2
        hm = tm // row_halves
        n_mm = row_halves * d_out // MXU_N
        n_pieces = (tk // LANES) * (nrow + 2)

        def matmul_chunk(n):
            cs = slice((n // row_halves) * MXU_N, (n // row_halves + 1) * MXU_N)
            rs = slice((n % row_halves) * hm, (n % row_halves + 1) * hm)
            acc_scr[rs, cs] += jnp.dot(src[rs, :], wo_ref[:, cs], preferred_element_type=F32)

        mm_at = {(n * n_pieces) // n_mm: n for n in range(n_mm)}
        assert len(mm_at) == n_mm
        piece = 0
        for lb in range(tk // LANES):
            ls = slice(lb * LANES, (lb + 1) * LANES)
            dwv = dw_ref[:, ls]
            bias = db_ref[:, ls]
            part = [None] * nrow
            for r in range(-1, nrow + 1):
                if piece in mm_at:
                    matmul_chunk(mm_at[piece])
                piece += 1
                if r == -1:
                    g = gp_ref[:, ls]
                elif r == nrow:
                    g = gn_ref[:, ls]
                else:
                    g = gm_ref[r * w:(r + 1) * w, ls]
                g = g.astype(F32)
                gl, gr = neighbours(g, w)
                for kh, ro in ((0, r + 1), (1, r), (2, r - 1)):
                    if not 0 <= ro < nrow:
                        continue
                    wv = dwv[3 * kh:3 * kh + 3]
                    if kh == 0:
                        wv = wv * jnp.where(blk0 + ro >= nb_ctx + 1, 1.0, 0.0)
                    if kh == 2:
                        ok = jnp.logical_and(blk0 + ro >= nb_ctx, blk0 + ro < nb_tot - 1)
                        wv = wv * jnp.where(ok, 1.0, 0.0)
                    c = gl * wv[0:1] + g * wv[1:2] + gr * wv[2:3]
                    part[ro] = c if part[ro] is None else part[ro] + c
                ro = r - 1
                if 0 <= ro < nrow:
                    rs = slice(ro * w, (ro + 1) * w)
                    dst[rs, ls] = _glu_act(part[ro], bias, u_ref[rs, ls])
                    part[ro] = None

        @pl.when(i == 0)
        def _():
            for lb in range(tk // LANES):
                ls = slice(lb * LANES, (lb + 1) * LANES)
                g = gm_ref[0:n_ctx, ls].astype(F32)
                gl, gr = neighbours(g, n_ctx)
                conv = gl * dw_ref[3:4, ls] + g * dw_ref[4:5, ls] + gr * dw_ref[5:6, ls]
                dst[0:n_ctx, ls] = _glu_act(conv, db_ref[:, ls], u_ref[0:n_ctx, ls])

    @pl.when(k == 0)
    def _():
        acc_scr[...] = jnp.zeros_like(acc_scr)
        act_b[...] = jnp.zeros_like(act_b)

    @pl.when(k % 2 == 0)
    def _():
        step(act_a, act_b)

    @pl.when(k % 2 == 1)
    def _():
        step(act_b, act_a)

    @pl.when(k == nk)
    def _():
        rb = 2 * LANES
        for r0 in range(0, tm, rb):
            rs = slice(r0, r0 + rb)
            rws = i * tm + r0 + lax.broadcasted_iota(jnp.int32, (rb, 1), 0)
            xn = x_ref[rs, :] + _mod_rows(mod_ref[0], 5, rws, n_ctx) * acc_scr[rs, :]
            if final_norm:
                xn = xn * lax.rsqrt(jnp.mean(xn * xn, axis=-1, keepdims=True) + EPS) * fg_ref[...]
            else:
                nmod = nmod_ref[0]
                hn = _norm_mod(xn, fg_ref[...], _mod_rows(nmod, 0, rws, n_ctx), _mod_rows(nmod, 1, rws, n_ctx))
                hn_ref[rs, :] = hn.astype(BF16)
            o_ref[rs, :] = xn


def _ffn_out_call(gate, up, dw9, dwb, wo, x, mod, layer, norm_g, n_ctx, final_norm):
    t, d = x.shape
    fp = gate.shape[1]
    tm, tk, w = ROW_TILE, 512, GRID_W
    assert n_ctx <= tm and n_ctx % w == 0 and tm % (2 * LANES) == 0
    rpt = tm // w
    n_rows = t // w
    spec_x = pl.BlockSpec((tm, d), lambda i, k: (i, 0))
    extra_in, extra_args = [], []
    out_specs, out_shape = spec_x, jax.ShapeDtypeStruct((t, d), F32)
    if not final_norm:
        extra_in = [pl.BlockSpec((1, SUBLANES, 6 * d), lambda i, k: (layer + 1, 0, 0))]
        extra_args = [mod]
        out_specs = [spec_x, spec_x]
        out_shape = [out_shape, jax.ShapeDtypeStruct((t, d), BF16)]
    nk = fp // tk
    assert nk % 2 == 1
    kc = lambda k: jnp.minimum(k, nk - 1)
    return pl.pallas_call(
        functools.partial(_ffn_out_kernel, tm=tm, tk=tk, n_ctx=n_ctx, t_total=t, final_norm=final_norm),
        grid=(t // tm, nk + 1),
        in_specs=[pl.BlockSpec((tm, tk), lambda i, k: (i, kc(k))),
                  pl.BlockSpec((w, tk), lambda i, k: (jnp.maximum(i * rpt - 1, 0), kc(k))),
                  pl.BlockSpec((w, tk), lambda i, k: (jnp.minimum((i + 1) * rpt, n_rows - 1), kc(k))),
                  pl.BlockSpec((tm, tk), lambda i, k: (i, kc(k))),
                  pl.BlockSpec((9, tk), lambda i, k: (0, kc(k))),
                  pl.BlockSpec((1, tk), lambda i, k: (0, kc(k))),
                  pl.BlockSpec((tk, d), lambda i, k: (jnp.maximum(k - 1, 0), 0)),
                  spec_x,
                  pl.BlockSpec((1, SUBLANES, 6 * d), lambda i, k: (layer, 0, 0)),
                  pl.BlockSpec((1, d), lambda i, k: (0, 0))] + extra_in,
        out_specs=out_specs,
        out_shape=out_shape,
        scratch_shapes=[pltpu.VMEM((tm, d), F32), pltpu.VMEM((tm, tk), BF16), pltpu.VMEM((tm, tk), BF16)],
        compiler_params=_cparams(2),
        name="ffn_conv_glu_out",
    )(gate, gate, gate, up, dw9, dwb, wo, x, mod, norm_g.reshape(1, d), *extra_args)


def _retention_kernel(*refs, rev, final, c):
    if final:
        q_ref, k_ref, v_ref, lg_ref, gate_ref, ob_ref, o_ref, r_scr, di_scr, dq_scr, dk_scr, dc_scr = refs
    else:
        q_ref, k_ref, v_ref, lg_ref, o_ref, r_scr, di_scr, dq_scr, dk_scr, dc_scr = refs
    d = 1 if rev else 0
    dk = RET_DK

    @pl.when(pl.program_id(0) == 0)
    def _():
        r_scr[...] = jnp.zeros_like(r_scr)
        i = lax.broadcasted_iota(jnp.int32, (c, c), 0)
        j = lax.broadcasted_iota(jnp.int32, (c, c), 1)
        diff = ((j - i) if rev else (i - j)).astype(F32)
        row = lax.broadcasted_iota(jnp.int32, (c, dk), 0)
        pos = ((c - 1 - row) if rev else row).astype(F32)
        for h in range(RET_HEADS):
            lg = _log_sigmoid(lg_ref[d:d + 1, h:h + 1])
            di_scr[h] = jnp.where(diff >= 0, jnp.exp(lg * jnp.maximum(diff, 0.0)), 0.0)
            dq_scr[h] = jnp.exp(lg * (pos + 1.0))
            dk_scr[h] = jnp.exp(lg * (c - 1.0 - pos)) * (dk ** -0.5)
            dc_scr[h] = jnp.exp(jnp.broadcast_to(lg, (1, dk)) * c)

    for h in range(RET_HEADS):
        sl = slice(h * dk, (h + 1) * dk)
        qh = q_ref[:, sl]
        kh = k_ref[:, sl]
        vh = v_ref[:, sl].astype(BF16)
        r_state = r_scr[h]
        scores = _bdot_nt(qh, kh) * (di_scr[h] * (dk ** -0.5))
        out = _bdot(scores, vh) + _bdot(qh * dq_scr[h], r_state)
        r_scr[h] = dc_scr[h] * r_state + _bdot_tn(kh * dk_scr[h], vh)
        if final:
            tot = out + ob_ref[:, sl]
            y = tot * lax.rsqrt(jnp.mean(tot * tot, axis=-1, keepdims=True) + EPS) * _silu(gate_ref[:, sl])
            o_ref[:, sl] = y.astype(o_ref.dtype)
        else:
            o_ref[:, sl] = out


def _retention_call(p, logit, other, rev, n_ctx):
    t = p.shape[0]
    c = RET_CHUNK
    n, nc = t // c, n_ctx // c
    width = RET_HEADS * RET_DK
    final = other is not None
    cm = lambda s: _chunk_index(s, rev, nc, n)
    in_specs = [pl.BlockSpec((c, width), lambda s: (cm(s), 0)),
                pl.BlockSpec((c, width), lambda s: (cm(s), 1)),
                pl.BlockSpec((c, width), lambda s: (cm(s), 2)),
                pl.BlockSpec((2, RET_HEADS), lambda s: (0, 0))]
    args = [p, p, p, logit]
    if final:
        in_specs += [pl.BlockSpec((c, width), lambda s: (cm(s), 3)), pl.BlockSpec((c, width), lambda s: (cm(s), 0))]
        args += [p, other]
    return pl.pallas_call(
        functools.partial(_retention_kernel, rev=rev, final=final, c=c),
        grid=(n,),
        in_specs=in_specs,
        out_specs=pl.BlockSpec((c, width), lambda s: (cm(s), 0)),
        out_shape=jax.ShapeDtypeStruct((t, width), BF16 if final else F32),
        scratch_shapes=[pltpu.VMEM((RET_HEADS, RET_DK, RET_DK), F32),
                        pltpu.VMEM((RET_HEADS, c, c), F32),
                        pltpu.VMEM((RET_HEADS, c, RET_DK), F32),
                        pltpu.VMEM((RET_HEADS, c, RET_DK), F32),
                        pltpu.VMEM((RET_HEADS, 1, RET_DK), F32)],
        compiler_params=_cparams(1),
        name="retention_bwd" if rev else "retention_fwd_merge",
    )(*args)


def _rwkv_kernel(*refs, rev, final, c, nc):
    if final:
        (r_ref, k_ref, v_ref, wd_ref, ad_ref, mu_ref, mus_ref, vec_ref, wup_ref, aup_ref,
         gl_ref, gup_ref, yb_ref, bb_ref, o_ref, st_scr, carry_scr) = refs
    else:
        (r_ref, k_ref, v_ref, wd_ref, ad_ref, mu_ref, mus_ref, vec_ref, wup_ref, aup_ref,
         y_ref, bonus_ref, st_scr, carry_scr) = refs
    d = 1 if rev else 0
    s = pl.program_id(0)
    n = RWKV_N
    wdt = RWKV_W

    @pl.when(s == 0)
    def _():
        st_scr[...] = jnp.zeros_like(st_scr)

    @pl.when(jnp.logical_or(s == 0, s == nc))
    def _():
        carry_scr[...] = jnp.zeros_like(carry_scr)

    row = lax.broadcasted_iota(jnp.int32, (c, 1), 0)
    edge = (row == c - 1) if rev else (row == 0)
    keep = c - 1 if not rev else 0

    def shifted(x, lo):
        width = x.shape[1]
        prev = pltpu.roll(x, (c - 1) if rev else 1, axis=0)
        prev = jnp.where(edge, carry_scr[0:1, lo:lo + width], prev)
        carry_scr[0:1, lo:lo + width] = x[keep:keep + 1, :]
        return prev

    def mix(x, lo, mu):
        prev = shifted(x, lo)
        return x + (prev - x) * mu

    r = mix(r_ref[...], 0, mu_ref[0:1, 0:wdt])
    k = mix(k_ref[...], wdt, mu_ref[0:1, wdt:2 * wdt])
    v = mix(v_ref[...], 2 * wdt, mu_ref[0:1, 2 * wdt:3 * wdt])
    wd = mix(wd_ref[...], 3 * wdt, mus_ref[0:1, :])[:, d * RWKV_LORA:(d + 1) * RWKV_LORA]
    ad = mix(ad_ref[...], 3 * wdt + LANES, mus_ref[1:2, :])[:, d * RWKV_LORA:(d + 1) * RWKV_LORA]

    w0, a0 = vec_ref[0:1, :], vec_ref[1:2, :]
    k_k, k_a, r_k = vec_ref[2:3, :], vec_ref[3:4, :], vec_ref[4:5, :]
    w_log = -_softplus(-(w0 + _bdot(jnp.tanh(wd), wup_ref[...]))) - 0.5
    lw = -jnp.exp(w_log)
    a = _sigmoid(a0 + _bdot(ad, aup_ref[...]))
    kk = k * k_k
    kk = kk * lax.rsqrt(_seg_sum(kk * kk, n) + L2_EPS)
    k = k * (1.0 + (a - 1.0) * k_a)
    bonus = _seg_sum(r * k * r_k, n) * v
    b = kk * a

    incl, strict = _order_masks(c, rev)
    cum = _dot_mask_lhs(jnp.where(incl, 1.0, 0.0).astype(BF16), lw)
    cum_end = cum[keep:keep + 1, :]
    e_pos = jnp.exp(cum)
    e_neg = jnp.exp(-cum)
    r_t = r * e_pos
    a_t = -kk * jnp.exp(cum - lw)
    k_t = k * e_neg
    b_t = b * e_neg
    w_end = jnp.exp(cum_end)
    kw = k_t * w_end
    bw = b_t * w_end
    eye = lax.broadcasted_iota(jnp.int32, (n, n), 0) == lax.broadcasted_iota(jnp.int32, (n, n), 1)

    ys = []
    for h0 in range(0, RWKV_HEADS, RWKV_GROUP):
        hs = range(h0, h0 + RWKV_GROUP)
        sls = [slice(h * n, (h + 1) * n) for h in hs]
        s1 = [_bdot_nt(jnp.concatenate([a_t[:, sl], r_t[:, sl]], axis=0),
                       jnp.concatenate([b_t[:, sl], k_t[:, sl]], axis=0)) for sl in sls]
        a_ab = [jnp.where(strict, x[:c, :c], 0.0) for x in s1]
        a_lo = [jnp.concatenate([jnp.where(strict, x[:c, c:], 0.0), jnp.where(incl, x[c:, c:], 0.0)], axis=0) for x in s1]
        a_rb = [jnp.where(incl, x[c:, :c], 0.0) for x in s1]
        tms = _neumann_inverse_multi(a_ab, c, _dot_inv)
        av = [_bdot(a, v[:, sl]) for a, sl in zip(a_lo, sls)]
        pp = [_dot_inv(t, jnp.concatenate([a_t[:, sl], x[:c]], axis=1)) for t, sl, x in zip(tms, sls, av)]
        rq_y0 = [_bdot(a, p) for a, p in zip(a_rb, pp)]
        m_g = [_bdot_tn(bw[:, sl], p) for sl, p in zip(sls, pp)]
        kv = [_bdot_tn(kw[:, sl], v[:, sl]) for sl in sls]
        so = []
        for i, h in enumerate(hs):
            rq = r_t[:, sls[i]] + rq_y0[i][:, :n]
            m = m_g[i][:, :n] + jnp.where(eye, jnp.broadcast_to(w_end[:, sls[i]], (n, n)), 0.0)
            so.append(_dot_state(jnp.concatenate([rq, m], axis=0), st_scr[h]))
        for i, h in enumerate(hs):
            st_scr[h] = so[i][c:] + (m_g[i][:, n:] + kv[i])
            ys.append(so[i][:c] + (rq_y0[i][:, n:] + av[i][c:]))
    y = jnp.concatenate(ys, axis=1)

    if final:
        ysum = y + yb_ref[...]
        mu_h = _seg_sum(ysum, n) * (1.0 / n)
        yc = ysum - mu_h
        var = _seg_sum(yc * yc, n) * (1.0 / n)
        ln_g, ln_b = vec_ref[5:6, :], vec_ref[6:7, :]
        yn = yc * lax.rsqrt(var + GN_EPS) * ln_g + ln_b + bonus + bb_ref[...]
        gate = _bdot(_sigmoid(gl_ref[...]), gup_ref[...])
        o_ref[...] = (yn * gate).astype(o_ref.dtype)
    else:
        y_ref[...] = y
        bonus_ref[...] = bonus


def _rwkv_call(p, prm, other, rev, n_ctx):
    t = p.shape[0]
    c = RWKV_CHUNK
    n, nc = t // c, n_ctx // c
    wdt = RWKV_W
    d = 1 if rev else 0
    final = other is not None
    cm = lambda s: _chunk_index(s, rev, nc, n)
    base = 4096 // wdt
    full = lambda shape: pl.BlockSpec(shape, lambda s: (0,) * len(shape))
    in_specs = [pl.BlockSpec((c, wdt), lambda s: (cm(s), base)),
                pl.BlockSpec((c, wdt), lambda s: (cm(s), base + 1)),
                pl.BlockSpec((c, wdt), lambda s: (cm(s), base + 2)),
                pl.BlockSpec((c, LANES), lambda s: (cm(s), 7168 // LANES)),
                pl.BlockSpec((c, LANES), lambda s: (cm(s), 7168 // LANES + 1)),
                full((1, 3 * wdt)), full((2, LANES)), full((SUBLANES, wdt)),
                full((RWKV_LORA, wdt)), full((RWKV_LORA, wdt))]
    args = [p, p, p, p, p, prm["mu"][d], prm["mu_small"][d], prm["vecs"][d], prm["w_up"][d], prm["a_up"][d]]
    if final:
        in_specs += [pl.BlockSpec((c, 2 * LANES), lambda s: (cm(s), 7424 // (2 * LANES))),
                     full((2 * LANES, wdt)),
                     pl.BlockSpec((c, wdt), lambda s: (cm(s), 0)),
                     pl.BlockSpec((c, wdt), lambda s: (cm(s), 0))]
        args += [p, prm["g_up"], other[0], other[1]]
        out_specs = pl.BlockSpec((c, wdt), lambda s: (cm(s), 0))
        out_shape = jax.ShapeDtypeStruct((t, wdt), BF16)
    else:
        out_specs = [pl.BlockSpec((c, wdt), lambda s: (cm(s), 0))] * 2
        out_shape = [jax.ShapeDtypeStruct((t, wdt), F32)] * 2
    return pl.pallas_call(
        functools.partial(_rwkv_kernel, rev=rev, final=final, c=c, nc=nc),
        grid=(n,),
        in_specs=in_specs,
        out_specs=out_specs,
        out_shape=out_shape,
        scratch_shapes=[pltpu.VMEM((RWKV_HEADS, RWKV_N, RWKV_N), F32),
                        pltpu.VMEM((SUBLANES, 3 * wdt + 2 * LANES), F32)],
        compiler_params=_cparams(1),
        name="rwkv7_bwd" if rev else "rwkv7_fwd_merge",
    )(*args)


def _mlstm_kernel(*refs, rev, final, c):
    if final:
        q_ref, k_ref, v_ref, sm_ref, bias_ref, og_ref, ng_ref, hb_ref, o_ref, ct_scr, m_scr = refs
    else:
        q_ref, k_ref, v_ref, sm_ref, bias_ref, o_ref, ct_scr, m_scr = refs
    d = 1 if rev else 0
    dk, dv = MLSTM_DK, MLSTM_DV
    nh = MLSTM_HEADS

    @pl.when(pl.program_id(0) == 0)
    def _():
        ct_scr[...] = jnp.zeros_like(ct_scr)
        m_scr[...] = jnp.zeros_like(m_scr)

    incl, _ = _order_masks(c, rev)
    keep = 0 if rev else c - 1
    sm = sm_ref[...]
    li = sm + bias_ref[0:1, :]
    lf = _log_sigmoid(sm + bias_ref[1:2, :])
    bcum = _dot_mask_lhs(jnp.where(incl, 1.0, 0.0).astype(BF16), lf)
    bcum_t = bcum.T
    li_t = li.T
    ones_col = jnp.where(lax.broadcasted_iota(jnp.int32, (c, LANES), 1) == 0, 1.0, 0.0).astype(BF16)

    heads = range(nh)
    cis = [d * 2 * nh + h for h in heads]
    b_col = [bcum[:, ci + nh:ci + nh + 1] for ci in cis]
    i_col = [li[:, ci:ci + 1] for ci in cis]
    m_prev = [m_scr[h:h + 1, 0:1] for h in heads]
    qs = [q_ref[:, h * dk:(h + 1) * dk] * (dk ** -0.5) for h in heads]
    ks = [k_ref[:, h * dk:(h + 1) * dk] for h in heads]
    v_aug = [jnp.concatenate([v_ref[:, h * dv:(h + 1) * dv].astype(BF16), ones_col], axis=1) for h in heads]
    qk_raw = [_bdot_nt(q, k) for q, k in zip(qs, ks)]
    q_ct = [_bdot(q, ct_scr[h]) for q, h in zip(qs, heads)]
    d_log = [jnp.where(incl, bc - bcum_t[ci + nh:ci + nh + 1, :] + li_t[ci:ci + 1, :], -jnp.inf)
             for bc, ci in zip(b_col, cis)]
    inter = [bc + mp for bc, mp in zip(b_col, m_prev)]
    m_t = [jnp.maximum(jnp.max(dl, axis=-1, keepdims=True), it) for dl, it in zip(d_log, inter)]
    qk = [x * jnp.exp(dl - mt) for x, dl, mt in zip(qk_raw, d_log, m_t)]
    num_aug = [_bdot(x, va) + jnp.exp(it - mt) * qc for x, va, it, mt, qc in zip(qk, v_aug, inter, m_t, q_ct)]
    b_end = [bc[keep:keep + 1, :] for bc in b_col]
    w_log = [be - bc + ic for be, bc, ic in zip(b_end, b_col, i_col)]
    m_new = [jnp.maximum(be + mp, jnp.max(wl, axis=0, keepdims=True)) for be, mp, wl in zip(b_end, m_prev, w_log)]
    kv = [_bdot_tn(k * jnp.exp(wl - mn), va) for k, wl, mn, va in zip(ks, w_log, m_new, v_aug)]
    for h in heads:
        ct_scr[h] = jnp.exp(b_end[h] + m_prev[h] - m_new[h]) * ct_scr[h] + kv[h]
        m_scr[h:h + 1, :] = jnp.broadcast_to(m_new[h], (1, LANES))
        den = num_aug[h][:, dv:dv + 1]
        hout = num_aug[h][:, :dv] / jnp.maximum(jnp.abs(den), jnp.exp(-m_t[h]))
        sl = slice(h * dv, (h + 1) * dv)
        if final:
            tot = hout + hb_ref[:, sl]
            y = tot * lax.rsqrt(jnp.mean(tot * tot, axis=-1, keepdims=True) + EPS)
            o_ref[:, sl] = (y * ng_ref[0:1, sl] * _sigmoid(og_ref[:, sl])).astype(o_ref.dtype)
        else:
            o_ref[:, sl] = hout


def _mlstm_call(p, bias_rows, norm_g, other, rev, n_ctx):
    t = p.shape[0]
    c = MLSTM_CHUNK
    n, nc = t // c, n_ctx // c
    dk, dv, nh = MLSTM_DK, MLSTM_DV, MLSTM_HEADS
    final = other is not None
    cm = lambda s: _chunk_index(s, rev, nc, n)
    full = lambda shape: pl.BlockSpec(shape, lambda s: (0,) * len(shape))
    in_specs = [pl.BlockSpec((c, nh * dk), lambda s: (cm(s), 0)),
                pl.BlockSpec((c, nh * dk), lambda s: (cm(s), 1)),
                pl.BlockSpec((c, nh * dv), lambda s: (cm(s), 1)),
                pl.BlockSpec((c, LANES), lambda s: (cm(s), 7168 // LANES)),
                full((2, LANES))]
    args = [p, p, p, p, bias_rows]
    if final:
        in_specs += [pl.BlockSpec((c, nh * dv), lambda s: (cm(s), 2)), full((1, nh * dv)),
                     pl.BlockSpec((c, nh * dv), lambda s: (cm(s), 0))]
        args += [p, norm_g.reshape(1, nh * dv), other]
    return pl.pallas_call(
        functools.partial(_mlstm_kernel, rev=rev, final=final, c=c),
        grid=(n,),
        in_specs=in_specs,
        out_specs=pl.BlockSpec((c, nh * dv), lambda s: (cm(s), 0)),
        out_shape=jax.ShapeDtypeStruct((t, nh * dv), BF16 if final else F32),
        scratch_shapes=[pltpu.VMEM((nh, dk, dv + LANES), F32), pltpu.VMEM((SUBLANES, LANES), F32)],
        compiler_params=_cparams(1),
        name="mlstm_bwd" if rev else "mlstm_fwd_merge",
    )(*args)


def _gdn_prep_kernel(xm_ref, xp_ref, xn_ref, cw_ref, sm_ref, gp_ref, q_ref, k_ref, v_ref, gb_ref, *, tm, n_ctx, t_total):
    i = pl.program_id(0)
    x = xm_ref[...]
    rows = i * tm + lax.broadcasted_iota(jnp.int32, (tm, 1), 0)
    first = rows == i * tm
    last = rows == i * tm + tm - 1
    prev = jnp.where(first, xp_ref[SUBLANES - 1:SUBLANES, :], pltpu.roll(x, 1, axis=0))
    nxt = jnp.where(last, xn_ref[0:1, :], pltpu.roll(x, tm - 1, axis=0))
    prev = jnp.where(jnp.logical_or(rows == 0, rows == n_ctx), 0.0, prev)
    nxt = jnp.where(jnp.logical_or(rows == n_ctx - 1, rows == t_total - 1), 0.0, nxt)
    y = _silu(prev * cw_ref[0:1, :] + x * cw_ref[1:2, :] + nxt * cw_ref[2:3, :])
    w = GDN_HEADS * GDN_DK
    q, k, v = y[:, :w], y[:, w:2 * w], y[:, 2 * w:]
    q_ref[...] = (q * lax.rsqrt(_seg_sum(q * q, GDN_DK) + L2_EPS) * (GDN_DK ** -0.5)).astype(BF16)
    k_ref[...] = (k * lax.rsqrt(_seg_sum(k * k, GDN_DK) + L2_EPS)).astype(BF16)
    v_ref[...] = v.astype(BF16)
    sm = sm_ref[...]
    lane = lax.broadcasted_iota(jnp.int32, sm.shape, 1)
    log_alpha = -jnp.exp(gp_ref[0:1, :]) * _softplus(sm + gp_ref[1:2, :])
    gb_ref[...] = jnp.where(lane < 32, log_alpha, _sigmoid(sm))


def _gdn_prep_call(p, conv_w, gate_params, n_ctx):
    t = p.shape[0]
    tm = 256
    w = GDN_HEADS * GDN_DK
    qkv_blk = 3072 // GDN_QKV
    nb8 = t // SUBLANES
    r8 = tm // SUBLANES
    spec_o = pl.BlockSpec((tm, w), lambda i: (i, 0))
    return pl.pallas_call(
        functools.partial(_gdn_prep_kernel, tm=tm, n_ctx=n_ctx, t_total=t),
        grid=(t // tm,),
        in_specs=[pl.BlockSpec((tm, GDN_QKV), lambda i: (i, qkv_blk)),
                  pl.BlockSpec((SUBLANES, GDN_QKV), lambda i: (jnp.maximum(i * r8 - 1, 0), qkv_blk)),
                  pl.BlockSpec((SUBLANES, GDN_QKV), lambda i: (jnp.minimum((i + 1) * r8, nb8 - 1), qkv_blk)),
                  pl.BlockSpec((3, GDN_QKV), lambda i: (0, 0)),
                  pl.BlockSpec((tm, LANES), lambda i: (i, 7168 // LANES)),
                  pl.BlockSpec((2, LANES), lambda i: (0, 0))],
        out_specs=[spec_o, spec_o, spec_o, pl.BlockSpec((tm, LANES), lambda i: (i, 0))],
        out_shape=[jax.ShapeDtypeStruct((t, w), BF16)] * 3 + [jax.ShapeDtypeStruct((t, LANES), F32)],
        compiler_params=_cparams(1),
        name="gdn_conv_norm_gates",
    )(p, p, p, conv_w, p, gate_params)


def _gdn_kernel(*refs, rev, final, c):
    if final:
        q_ref, k_ref, v_ref, gb_ref, gate_ref, ng_ref, ob_ref, o_ref, st_scr = refs
    else:
        q_ref, k_ref, v_ref, gb_ref, o_ref, st_scr = refs
    d = 1 if rev else 0
    dk = GDN_DK
    nh = GDN_HEADS

    @pl.when(pl.program_id(0) == 0)
    def _():
        st_scr[...] = jnp.zeros_like(st_scr)

    incl, strict = _order_masks(c, rev)
    keep = 0 if rev else c - 1
    gb = gb_ref[...]
    gc = _dot_mask_lhs(jnp.where(incl, 1.0, 0.0).astype(BF16), gb)
    gc_t = gc.T

    for h0 in range(0, nh, GDN_GROUP):
        hs = list(range(h0, h0 + GDN_GROUP))
        sls = [slice(h * dk, (h + 1) * dk) for h in hs]
        cols = [16 + d * nh + h for h in hs]
        g_col = [gc[:, cc:cc + 1] for cc in cols]
        beta = [gb[:, cc + 16:cc + 17] for cc in cols]
        g_end = [g[keep:keep + 1, :] for g in g_col]
        e_g = [jnp.exp(g) for g in g_col]
        decay = [jnp.exp(jnp.where(incl, g - gc_t[cc:cc + 1, :], -jnp.inf)) for g, cc in zip(g_col, cols)]
        qs = [q_ref[:, sl].astype(F32) for sl in sls]
        ks = [k_ref[:, sl].astype(F32) for sl in sls]
        vs = [v_ref[:, sl].astype(F32) for sl in sls]
        kq = [_bdot_nt(jnp.concatenate([kh * b, qh], axis=0), kh) for kh, qh, b in zip(ks, qs, beta)]
        a_qk = [x[c:] * dc for x, dc in zip(kq, decay)]
        tinv = _neumann_inverse_multi([jnp.where(strict, -(x[:c] * dc), 0.0) for x, dc in zip(kq, decay)], c, _dot_inv)
        uw = [_dot_inv(t, jnp.concatenate([vh * b, kh * (b * e)], axis=1))
              for t, vh, kh, b, e in zip(tinv, vs, ks, beta, e_g)]
        o_part = [_bdot(a, u) for a, u in zip(a_qk, uw)]
        s_part = [_bdot_tn(kh * jnp.exp(ge - g), u) for kh, ge, g, u in zip(ks, g_end, g_col, uw)]
        so = [_dot_state(jnp.concatenate([qh * e - op[:, dk:], sp[:, dk:]], axis=0), st_scr[h])
              for qh, e, op, sp, h in zip(qs, e_g, o_part, s_part, hs)]
        for i, h in enumerate(hs):
            sl = sls[i]
            out = so[i][:c] + o_part[i][:, :dk]
            st_scr[h] = jnp.exp(g_end[i]) * st_scr[h] - so[i][c:] + s_part[i][:, :dk]
            if final:
                tot = out + ob_ref[:, sl]
                y = tot * lax.rsqrt(jnp.mean(tot * tot, axis=-1, keepdims=True) + EPS)
                o_ref[:, sl] = (y * ng_ref[0:1, sl] * _silu(gate_ref[:, sl])).astype(o_ref.dtype)
            else:
                o_ref[:, sl] = out


def _gdn_call(p, q, k, v, gb, norm_g, other, rev, n_ctx):
    t = q.shape[0]
    c = GDN_CHUNK
    n, nc = t // c, n_ctx // c
    w = GDN_HEADS * GDN_DK
    final = other is not None
    cm = lambda s: _chunk_index(s, rev, nc, n)
    blk = pl.BlockSpec((c, w), lambda s: (cm(s), 0))
    in_specs = [blk, blk, blk, pl.BlockSpec((c, LANES), lambda s: (cm(s), 0))]
    args = [q, k, v, gb]
    if final:
        in_specs += [pl.BlockSpec((c, w), lambda s: (cm(s), 6144 // w)), pl.BlockSpec((1, w), lambda s: (0, 0)), blk]
        args += [p, norm_g.reshape(1, w), other]
    return pl.pallas_call(
        functools.partial(_gdn_kernel, rev=rev, final=final, c=c),
        grid=(n,),
        in_specs=in_specs,
        out_specs=blk,
        out_shape=jax.ShapeDtypeStruct((t, w), BF16 if final else F32),
        scratch_shapes=[pltpu.VMEM((GDN_HEADS, GDN_DK, GDN_DK), F32)],
        compiler_params=_cparams(1),
        name="gdn_bwd" if rev else "gdn_fwd_merge",
    )(*args)


def _pad_cols(w, width):
    return jnp.pad(w, ((0, 0), (0, width - w.shape[1])))


def _lane_row(pieces):
    row = jnp.zeros((LANES,), F32)
    for off, vec in pieces:
        row = row.at[off:off + vec.shape[0]].set(vec.astype(F32))
    return row


def _odd_weight(w):
    cols = [w[:, 0:3072], w[:, 3088:6160], w[:, 6160:7184], w[:, 3072:3088], w[:, 7184:7216]]
    return _pad_cols(jnp.concatenate(cols, axis=1), F_PAD).astype(BF16)


def _rwkv_params(e, rwkv_mu, rwkv_w0, rwkv_w_up, rwkv_a0, rwkv_a_up, rwkv_k_k, rwkv_k_a, rwkv_r_k,
                 rwkv_ln_g, rwkv_ln_b, rwkv_g_up):
    wdt = RWKV_W
    mu = rwkv_mu[e]
    zeros64 = jnp.zeros((RWKV_LORA,), F32)
    mu_small, vecs = [], []
    for d in range(2):
        m_wd = mu[d, 3 * wdt:3 * wdt + RWKV_LORA]
        m_ad = mu[d, 3 * wdt + RWKV_LORA:]
        lo = [m_wd, zeros64] if d == 0 else [zeros64, m_wd]
        la = [m_ad, zeros64] if d == 0 else [zeros64, m_ad]
        mu_small.append(jnp.stack([jnp.concatenate(lo), jnp.concatenate(la)]))
        vecs.append(jnp.stack([rwkv_w0[e, d], rwkv_a0[e, d], rwkv_k_k[e], rwkv_k_a[e], rwkv_r_k[e].reshape(wdt),
                               rwkv_ln_g[e], rwkv_ln_b[e], jnp.zeros((wdt,), F32)]))
    return {
        "mu": mu[:, None, :3 * wdt],
        "mu_small": jnp.stack(mu_small),
        "vecs": jnp.stack(vecs),
        "w_up": rwkv_w_up[e].astype(BF16),
        "a_up": rwkv_a_up[e].astype(BF16),
        "g_up": jnp.pad(rwkv_g_up[e], ((0, 2 * LANES - RWKV_GATE_LORA), (0, 0))).astype(BF16),
    }


def _even_mixer(p, e, n_ctx, ret_decay_logit, rwkv_prm):
    logit = ret_decay_logit[e].astype(F32)
    ret_b = _retention_call(p, logit, None, True, n_ctx)
    y_ret = _retention_call(p, logit, ret_b, False, n_ctx)
    rw_b = _rwkv_call(p, rwkv_prm, None, True, n_ctx)
    y_rwkv = _rwkv_call(p, rwkv_prm, rw_b, False, n_ctx)
    return y_ret, y_rwkv


def _odd_mixer(p, o, n_ctx, mlstm_gate_b, mlstm_norm_g, gdn_conv, gdn_a_log, gdn_dt_bias, gdn_norm_g):
    gate_b = mlstm_gate_b[o]
    nh = MLSTM_HEADS
    bias_i = _lane_row([(d * 2 * nh, gate_b[d, 0]) for d in range(2)])
    bias_f = _lane_row([(d * 2 * nh + nh, gate_b[d, 1]) for d in range(2)])
    bias_rows = jnp.stack([bias_i, bias_f])
    ml_b = _mlstm_call(p, bias_rows, mlstm_norm_g[o], None, True, n_ctx)
    y_ml = _mlstm_call(p, bias_rows, mlstm_norm_g[o], ml_b, False, n_ctx)

    gate_params = jnp.stack([_lane_row([(16, gdn_a_log[o].reshape(-1))]), _lane_row([(16, gdn_dt_bias[o].reshape(-1))])])
    q, k, v, gb = _gdn_prep_call(p, gdn_conv[o], gate_params, n_ctx)
    gd_b = _gdn_call(p, q, k, v, gb, gdn_norm_g[o], None, True, n_ctx)
    y_gd = _gdn_call(p, q, k, v, gb, gdn_norm_g[o], gd_b, False, n_ctx)
    return y_ml, y_gd


def kernel(x, c, ctx, c_ctx, ada_w, ada_b, norm1_g, norm2_g, mix_w_out, ffn_w_in, ffn_dw, ffn_dw_b, ffn_w_out, final_norm_g, ev_w_in, ret_decay_logit, rwkv_mu, rwkv_w0, rwkv_w_up, rwkv_a0, rwkv_a_up, rwkv_k_k, rwkv_k_a, rwkv_r_k, rwkv_ln_g, rwkv_ln_b, rwkv_g_up, od_w_in, mlstm_gate_b, mlstm_norm_g, gdn_conv, gdn_a_log, gdn_dt_bias, gdn_norm_g):
    assert x.shape[0] == 1 and ctx.shape[0] == 1
    n_ctx = ctx.shape[1]
    n_lat = x.shape[1]
    depth = ada_w.shape[0]
    cvecs = jnp.stack([c_ctx, c[0]], axis=1)
    mod = _ada_call(cvecs, ada_w, ada_b)
    xt, h1 = _assemble_call(ctx, x, norm1_g[0], mod)

    for l in range(depth):
        last = l == depth - 1
        half = D_MODEL // 2
        if l % 2 == 0:
            e = l // 2
            p = _proj_call(h1, ev_w_in[e], F_PAD)
            prm = _rwkv_params(e, rwkv_mu, rwkv_w0, rwkv_w_up, rwkv_a0, rwkv_a_up, rwkv_k_k, rwkv_k_a, rwkv_r_k,
                               rwkv_ln_g, rwkv_ln_b, rwkv_g_up)
            ya, yb = _even_mixer(p, e, n_ctx, ret_decay_logit, prm)
        else:
            o = l // 2
            p = _proj_call(h1, _odd_weight(od_w_in[o]), F_PAD)
            ya, yb = _odd_mixer(p, o, n_ctx, mlstm_gate_b, mlstm_norm_g, gdn_conv, gdn_a_log, gdn_dt_bias, gdn_norm_g)
        w_mix = mix_w_out[l].astype(BF16)
        xt, h2 = _outproj_call(ya, yb, w_mix[:half], w_mix[half:], xt, norm2_g[l], mod, l, n_ctx)
        gate, up, wo = _ffn_in_call(h2, ffn_w_in, ffn_w_out, l)
        dw9 = _pad_cols(ffn_dw[l].reshape(9, D_FF), FF_PAD)
        dwb = _pad_cols(ffn_dw_b[l].reshape(1, D_FF), FF_PAD)
        if last:
            xt = _ffn_out_call(gate, up, dw9, dwb, wo, xt, mod, l, final_norm_g, n_ctx, True)
        else:
            xt, h1 = _ffn_out_call(gate, up, dw9, dwb, wo, xt, mod, l, norm1_g[l + 1], n_ctx, False)
    return xt[n_ctx:][None]
```

```python
import functools
import math

import jax
import jax.numpy as jnp
from jax import lax
from jax.experimental import pallas as pl
from jax.experimental.pallas import tpu as pltpu

F32 = jnp.float32
BF16 = jnp.bfloat16

D_MODEL = 2048
DEPTH = 2
GRID_W = 64
EPS = 1e-6
GN_EPS = 64e-5
L2_EPS = 1e-12
D_FF = 5504

RET_HEADS = 8
RET_DK = 128
RWKV_HEADS = 16
RWKV_N = 64
RWKV_W = RWKV_HEADS * RWKV_N
RWKV_LORA = 64
RWKV_GATE_LORA = 160
MLSTM_HEADS = 4
MLSTM_DK = 128
MLSTM_DV = 256
GDN_HEADS = 8
GDN_DK = 128
GDN_QKV = 3072

LANES = 128
SUBLANES = 8
MXU_N = 256
VMEM_LIMIT = 56 * 1024 * 1024

F_PAD = 7680
FF_PAD = 5632
ROW_TILE = 768
RET_CHUNK = 256
MLSTM_CHUNK = 256
GDN_CHUNK = 64
RWKV_CHUNK = 64
RWKV_GROUP = 8
GDN_GROUP = 8


def _cparams(n_axes):
    return pltpu.CompilerParams(dimension_semantics=("arbitrary",) * n_axes, vmem_limit_bytes=VMEM_LIMIT)


def _bdot(a, b):
    return jnp.dot(a.astype(BF16), b.astype(BF16), preferred_element_type=F32)


def _bdot_nt(a, b):
    return lax.dot_general(a.astype(BF16), b.astype(BF16), (((1,), (1,)), ((), ())), preferred_element_type=F32)


def _bdot_tn(a, b):
    return lax.dot_general(a.astype(BF16), b.astype(BF16), (((0,), (0,)), ((), ())), preferred_element_type=F32)


def _split3(x):
    x1 = x.astype(BF16)
    r1 = x - x1.astype(F32)
    x2 = r1.astype(BF16)
    x3 = (r1 - x2.astype(F32)).astype(BF16)
    return x1, x2, x3


def _dot_mask_lhs(m_bf16, x):
    x1, x2, x3 = _split3(x)
    d = lambda t: jnp.dot(m_bf16, t, preferred_element_type=F32)
    return (d(x3) + d(x2)) + d(x1)


def _dot3(a, b):
    a1 = a.astype(BF16)
    a2 = (a - a1.astype(F32)).astype(BF16)
    b1 = b.astype(BF16)
    b2 = (b - b1.astype(F32)).astype(BF16)
    d = lambda u, v: jnp.dot(u, v, preferred_element_type=F32)
    return (d(a2, b1) + d(a1, b2)) + d(a1, b1)


def _sigmoid(x):
    return 1.0 / (1.0 + jnp.exp(-x))


def _silu(x):
    return x * _sigmoid(x)


def _softplus(x):
    return jnp.maximum(x, 0.0) + jnp.log1p(jnp.exp(-jnp.abs(x)))


def _log_sigmoid(x):
    return -_softplus(-x)


def _order_masks(c, rev):
    i = lax.broadcasted_iota(jnp.int32, (c, c), 0)
    j = lax.broadcasted_iota(jnp.int32, (c, c), 1)
    if rev:
        return j >= i, j > i
    return j <= i, j < i


def _neumann_inverse(n, c, dotf):
    i = lax.broadcasted_iota(jnp.int32, (c, c), 0)
    j = lax.broadcasted_iota(jnp.int32, (c, c), 1)
    x = jnp.where(i == j, 1.0, 0.0).astype(F32) + n
    p = dotf(n, n)
    for _ in range(int(math.log2(c)) - 2):
        r = dotf(jnp.concatenate([x, p], axis=0), p)
        x = x + r[:c]
        p = r[c:]
    return x + dotf(x, p)


def _neumann_inverse_multi(ns, c, dotf):
    i = lax.broadcasted_iota(jnp.int32, (c, c), 0)
    j = lax.broadcasted_iota(jnp.int32, (c, c), 1)
    eye = jnp.where(i == j, 1.0, 0.0).astype(F32)
    xs = [eye + n for n in ns]
    ps = [dotf(n, n) for n in ns]
    for _ in range(int(math.log2(c)) - 2):
        rs = [dotf(jnp.concatenate([x, p], axis=0), p) for x, p in zip(xs, ps)]
        xs = [x + r[:c] for x, r in zip(xs, rs)]
        ps = [r[c:] for r in rs]
    return [x + dotf(x, p) for x, p in zip(xs, ps)]


def _pair_masks(c, rev):
    assert 2 * c == LANES and c == 64
    row = lax.broadcasted_iota(jnp.int32, (c, 2 * c), 0)
    lane = lax.broadcasted_iota(jnp.int32, (c, 2 * c), 1)
    src = jnp.bitwise_and(lane, c - 1)
    incl2 = (src >= row) if rev else (src <= row)
    strict2 = (src > row) if rev else (src < row)
    return incl2, strict2, src == row, lane < c


def _bdiag(x2, anti=False):
    n2 = x2.shape[1]
    blk_r = jnp.right_shift(lax.broadcasted_iota(jnp.int32, (n2, n2), 0), 6)
    blk_l = jnp.right_shift(lax.broadcasted_iota(jnp.int32, (n2, n2), 1), 6)
    stacked = jnp.concatenate([x2, x2], axis=0)
    if anti:
        return jnp.where(blk_r == blk_l, 0.0, stacked)
    return jnp.where(blk_r == blk_l, stacked, 0.0)


def _neumann_inverse_packed(ns, eye2, c, pdot):
    xs = [jnp.where(eye2, 1.0, 0.0) + n for n in ns]
    ps = [pdot(n, n) for n in ns]
    for _ in range(int(math.log2(c)) - 2):
        rs = [pdot(jnp.concatenate([x, p], axis=0), p) for x, p in zip(xs, ps)]
        xs = [x + r[:c] for x, r in zip(xs, rs)]
        ps = [r[c:] for r in rs]
    return [x + pdot(x, p) for x, p in zip(xs, ps)]


def _dot_inv(a, b):
    return _dot3(a, b)


def _dot_state(a, b):
    return _bdot(a, b)


def _chunk_index(s, rev, n_ctx_chunks, n_chunks):
    if not rev:
        return s
    return jnp.where(s < n_ctx_chunks, n_ctx_chunks - 1 - s, n_chunks + n_ctx_chunks - 1 - s)


def _seg_sum(x, seg):
    c, w = x.shape
    if seg == LANES:
        parts = [jnp.broadcast_to(jnp.sum(x[:, b * LANES:(b + 1) * LANES], axis=-1, keepdims=True), (c, LANES))
                 for b in range(w // LANES)]
        return jnp.concatenate(parts, axis=-1)
    assert seg * 2 == LANES
    lane = lax.broadcasted_iota(jnp.int32, (c, LANES), 1)
    low = lane < seg
    parts = []
    for b in range(w // LANES):
        xb = x[:, b * LANES:(b + 1) * LANES]
        s_lo = jnp.sum(jnp.where(low, xb, 0.0), axis=-1, keepdims=True)
        s_hi = jnp.sum(jnp.where(low, 0.0, xb), axis=-1, keepdims=True)
        parts.append(jnp.where(low, s_lo, s_hi))
    return jnp.concatenate(parts, axis=-1)


def _ada_kernel(c_ref, w_ref, b_ref, o_ref):
    cv = c_ref[...]
    s = _silu(cv)
    w = w_ref[0]
    bias = b_ref[0]
    r0 = jnp.sum(w * s[:, 0:1], axis=0, keepdims=True) + bias
    r1 = jnp.sum(w * s[:, 1:2], axis=0, keepdims=True) + bias
    row = lax.broadcasted_iota(jnp.int32, (SUBLANES, w.shape[1]), 0)
    o_ref[0] = jnp.where(row == 0, r0, jnp.where(row == 1, r1, 0.0))


def _ada_call(cvecs, ada_w, ada_b):
    depth, d, n = ada_w.shape
    tn = 1024
    return pl.pallas_call(
        _ada_kernel,
        grid=(depth, n // tn),
        in_specs=[pl.BlockSpec((d, 2), lambda l, j: (0, 0)),
                  pl.BlockSpec((1, d, tn), lambda l, j: (l, 0, j)),
                  pl.BlockSpec((1, 1, tn), lambda l, j: (l, 0, j))],
        out_specs=pl.BlockSpec((1, SUBLANES, tn), lambda l, j: (l, 0, j)),
        out_shape=jax.ShapeDtypeStruct((depth, SUBLANES, n), F32),
        compiler_params=_cparams(2),
        name="ada_modulation",
    )(cvecs, ada_w, ada_b.reshape(depth, 1, n))


def _mod_rows(mod, k, rows, n_ctx):
    d = D_MODEL
    vc = mod[0:1, k * d:(k + 1) * d]
    vl = mod[1:2, k * d:(k + 1) * d]
    return jnp.where(rows < n_ctx, vc, vl)


def _norm_mod(x, g, shift, scale):
    y = x * lax.rsqrt(jnp.mean(x * x, axis=-1, keepdims=True) + EPS) * g
    return y * (1.0 + scale) + shift


def _assemble_kernel(ctx_ref, x_ref, g_ref, mod_ref, xt_ref, h_ref):
    i = pl.program_id(0)
    d = D_MODEL
    mod = mod_ref[0]

    def emit(src, row):
        xt_ref[...] = src
        h = _norm_mod(src, g_ref[...], mod[row:row + 1, 0:d], mod[row:row + 1, d:2 * d])
        h_ref[...] = h.astype(BF16)

    @pl.when(i == 0)
    def _():
        emit(ctx_ref[0], 0)

    @pl.when(i > 0)
    def _():
        emit(x_ref[0], 1)


def _assemble_call(ctx, x, g, mod):
    n_ctx, d = ctx.shape[1], ctx.shape[2]
    t = n_ctx + x.shape[1]
    assert x.shape[1] % n_ctx == 0
    spec_o = pl.BlockSpec((n_ctx, d), lambda i: (i, 0))
    return pl.pallas_call(
        _assemble_kernel,
        grid=(t // n_ctx,),
        in_specs=[pl.BlockSpec((1, n_ctx, d), lambda i: (0, 0, 0)),
                  pl.BlockSpec((1, n_ctx, d), lambda i: (0, jnp.maximum(i - 1, 0), 0)),
                  pl.BlockSpec((1, d), lambda i: (0, 0)),
                  pl.BlockSpec((1, SUBLANES, 6 * d), lambda i: (0, 0, 0))],
        out_specs=[spec_o, spec_o],
        out_shape=[jax.ShapeDtypeStruct((t, d), F32), jax.ShapeDtypeStruct((t, d), BF16)],
        compiler_params=_cparams(1),
        name="assemble_norm1",
    )(ctx, x, g.reshape(1, d), mod)


def _proj_kernel(h_ref, w_ref, o_ref, w_scr, *, tn, n_valid):
    @pl.when(pl.program_id(1) == 0)
    def _():
        w = w_ref[...]
        col = pl.program_id(0) * tn + lax.broadcasted_iota(jnp.int32, w.shape, 1)
        w_scr[...] = jnp.where(col < n_valid, w, 0).astype(BF16)

    o_ref[...] = jnp.dot(h_ref[...], w_scr[...], preferred_element_type=F32)


def _proj_call(h, w, f_pad):
    t, d = h.shape
    tm, tn = ROW_TILE, 768
    return pl.pallas_call(
        functools.partial(_proj_kernel, tn=tn, n_valid=w.shape[1]),
        grid=(f_pad // tn, t // tm),
        in_specs=[pl.BlockSpec((tm, d), lambda j, i: (i, 0)),
                  pl.BlockSpec((d, tn), lambda j, i: (0, j))],
        out_specs=pl.BlockSpec((tm, tn), lambda j, i: (i, j)),
        out_shape=jax.ShapeDtypeStruct((t, f_pad), F32),
        scratch_shapes=[pltpu.VMEM((d, tn), BF16)],
        compiler_params=_cparams(2),
        name="in_proj",
    )(h, w)


def _outproj_kernel(ya_ref, yb_ref, wa_ref, wb_ref, x_ref, g_ref, mod_ref, xo_ref, h_ref, *, tm, n_ctx):
    i = pl.program_id(0)
    rows = i * tm + lax.broadcasted_iota(jnp.int32, (tm, 1), 0)
    mod = mod_ref[0]
    acc = jnp.dot(ya_ref[...], wa_ref[...], preferred_element_type=F32)
    acc = acc + jnp.dot(yb_ref[...], wb_ref[...], preferred_element_type=F32)
    xn = x_ref[...] + _mod_rows(mod, 2, rows, n_ctx) * acc
    xo_ref[...] = xn
    h = _norm_mod(xn, g_ref[...], _mod_rows(mod, 3, rows, n_ctx), _mod_rows(mod, 4, rows, n_ctx))
    h_ref[...] = h.astype(BF16)


def _outproj_call(ya, yb, wa, wb, x, g2, mod, layer, n_ctx):
    t, d = x.shape
    half = ya.shape[1]
    tm = 384
    return pl.pallas_call(
        functools.partial(_outproj_kernel, tm=tm, n_ctx=n_ctx),
        grid=(t // tm,),
        in_specs=[pl.BlockSpec((tm, half), lambda i: (i, 0)),
                  pl.BlockSpec((tm, half), lambda i: (i, 0)),
                  pl.BlockSpec((half, d), lambda i: (0, 0)),
                  pl.BlockSpec((half, d), lambda i: (0, 0)),
                  pl.BlockSpec((tm, d), lambda i: (i, 0)),
                  pl.BlockSpec((1, d), lambda i: (0, 0)),
                  pl.BlockSpec((1, SUBLANES, 6 * d), lambda i: (layer, 0, 0))],
        out_specs=[pl.BlockSpec((tm, d), lambda i: (i, 0)),
                   pl.BlockSpec((tm, d), lambda i: (i, 0))],
        out_shape=[jax.ShapeDtypeStruct((t, d), F32), jax.ShapeDtypeStruct((t, d), BF16)],
        compiler_params=_cparams(1),
        name="mix_out_proj_norm2",
    )(ya, yb, wa, wb, x, g2.reshape(1, d), mod)


def _ffn_in_kernel(h_ref, wg_ref, wu0_ref, wu1_ref, wu2_ref, wu3_ref, wo_ref, g_ref, u_ref, wob_ref, w_scr, *, tn):
    @pl.when(pl.program_id(1) == 0)
    def _():
        j = pl.program_id(0)
        wu = jnp.concatenate([wu0_ref[0], wu1_ref[0], wu2_ref[0], wu3_ref[0]], axis=1)
        col = j * tn + lax.broadcasted_iota(jnp.int32, wu.shape, 1)
        w_scr[0] = jnp.where(col < D_FF, wg_ref[0], 0.0).astype(BF16)
        w_scr[1] = jnp.where(col < D_FF, wu, 0.0).astype(BF16)
        wo = wo_ref[0]
        row = j * tn + lax.broadcasted_iota(jnp.int32, wo.shape, 0)
        wob_ref[...] = jnp.where(row < D_FF, wo, 0.0).astype(BF16)

    h = h_ref[...]
    g_ref[...] = jnp.dot(h, w_scr[0], preferred_element_type=F32).astype(BF16)
    u_ref[...] = jnp.dot(h, w_scr[1], preferred_element_type=F32).astype(BF16)


def _ffn_in_call(h, ffn_w_in, ffn_w_out, layer):
    t, d = h.shape
    tm, tn = ROW_TILE, 512
    q = tn // LANES
    assert D_FF % LANES == 0 and q == 4
    up0 = D_FF // LANES
    last = 2 * D_FF // LANES - 1
    spec_o = pl.BlockSpec((tm, tn), lambda j, i: (i, j))
    up_specs = [pl.BlockSpec((1, d, LANES), lambda j, i, r=r: (layer, 0, jnp.minimum(up0 + q * j + r, last)))
                for r in range(q)]
    return pl.pallas_call(
        functools.partial(_ffn_in_kernel, tn=tn),
        grid=(FF_PAD // tn, t // tm),
        in_specs=[pl.BlockSpec((tm, d), lambda j, i: (i, 0)),
                  pl.BlockSpec((1, d, tn), lambda j, i: (layer, 0, j))] + up_specs
                 + [pl.BlockSpec((1, tn, d), lambda j, i: (layer, j, 0))],
        out_specs=[spec_o, spec_o, pl.BlockSpec((tn, d), lambda j, i: (j, 0))],
        out_shape=[jax.ShapeDtypeStruct((t, FF_PAD), BF16)] * 2 + [jax.ShapeDtypeStruct((FF_PAD, d), BF16)],
        scratch_shapes=[pltpu.VMEM((2, d, tn), BF16)],
        compiler_params=_cparams(2),
        name="ffn_in_proj",
    )(h, ffn_w_in, ffn_w_in, ffn_w_in, ffn_w_in, ffn_w_in, ffn_w_out)


def _glu_act(conv, bias, up):
    gate = conv + bias
    return (0.5 * gate * (1.0 + lax.erf(gate * (2.0 ** -0.5))) * up.astype(F32)).astype(BF16)


def _ffn_out_kernel(*refs, tm, tk, n_ctx, t_total, final_norm):
    if final_norm:
        (gm_ref, gp_ref, gn_ref, u_ref, dw_ref, db_ref, wo_ref, x_ref, mod_ref, fg_ref,
         o_ref, acc_scr, act_a, act_b) = refs
    else:
        (gm_ref, gp_ref, gn_ref, u_ref, dw_ref, db_ref, wo_ref, x_ref, mod_ref, fg_ref, nmod_ref,
         o_ref, hn_ref, acc_scr, act_a, act_b) = refs
    i = pl.program_id(0)
    k = pl.program_id(1)
    nk = pl.num_programs(1) - 1
    w = GRID_W
    nrow = tm // w
    blk0 = i * nrow
    nb_ctx, nb_tot = n_ctx // w, t_total // w
    sub = lax.broadcasted_iota(jnp.int32, (SUBLANES, LANES), 0)

    def neighbours(g, n):
        gl = pltpu.roll(g, 1, axis=0)
        gr = pltpu.roll(g, n - 1, axis=0)
        gl = jnp.concatenate([jnp.where(sub == 0, 0.0, gl[:SUBLANES]), gl[SUBLANES:]], axis=0)
        gr = jnp.concatenate([gr[:n - SUBLANES], jnp.where(sub == SUBLANES - 1, 0.0, gr[n - SUBLANES:])], axis=0)
        return gl, gr

    def step(dst, src):
        d_out = acc_scr.shape[1]
        row_halves = 2
        hm = tm // row_halves
        n_mm = row_halves * d_out // MXU_N
        n_pieces = (tk // LANES) * (nrow + 2)

        def matmul_chunk(n):
            cs = slice((n // row_halves) * MXU_N, (n // row_halves + 1) * MXU_N)
            rs = slice((n % row_halves) * hm, (n % row_halves + 1) * hm)
            acc_scr[rs, cs] += jnp.dot(src[rs, :], wo_ref[:, cs], preferred_element_type=F32)

        mm_at = {(n * n_pieces) // n_mm: n for n in range(n_mm)}
        assert len(mm_at) == n_mm
        piece = 0
        for lb in range(tk // LANES):
            ls = slice(lb * LANES, (lb + 1) * LANES)
            dwv = dw_ref[:, ls]
            bias = db_ref[:, ls]
            part = [None] * nrow
            for r in range(-1, nrow + 1):
                if piece in mm_at:
                    matmul_chunk(mm_at[piece])
                piece += 1
                if r == -1:
                    g = gp_ref[:, ls]
                elif r == nrow:
                    g = gn_ref[:, ls]
                else:
                    g = gm_ref[r * w:(r + 1) * w, ls]
                g = g.astype(F32)
                gl, gr = neighbours(g, w)
                for kh, ro in ((0, r + 1), (1, r), (2, r - 1)):
                    if not 0 <= ro < nrow:
                        continue
                    wv = dwv[3 * kh:3 * kh + 3]
                    if kh == 0:
                        wv = wv * jnp.where(blk0 + ro >= nb_ctx + 1, 1.0, 0.0)
                    if kh == 2:
                        ok = jnp.logical_and(blk0 + ro >= nb_ctx, blk0 + ro < nb_tot - 1)
                        wv = wv * jnp.where(ok, 1.0, 0.0)
                    c = gl * wv[0:1] + g * wv[1:2] + gr * wv[2:3]
                    part[ro] = c if part[ro] is None else part[ro] + c
                ro = r - 1
                if 0 <= ro < nrow:
                    rs = slice(ro * w, (ro + 1) * w)
                    dst[rs, ls] = _glu_act(part[ro], bias, u_ref[rs, ls])
                    part[ro] = None

        @pl.when(i == 0)
        def _():
            for lb in range(tk // LANES):
                ls = slice(lb * LANES, (lb + 1) * LANES)
                g = gm_ref[0:n_ctx, ls].astype(F32)
                gl, gr = neighbours(g, n_ctx)
                conv = gl * dw_ref[3:4, ls] + g * dw_ref[4:5, ls] + gr * dw_ref[5:6, ls]
                dst[0:n_ctx, ls] = _glu_act(conv, db_ref[:, ls], u_ref[0:n_ctx, ls])

    @pl.when(k == 0)
    def _():
        acc_scr[...] = jnp.zeros_like(acc_scr)
        act_b[...] = jnp.zeros_like(act_b)

    @pl.when(k % 2 == 0)
    def _():
        step(act_a, act_b)

    @pl.when(k % 2 == 1)
    def _():
        step(act_b, act_a)

    @pl.when(k == nk)
    def _():
        rb = 2 * LANES
        for r0 in range(0, tm, rb):
            rs = slice(r0, r0 + rb)
            rws = i * tm + r0 + lax.broadcasted_iota(jnp.int32, (rb, 1), 0)
            xn = x_ref[rs, :] + _mod_rows(mod_ref[0], 5, rws, n_ctx) * acc_scr[rs, :]
            if final_norm:
                xn = xn * lax.rsqrt(jnp.mean(xn * xn, axis=-1, keepdims=True) + EPS) * fg_ref[...]
            else:
                nmod = nmod_ref[0]
                hn = _norm_mod(xn, fg_ref[...], _mod_rows(nmod, 0, rws, n_ctx), _mod_rows(nmod, 1, rws, n_ctx))
                hn_ref[rs, :] = hn.astype(BF16)
            o_ref[rs, :] = xn


def _ffn_out_call(gate, up, dw9, dwb, wo, x, mod, layer, norm_g, n_ctx, final_norm):
    t, d = x.shape
    fp = gate.shape[1]
    tm, tk, w = ROW_TILE, 512, GRID_W
    assert n_ctx <= tm and n_ctx % w == 0 and tm % (2 * LANES) == 0
    rpt = tm // w
    n_rows = t // w
    spec_x = pl.BlockSpec((tm, d), lambda i, k: (i, 0))
    extra_in, extra_args = [], []
    out_specs, out_shape = spec_x, jax.ShapeDtypeStruct((t, d), F32)
    if not final_norm:
        extra_in = [pl.BlockSpec((1, SUBLANES, 6 * d), lambda i, k: (layer + 1, 0, 0))]
        extra_args = [mod]
        out_specs = [spec_x, spec_x]
        out_shape = [out_shape, jax.ShapeDtypeStruct((t, d), BF16)]
    nk = fp // tk
    assert nk % 2 == 1
    kc = lambda k: jnp.minimum(k, nk - 1)
    return pl.pallas_call(
        functools.partial(_ffn_out_kernel, tm=tm, tk=tk, n_ctx=n_ctx, t_total=t, final_norm=final_norm),
        grid=(t // tm, nk + 1),
        in_specs=[pl.BlockSpec((tm, tk), lambda i, k: (i, kc(k))),
                  pl.BlockSpec((w, tk), lambda i, k: (jnp.maximum(i * rpt - 1, 0), kc(k))),
                  pl.BlockSpec((w, tk), lambda i, k: (jnp.minimum((i + 1) * rpt, n_rows - 1), kc(k))),
                  pl.BlockSpec((tm, tk), lambda i, k: (i, kc(k))),
                  pl.BlockSpec((9, tk), lambda i, k: (0, kc(k))),
                  pl.BlockSpec((1, tk), lambda i, k: (0, kc(k))),
                  pl.BlockSpec((tk, d), lambda i, k: (jnp.maximum(k - 1, 0), 0)),
                  spec_x,
                  pl.BlockSpec((1, SUBLANES, 6 * d), lambda i, k: (layer, 0, 0)),
                  pl.BlockSpec((1, d), lambda i, k: (0, 0))] + extra_in,
        out_specs=out_specs,
        out_shape=out_shape,
        scratch_shapes=[pltpu.VMEM((tm, d), F32), pltpu.VMEM((tm, tk), BF16), pltpu.VMEM((tm, tk), BF16)],
        compiler_params=_cparams(2),
        name="ffn_conv_glu_out",
    )(gate, gate, gate, up, dw9, dwb, wo, x, mod, norm_g.reshape(1, d), *extra_args)


def _retention_kernel(*refs, rev, final, c):
    if final:
        q_ref, k_ref, v_ref, lg_ref, gate_ref, ob_ref, o_ref, r_scr, di_scr, dq_scr, dk_scr, dc_scr = refs
    else:
        q_ref, k_ref, v_ref, lg_ref, o_ref, r_scr, di_scr, dq_scr, dk_scr, dc_scr = refs
    d = 1 if rev else 0
    dk = RET_DK

    @pl.when(pl.program_id(0) == 0)
    def _():
        r_scr[...] = jnp.zeros_like(r_scr)
        i = lax.broadcasted_iota(jnp.int32, (c, c), 0)
        j = lax.broadcasted_iota(jnp.int32, (c, c), 1)
        diff = ((j - i) if rev else (i - j)).astype(F32)
        row = lax.broadcasted_iota(jnp.int32, (c, dk), 0)
        pos = ((c - 1 - row) if rev else row).astype(F32)
        for h in range(RET_HEADS):
            lg = _log_sigmoid(lg_ref[d:d + 1, h:h + 1])
            di_scr[h] = jnp.where(diff >= 0, jnp.exp(lg * jnp.maximum(diff, 0.0)), 0.0)
            dq_scr[h] = jnp.exp(lg * (pos + 1.0))
            dk_scr[h] = jnp.exp(lg * (c - 1.0 - pos)) * (dk ** -0.5)
            dc_scr[h] = jnp.exp(jnp.broadcast_to(lg, (1, dk)) * c)

    for h in range(RET_HEADS):
        sl = slice(h * dk, (h + 1) * dk)
        qh = q_ref[:, sl]
        kh = k_ref[:, sl]
        vh = v_ref[:, sl].astype(BF16)
        r_state = r_scr[h]
        scores = _bdot_nt(qh, kh) * (di_scr[h] * (dk ** -0.5))
        out = _bdot(scores, vh) + _bdot(qh * dq_scr[h], r_state)
        r_scr[h] = dc_scr[h] * r_state + _bdot_tn(kh * dk_scr[h], vh)
        if final:
            tot = out + ob_ref[:, sl]
            y = tot * lax.rsqrt(jnp.mean(tot * tot, axis=-1, keepdims=True) + EPS) * _silu(gate_ref[:, sl])
            o_ref[:, sl] = y.astype(o_ref.dtype)
        else:
            o_ref[:, sl] = out


def _retention_call(p, logit, other, rev, n_ctx):
    t = p.shape[0]
    c = RET_CHUNK
    n, nc = t // c, n_ctx // c
    width = RET_HEADS * RET_DK
    final = other is not None
    cm = lambda s: _chunk_index(s, rev, nc, n)
    in_specs = [pl.BlockSpec((c, width), lambda s: (cm(s), 0)),
                pl.BlockSpec((c, width), lambda s: (cm(s), 1)),
                pl.BlockSpec((c, width), lambda s: (cm(s), 2)),
                pl.BlockSpec((2, RET_HEADS), lambda s: (0, 0))]
    args = [p, p, p, logit]
    if final:
        in_specs += [pl.BlockSpec((c, width), lambda s: (cm(s), 3)), pl.BlockSpec((c, width), lambda s: (cm(s), 0))]
        args += [p, other]
    return pl.pallas_call(
        functools.partial(_retention_kernel, rev=rev, final=final, c=c),
        grid=(n,),
        in_specs=in_specs,
        out_specs=pl.BlockSpec((c, width), lambda s: (cm(s), 0)),
        out_shape=jax.ShapeDtypeStruct((t, width), BF16 if final else F32),
        scratch_shapes=[pltpu.VMEM((RET_HEADS, RET_DK, RET_DK), F32),
                        pltpu.VMEM((RET_HEADS, c, c), F32),
                        pltpu.VMEM((RET_HEADS, c, RET_DK), F32),
                        pltpu.VMEM((RET_HEADS, c, RET_DK), F32),
                        pltpu.VMEM((RET_HEADS, 1, RET_DK), F32)],
        compiler_params=_cparams(1),
        name="retention_bwd" if rev else "retention_fwd_merge",
    )(*args)


def _rwkv_kernel(*refs, rev, final, c, nc):
    if final:
        (r_ref, k_ref, v_ref, wd_ref, ad_ref, mu_ref, mus_ref, vec_ref, wup_ref, aup_ref,
         gl_ref, gup_ref, yb_ref, bb_ref, o_ref, st_scr, carry_scr) = refs
    else:
        (r_ref, k_ref, v_ref, wd_ref, ad_ref, mu_ref, mus_ref, vec_ref, wup_ref, aup_ref,
         y_ref, bonus_ref, st_scr, carry_scr) = refs
    d = 1 if rev else 0
    s = pl.program_id(0)
    n = RWKV_N
    wdt = RWKV_W

    @pl.when(s == 0)
    def _():
        st_scr[...] = jnp.zeros_like(st_scr)

    @pl.when(jnp.logical_or(s == 0, s == nc))
    def _():
        carry_scr[...] = jnp.zeros_like(carry_scr)

    row = lax.broadcasted_iota(jnp.int32, (c, 1), 0)
    edge = (row == c - 1) if rev else (row == 0)
    keep = c - 1 if not rev else 0

    def shifted(x, lo):
        width = x.shape[1]
        prev = pltpu.roll(x, (c - 1) if rev else 1, axis=0)
        prev = jnp.where(edge, carry_scr[0:1, lo:lo + width], prev)
        carry_scr[0:1, lo:lo + width] = x[keep:keep + 1, :]
        return prev

    def mix(x, lo, mu):
        prev = shifted(x, lo)
        return x + (prev - x) * mu

    r = mix(r_ref[...], 0, mu_ref[0:1, 0:wdt])
    k = mix(k_ref[...], wdt, mu_ref[0:1, wdt:2 * wdt])
    v = mix(v_ref[...], 2 * wdt, mu_ref[0:1, 2 * wdt:3 * wdt])
    wd = mix(wd_ref[...], 3 * wdt, mus_ref[0:1, :])[:, d * RWKV_LORA:(d + 1) * RWKV_LORA]
    ad = mix(ad_ref[...], 3 * wdt + LANES, mus_ref[1:2, :])[:, d * RWKV_LORA:(d + 1) * RWKV_LORA]

    w0, a0 = vec_ref[0:1, :], vec_ref[1:2, :]
    k_k, k_a, r_k = vec_ref[2:3, :], vec_ref[3:4, :], vec_ref[4:5, :]
    w_log = -_softplus(-(w0 + _bdot(jnp.tanh(wd), wup_ref[...]))) - 0.5
    lw = -jnp.exp(w_log)
    a = _sigmoid(a0 + _bdot(ad, aup_ref[...]))
    kk = k * k_k
    kk = kk * lax.rsqrt(_seg_sum(kk * kk, n) + L2_EPS)
    k = k * (1.0 + (a - 1.0) * k_a)
    bonus = _seg_sum(r * k * r_k, n) * v
    b = kk * a

    incl, strict = _order_masks(c, rev)
    cum = _dot_mask_lhs(jnp.where(incl, 1.0, 0.0).astype(BF16), lw)
    cum_end = cum[keep:keep + 1, :]
    e_pos = jnp.exp(cum)
    e_neg = jnp.exp(-cum)
    r_t = r * e_pos
    a_t = -kk * jnp.exp(cum - lw)
    k_t = k * e_neg
    b_t = b * e_neg
    w_end = jnp.exp(cum_end)
    kw = k_t * w_end
    bw = b_t * w_end
    assert c == n
    pw = 2 * n
    incl2, strict2, eye2, lo_c = _pair_masks(c, rev)
    lo_2c = lax.broadcasted_iota(jnp.int32, (2 * c, pw), 1) < n
    bdiag = _bdiag

    adiag = functools.partial(_bdiag, anti=True)

    def pdot_inv(a2, b2):
        return _dot_inv(a2, bdiag(b2))

    ys = []
    n_pairs = RWKV_HEADS // 2
    for p0 in range(0, n_pairs, RWKV_GROUP):
        prs = range(p0, p0 + RWKV_GROUP)
        pls = [slice(p * pw, (p + 1) * pw) for p in prs]
        ar = [jnp.concatenate([a_t[:, ps], r_t[:, ps]], axis=0) for ps in pls]
        s_e = [_bdot_nt(jnp.where(lo_2c, x, 0.0), jnp.concatenate([b_t[:, ps], k_t[:, ps]], axis=0))
               for x, ps in zip(ar, pls)]
        s_o = [_bdot_nt(jnp.where(lo_2c, 0.0, x), jnp.concatenate([k_t[:, ps], b_t[:, ps]], axis=0))
               for x, ps in zip(ar, pls)]
        a_ab = [jnp.where(strict2, jnp.where(lo_c, e[:c], o[:c]), 0.0) for e, o in zip(s_e, s_o)]
        a_rb = [jnp.where(incl2, jnp.where(lo_c, e[c:], o[c:]), 0.0) for e, o in zip(s_e, s_o)]
        a_kk = [jnp.concatenate([jnp.where(strict2, jnp.where(lo_c, o[:c], e[:c]), 0.0),
                                 jnp.where(incl2, jnp.where(lo_c, o[c:], e[c:]), 0.0)], axis=0)
                for e, o in zip(s_e, s_o)]
        tms = _neumann_inverse_packed(a_ab, eye2, c, pdot_inv)
        av = [_bdot(a, adiag(v[:, ps])) for a, ps in zip(a_kk, pls)]
        pp = [_dot_inv(t, jnp.concatenate([bdiag(a_t[:, ps]), bdiag(x[:c])], axis=1))
              for t, ps, x in zip(tms, pls, av)]
        rq_y0 = [_bdot(a, jnp.concatenate([bdiag(x[:, :pw]), bdiag(x[:, pw:])], axis=1)) for a, x in zip(a_rb, pp)]
        m_g = [_bdot_tn(bw[:, ps], x) for ps, x in zip(pls, pp)]
        kv = [_bdot_tn(kw[:, ps], v[:, ps]) for ps in pls]
        so = []
        for i, p in enumerate(prs):
            rq = r_t[:, pls[i]] + rq_y0[i][:, :pw]
            m = (jnp.where(lo_c, m_g[i][:n, :pw], m_g[i][n:, :pw])
                 + jnp.where(eye2, jnp.broadcast_to(w_end[:, pls[i]], (n, pw)), 0.0))
            so.append(_dot_state(jnp.concatenate([rq, m], axis=0), bdiag(st_scr[p])))
        for i, p in enumerate(prs):
            g = jnp.where(lo_c, m_g[i][:n, pw:], m_g[i][n:, pw:]) + jnp.where(lo_c, kv[i][:n], kv[i][n:])
            st_scr[p] = so[i][c:] + g
            ys.append(so[i][:c] + (rq_y0[i][:, pw:] + av[i][c:]))
    y = jnp.concatenate(ys, axis=1)

    if final:
        ysum = y + yb_ref[...]
        mu_h = _seg_sum(ysum, n) * (1.0 / n)
        yc = ysum - mu_h
        var = _seg_sum(yc * yc, n) * (1.0 / n)
        ln_g, ln_b = vec_ref[5:6, :], vec_ref[6:7, :]
        yn = yc * lax.rsqrt(var + GN_EPS) * ln_g + ln_b + bonus + bb_ref[...]
        gate = _bdot(_sigmoid(gl_ref[...]), gup_ref[...])
        o_ref[...] = (yn * gate).astype(o_ref.dtype)
    else:
        y_ref[...] = y
        bonus_ref[...] = bonus


def _rwkv_call(p, prm, other, rev, n_ctx):
    t = p.shape[0]
    c = RWKV_CHUNK
    n, nc = t // c, n_ctx // c
    wdt = RWKV_W
    d = 1 if rev else 0
    final = other is not None
    cm = lambda s: _chunk_index(s, rev, nc, n)
    base = 4096 // wdt
    full = lambda shape: pl.BlockSpec(shape, lambda s: (0,) * len(shape))
    in_specs = [pl.BlockSpec((c, wdt), lambda s: (cm(s), base)),
                pl.BlockSpec((c, wdt), lambda s: (cm(s), base + 1)),
                pl.BlockSpec((c, wdt), lambda s: (cm(s), base + 2)),
                pl.BlockSpec((c, LANES), lambda s: (cm(s), 7168 // LANES)),
                pl.BlockSpec((c, LANES), lambda s: (cm(s), 7168 // LANES + 1)),
                full((1, 3 * wdt)), full((2, LANES)), full((SUBLANES, wdt)),
                full((RWKV_LORA, wdt)), full((RWKV_LORA, wdt))]
    args = [p, p, p, p, p, prm["mu"][d], prm["mu_small"][d], prm["vecs"][d], prm["w_up"][d], prm["a_up"][d]]
    if final:
        in_specs += [pl.BlockSpec((c, 2 * LANES), lambda s: (cm(s), 7424 // (2 * LANES))),
                     full((2 * LANES, wdt)),
                     pl.BlockSpec((c, wdt), lambda s: (cm(s), 0)),
                     pl.BlockSpec((c, wdt), lambda s: (cm(s), 0))]
        args += [p, prm["g_up"], other[0], other[1]]
        out_specs = pl.BlockSpec((c, wdt), lambda s: (cm(s), 0))
        out_shape = jax.ShapeDtypeStruct((t, wdt), BF16)
    else:
        out_specs = [pl.BlockSpec((c, wdt), lambda s: (cm(s), 0))] * 2
        out_shape = [jax.ShapeDtypeStruct((t, wdt), F32)] * 2
    return pl.pallas_call(
        functools.partial(_rwkv_kernel, rev=rev, final=final, c=c, nc=nc),
        grid=(n,),
        in_specs=in_specs,
        out_specs=out_specs,
        out_shape=out_shape,
        scratch_shapes=[pltpu.VMEM((RWKV_HEADS // 2, RWKV_N, 2 * RWKV_N), F32),
                        pltpu.VMEM((SUBLANES, 3 * wdt + 2 * LANES), F32)],
        compiler_params=_cparams(1),
        name="rwkv7_bwd" if rev else "rwkv7_fwd_merge",
    )(*args)


def _mlstm_kernel(*refs, rev, final, c):
    if final:
        q_ref, k_ref, v_ref, sm_ref, bias_ref, og_ref, ng_ref, hb_ref, o_ref, ct_scr, m_scr = refs
    else:
        q_ref, k_ref, v_ref, sm_ref, bias_ref, o_ref, ct_scr, m_scr = refs
    d = 1 if rev else 0
    dk, dv = MLSTM_DK, MLSTM_DV
    nh = MLSTM_HEADS

    @pl.when(pl.program_id(0) == 0)
    def _():
        ct_scr[...] = jnp.zeros_like(ct_scr)
        m_scr[...] = jnp.zeros_like(m_scr)

    incl, _ = _order_masks(c, rev)
    keep = 0 if rev else c - 1
    sm = sm_ref[...]
    li = sm + bias_ref[0:1, :]
    lf = _log_sigmoid(sm + bias_ref[1:2, :])
    bcum = _dot_mask_lhs(jnp.where(incl, 1.0, 0.0).astype(BF16), lf)
    bcum_t = bcum.T
    li_t = li.T
    ones_col = jnp.where(lax.broadcasted_iota(jnp.int32, (c, LANES), 1) == 0, 1.0, 0.0).astype(BF16)

    heads = range(nh)
    cis = [d * 2 * nh + h for h in heads]
    b_col = [bcum[:, ci + nh:ci + nh + 1] for ci in cis]
    i_col = [li[:, ci:ci + 1] for ci in cis]
    m_prev = [m_scr[h:h + 1, 0:1] for h in heads]
    qs = [q_ref[:, h * dk:(h + 1) * dk] * (dk ** -0.5) for h in heads]
    ks = [k_ref[:, h * dk:(h + 1) * dk] for h in heads]
    v_aug = [jnp.concatenate([v_ref[:, h * dv:(h + 1) * dv].astype(BF16), ones_col], axis=1) for h in heads]
    qk_raw = [_bdot_nt(q, k) for q, k in zip(qs, ks)]
    q_ct = [_bdot(q, ct_scr[h]) for q, h in zip(qs, heads)]
    d_log = [jnp.where(incl, bc - bcum_t[ci + nh:ci + nh + 1, :] + li_t[ci:ci + 1, :], -jnp.inf)
             for bc, ci in zip(b_col, cis)]
    inter = [bc + mp for bc, mp in zip(b_col, m_prev)]
    m_t = [jnp.maximum(jnp.max(dl, axis=-1, keepdims=True), it) for dl, it in zip(d_log, inter)]
    qk = [x * jnp.exp(dl - mt) for x, dl, mt in zip(qk_raw, d_log, m_t)]
    num_aug = [_bdot(x, va) + jnp.exp(it - mt) * qc for x, va, it, mt, qc in zip(qk, v_aug, inter, m_t, q_ct)]
    b_end = [bc[keep:keep + 1, :] for bc in b_col]
    w_log = [be - bc + ic for be, bc, ic in zip(b_end, b_col, i_col)]
    m_new = [jnp.maximum(be + mp, jnp.max(wl, axis=0, keepdims=True)) for be, mp, wl in zip(b_end, m_prev, w_log)]
    kv = [_bdot_tn(k * jnp.exp(wl - mn), va) for k, wl, mn, va in zip(ks, w_log, m_new, v_aug)]
    for h in heads:
        ct_scr[h] = jnp.exp(b_end[h] + m_prev[h] - m_new[h]) * ct_scr[h] + kv[h]
        m_scr[h:h + 1, :] = jnp.broadcast_to(m_new[h], (1, LANES))
        den = num_aug[h][:, dv:dv + 1]
        hout = num_aug[h][:, :dv] / jnp.maximum(jnp.abs(den), jnp.exp(-m_t[h]))
        sl = slice(h * dv, (h + 1) * dv)
        if final:
            tot = hout + hb_ref[:, sl]
            y = tot * lax.rsqrt(jnp.mean(tot * tot, axis=-1, keepdims=True) + EPS)
            o_ref[:, sl] = (y * ng_ref[0:1, sl] * _sigmoid(og_ref[:, sl])).astype(o_ref.dtype)
        else:
            o_ref[:, sl] = hout


def _mlstm_call(p, bias_rows, norm_g, other, rev, n_ctx):
    t = p.shape[0]
    c = MLSTM_CHUNK
    n, nc = t // c, n_ctx // c
    dk, dv, nh = MLSTM_DK, MLSTM_DV, MLSTM_HEADS
    final = other is not None
    cm = lambda s: _chunk_index(s, rev, nc, n)
    full = lambda shape: pl.BlockSpec(shape, lambda s: (0,) * len(shape))
    in_specs = [pl.BlockSpec((c, nh * dk), lambda s: (cm(s), 0)),
                pl.BlockSpec((c, nh * dk), lambda s: (cm(s), 1)),
                pl.BlockSpec((c, nh * dv), lambda s: (cm(s), 1)),
                pl.BlockSpec((c, LANES), lambda s: (cm(s), 7168 // LANES)),
                full((2, LANES))]
    args = [p, p, p, p, bias_rows]
    if final:
        in_specs += [pl.BlockSpec((c, nh * dv), lambda s: (cm(s), 2)), full((1, nh * dv)),
                     pl.BlockSpec((c, nh * dv), lambda s: (cm(s), 0))]
        args += [p, norm_g.reshape(1, nh * dv), other]
    return pl.pallas_call(
        functools.partial(_mlstm_kernel, rev=rev, final=final, c=c),
        grid=(n,),
        in_specs=in_specs,
        out_specs=pl.BlockSpec((c, nh * dv), lambda s: (cm(s), 0)),
        out_shape=jax.ShapeDtypeStruct((t, nh * dv), BF16 if final else F32),
        scratch_shapes=[pltpu.VMEM((nh, dk, dv + LANES), F32), pltpu.VMEM((SUBLANES, LANES), F32)],
        compiler_params=_cparams(1),
        name="mlstm_bwd" if rev else "mlstm_fwd_merge",
    )(*args)


def _gdn_prep_kernel(xm_ref, xp_ref, xn_ref, cw_ref, sm_ref, gp_ref, q_ref, k_ref, v_ref, gb_ref, *, tm, n_ctx, t_total):
    i = pl.program_id(0)
    x = xm_ref[...]
    rows = i * tm + lax.broadcasted_iota(jnp.int32, (tm, 1), 0)
    first = rows == i * tm
    last = rows == i * tm + tm - 1
    prev = jnp.where(first, xp_ref[SUBLANES - 1:SUBLANES, :], pltpu.roll(x, 1, axis=0))
    nxt = jnp.where(last, xn_ref[0:1, :], pltpu.roll(x, tm - 1, axis=0))
    prev = jnp.where(jnp.logical_or(rows == 0, rows == n_ctx), 0.0, prev)
    nxt = jnp.where(jnp.logical_or(rows == n_ctx - 1, rows == t_total - 1), 0.0, nxt)
    y = _silu(prev * cw_ref[0:1, :] + x * cw_ref[1:2, :] + nxt * cw_ref[2:3, :])
    w = GDN_HEADS * GDN_DK
    q, k, v = y[:, :w], y[:, w:2 * w], y[:, 2 * w:]
    q_ref[...] = (q * lax.rsqrt(_seg_sum(q * q, GDN_DK) + L2_EPS) * (GDN_DK ** -0.5)).astype(BF16)
    k_ref[...] = (k * lax.rsqrt(_seg_sum(k * k, GDN_DK) + L2_EPS)).astype(BF16)
    v_ref[...] = v.astype(BF16)
    sm = sm_ref[...]
    lane = lax.broadcasted_iota(jnp.int32, sm.shape, 1)
    log_alpha = -jnp.exp(gp_ref[0:1, :]) * _softplus(sm + gp_ref[1:2, :])
    gb_ref[...] = jnp.where(lane < 32, log_alpha, _sigmoid(sm))


def _gdn_prep_call(p, conv_w, gate_params, n_ctx):
    t = p.shape[0]
    tm = 256
    w = GDN_HEADS * GDN_DK
    qkv_blk = 3072 // GDN_QKV
    nb8 = t // SUBLANES
    r8 = tm // SUBLANES
    spec_o = pl.BlockSpec((tm, w), lambda i: (i, 0))
    return pl.pallas_call(
        functools.partial(_gdn_prep_kernel, tm=tm, n_ctx=n_ctx, t_total=t),
        grid=(t // tm,),
        in_specs=[pl.BlockSpec((tm, GDN_QKV), lambda i: (i, qkv_blk)),
                  pl.BlockSpec((SUBLANES, GDN_QKV), lambda i: (jnp.maximum(i * r8 - 1, 0), qkv_blk)),
                  pl.BlockSpec((SUBLANES, GDN_QKV), lambda i: (jnp.minimum((i + 1) * r8, nb8 - 1), qkv_blk)),
                  pl.BlockSpec((3, GDN_QKV), lambda i: (0, 0)),
                  pl.BlockSpec((tm, LANES), lambda i: (i, 7168 // LANES)),
                  pl.BlockSpec((2, LANES), lambda i: (0, 0))],
        out_specs=[spec_o, spec_o, spec_o, pl.BlockSpec((tm, LANES), lambda i: (i, 0))],
        out_shape=[jax.ShapeDtypeStruct((t, w), BF16)] * 3 + [jax.ShapeDtypeStruct((t, LANES), F32)],
        compiler_params=_cparams(1),
        name="gdn_conv_norm_gates",
    )(p, p, p, conv_w, p, gate_params)


def _gdn_kernel(*refs, rev, final, c):
    if final:
        q_ref, k_ref, v_ref, gb_ref, gate_ref, ng_ref, ob_ref, o_ref, st_scr = refs
    else:
        q_ref, k_ref, v_ref, gb_ref, o_ref, st_scr = refs
    d = 1 if rev else 0
    dk = GDN_DK
    nh = GDN_HEADS

    @pl.when(pl.program_id(0) == 0)
    def _():
        st_scr[...] = jnp.zeros_like(st_scr)

    incl, strict = _order_masks(c, rev)
    keep = 0 if rev else c - 1
    gb = gb_ref[...]
    gc = _dot_mask_lhs(jnp.where(incl, 1.0, 0.0).astype(BF16), gb)
    gc_t = gc.T
    incl2, strict2, eye2, lo_c = _pair_masks(c, rev)
    bdiag = _bdiag

    for h0 in range(0, nh, GDN_GROUP):
        hs = list(range(h0, h0 + GDN_GROUP))
        sls = [slice(h * dk, (h + 1) * dk) for h in hs]
        cols = [16 + d * nh + h for h in hs]
        g_col = [gc[:, cc:cc + 1] for cc in cols]
        beta = [gb[:, cc + 16:cc + 17] for cc in cols]
        g_end = [g[keep:keep + 1, :] for g in g_col]
        e_g = [jnp.exp(g) for g in g_col]
        qs = [q_ref[:, sl].astype(F32) for sl in sls]
        ks = [k_ref[:, sl].astype(F32) for sl in sls]
        vs = [v_ref[:, sl].astype(F32) for sl in sls]
        uw, o_part = [], []
        kq_e, kq_o, decay2 = [], [], []
        for a in range(0, len(hs), 2):
            kk2 = jnp.concatenate([ks[a], ks[a + 1]], axis=0)
            kq_e.append(_bdot_nt(jnp.concatenate([ks[a] * beta[a], qs[a]], axis=0), kk2))
            kq_o.append(_bdot_nt(jnp.concatenate([ks[a + 1] * beta[a + 1], qs[a + 1]], axis=0), kk2))
            g_row2 = jnp.concatenate([gc_t[cols[a]:cols[a] + 1, :], gc_t[cols[a + 1]:cols[a + 1] + 1, :]], axis=1)
            decay2.append(jnp.exp(jnp.where(incl2, jnp.where(lo_c, g_col[a], g_col[a + 1]) - g_row2, -jnp.inf)))
        l2 = [jnp.where(strict2, -(jnp.where(lo_c, e[:c], o[:c]) * dc), 0.0) for e, o, dc in zip(kq_e, kq_o, decay2)]
        a_qk2 = [jnp.where(lo_c, e[c:], o[c:]) * dc for e, o, dc in zip(kq_e, kq_o, decay2)]
        tinv2 = _neumann_inverse_packed(l2, eye2, c, lambda x2, y2: _dot_inv(x2, bdiag(y2)))
        zeros = jnp.zeros((c, 2 * dk), F32)
        rhs2 = []
        for a in range(0, len(hs), 2):
            r_e = jnp.concatenate([vs[a] * beta[a], ks[a] * (beta[a] * e_g[a]), zeros], axis=1)
            r_o = jnp.concatenate([zeros, vs[a + 1] * beta[a + 1], ks[a + 1] * (beta[a + 1] * e_g[a + 1])], axis=1)
            rhs2.append(jnp.concatenate([r_e, r_o], axis=0))
        uw2 = [_dot_inv(t, r) for t, r in zip(tinv2, rhs2)]
        op2 = []
        for aq, u in zip(a_qk2, uw2):
            u_bd = jnp.concatenate([jnp.concatenate([u[:, :2 * dk], zeros], axis=1),
                                    jnp.concatenate([zeros, u[:, 2 * dk:]], axis=1)], axis=0)
            op2.append(_bdot(aq, u_bd))
        for u, o2 in zip(uw2, op2):
            uw += [u[:, :2 * dk], u[:, 2 * dk:]]
            o_part += [o2[:, :2 * dk], o2[:, 2 * dk:]]
        s_part = [_bdot_tn(kh * jnp.exp(ge - g), u) for kh, ge, g, u in zip(ks, g_end, g_col, uw)]
        so = [_dot_state(jnp.concatenate([qh * e - op[:, dk:], sp[:, dk:]], axis=0), st_scr[h])
              for qh, e, op, sp, h in zip(qs, e_g, o_part, s_part, hs)]
        for i, h in enumerate(hs):
            sl = sls[i]
            out = so[i][:c] + o_part[i][:, :dk]
            st_scr[h] = jnp.exp(g_end[i]) * st_scr[h] - so[i][c:] + s_part[i][:, :dk]
            if final:
                tot = out + ob_ref[:, sl]
                y = tot * lax.rsqrt(jnp.mean(tot * tot, axis=-1, keepdims=True) + EPS)
                o_ref[:, sl] = (y * ng_ref[0:1, sl] * _silu(gate_ref[:, sl])).astype(o_ref.dtype)
            else:
                o_ref[:, sl] = out


def _gdn_call(p, q, k, v, gb, norm_g, other, rev, n_ctx):
    t = q.shape[0]
    c = GDN_CHUNK
    n, nc = t // c, n_ctx // c
    w = GDN_HEADS * GDN_DK
    final = other is not None
    cm = lambda s: _chunk_index(s, rev, nc, n)
    blk = pl.BlockSpec((c, w), lambda s: (cm(s), 0))
    in_specs = [blk, blk, blk, pl.BlockSpec((c, LANES), lambda s: (cm(s), 0))]
    args = [q, k, v, gb]
    if final:
        in_specs += [pl.BlockSpec((c, w), lambda s: (cm(s), 6144 // w)), pl.BlockSpec((1, w), lambda s: (0, 0)), blk]
        args += [p, norm_g.reshape(1, w), other]
    return pl.pallas_call(
        functools.partial(_gdn_kernel, rev=rev, final=final, c=c),
        grid=(n,),
        in_specs=in_specs,
        out_specs=blk,
        out_shape=jax.ShapeDtypeStruct((t, w), BF16 if final else F32),
        scratch_shapes=[pltpu.VMEM((GDN_HEADS, GDN_DK, GDN_DK), F32)],
        compiler_params=_cparams(1),
        name="gdn_bwd" if rev else "gdn_fwd_merge",
    )(*args)


def _pad_cols(w, width):
    return jnp.pad(w, ((0, 0), (0, width - w.shape[1])))


def _lane_row(pieces):
    row = jnp.zeros((LANES,), F32)
    for off, vec in pieces:
        row = row.at[off:off + vec.shape[0]].set(vec.astype(F32))
    return row


def _odd_weight(w):
    cols = [w[:, 0:3072], w[:, 3088:6160], w[:, 6160:7184], w[:, 3072:3088], w[:, 7184:7216]]
    return _pad_cols(jnp.concatenate(cols, axis=1), F_PAD).astype(BF16)


def _rwkv_params(e, rwkv_mu, rwkv_w0, rwkv_w_up, rwkv_a0, rwkv_a_up, rwkv_k_k, rwkv_k_a, rwkv_r_k,
                 rwkv_ln_g, rwkv_ln_b, rwkv_g_up):
    wdt = RWKV_W
    mu = rwkv_mu[e]
    zeros64 = jnp.zeros((RWKV_LORA,), F32)
    mu_small, vecs = [], []
    for d in range(2):
        m_wd = mu[d, 3 * wdt:3 * wdt + RWKV_LORA]
        m_ad = mu[d, 3 * wdt + RWKV_LORA:]
        lo = [m_wd, zeros64] if d == 0 else [zeros64, m_wd]
        la = [m_ad, zeros64] if d == 0 else [zeros64, m_ad]
        mu_small.append(jnp.stack([jnp.concatenate(lo), jnp.concatenate(la)]))
        vecs.append(jnp.stack([rwkv_w0[e, d], rwkv_a0[e, d], rwkv_k_k[e], rwkv_k_a[e], rwkv_r_k[e].reshape(wdt),
                               rwkv_ln_g[e], rwkv_ln_b[e], jnp.zeros((wdt,), F32)]))
    return {
        "mu": mu[:, None, :3 * wdt],
        "mu_small": jnp.stack(mu_small),
        "vecs": jnp.stack(vecs),
        "w_up": rwkv_w_up[e].astype(BF16),
        "a_up": rwkv_a_up[e].astype(BF16),
        "g_up": jnp.pad(rwkv_g_up[e], ((0, 2 * LANES - RWKV_GATE_LORA), (0, 0))).astype(BF16),
    }


def _even_mixer(p, e, n_ctx, ret_decay_logit, rwkv_prm):
    logit = ret_decay_logit[e].astype(F32)
    ret_b = _retention_call(p, logit, None, True, n_ctx)
    y_ret = _retention_call(p, logit, ret_b, False, n_ctx)
    rw_b = _rwkv_call(p, rwkv_prm, None, True, n_ctx)
    y_rwkv = _rwkv_call(p, rwkv_prm, rw_b, False, n_ctx)
    return y_ret, y_rwkv


def _odd_mixer(p, o, n_ctx, mlstm_gate_b, mlstm_norm_g, gdn_conv, gdn_a_log, gdn_dt_bias, gdn_norm_g):
    gate_b = mlstm_gate_b[o]
    nh = MLSTM_HEADS
    bias_i = _lane_row([(d * 2 * nh, gate_b[d, 0]) for d in range(2)])
    bias_f = _lane_row([(d * 2 * nh + nh, gate_b[d, 1]) for d in range(2)])
    bias_rows = jnp.stack([bias_i, bias_f])
    ml_b = _mlstm_call(p, bias_rows, mlstm_norm_g[o], None, True, n_ctx)
    y_ml = _mlstm_call(p, bias_rows, mlstm_norm_g[o], ml_b, False, n_ctx)

    gate_params = jnp.stack([_lane_row([(16, gdn_a_log[o].reshape(-1))]), _lane_row([(16, gdn_dt_bias[o].reshape(-1))])])
    q, k, v, gb = _gdn_prep_call(p, gdn_conv[o], gate_params, n_ctx)
    gd_b = _gdn_call(p, q, k, v, gb, gdn_norm_g[o], None, True, n_ctx)
    y_gd = _gdn_call(p, q, k, v, gb, gdn_norm_g[o], gd_b, False, n_ctx)
    return y_ml, y_gd


def kernel(x, c, ctx, c_ctx, ada_w, ada_b, norm1_g, norm2_g, mix_w_out, ffn_w_in, ffn_dw, ffn_dw_b, ffn_w_out, final_norm_g, ev_w_in, ret_decay_logit, rwkv_mu, rwkv_w0, rwkv_w_up, rwkv_a0, rwkv_a_up, rwkv_k_k, rwkv_k_a, rwkv_r_k, rwkv_ln_g, rwkv_ln_b, rwkv_g_up, od_w_in, mlstm_gate_b, mlstm_norm_g, gdn_conv, gdn_a_log, gdn_dt_bias, gdn_norm_g):
    assert x.shape[0] == 1 and ctx.shape[0] == 1
    n_ctx = ctx.shape[1]
    n_lat = x.shape[1]
    depth = ada_w.shape[0]
    cvecs = jnp.stack([c_ctx, c[0]], axis=1)
    mod = _ada_call(cvecs, ada_w, ada_b)
    xt, h1 = _assemble_call(ctx, x, norm1_g[0], mod)

    for l in range(depth):
        last = l == depth - 1
        half = D_MODEL // 2
        if l % 2 == 0:
            e = l // 2
            p = _proj_call(h1, ev_w_in[e], F_PAD)
            prm = _rwkv_params(e, rwkv_mu, rwkv_w0, rwkv_w_up, rwkv_a0, rwkv_a_up, rwkv_k_k, rwkv_k_a, rwkv_r_k,
                               rwkv_ln_g, rwkv_ln_b, rwkv_g_up)
            ya, yb = _even_mixer(p, e, n_ctx, ret_decay_logit, prm)
        else:
            o = l // 2
            p = _proj_call(h1, _odd_weight(od_w_in[o]), F_PAD)
            ya, yb = _odd_mixer(p, o, n_ctx, mlstm_gate_b, mlstm_norm_g, gdn_conv, gdn_a_log, gdn_dt_bias, gdn_norm_g)
        w_mix = mix_w_out[l].astype(BF16)
        xt, h2 = _outproj_call(ya, yb, w_mix[:half], w_mix[half:], xt, norm2_g[l], mod, l, n_ctx)
        gate, up, wo = _ffn_in_call(h2, ffn_w_in, ffn_w_out, l)
        dw9 = _pad_cols(ffn_dw[l].reshape(9, D_FF), FF_PAD)
        dwb = _pad_cols(ffn_dw_b[l].reshape(1, D_FF), FF_PAD)
        if last:
            xt = _ffn_out_call(gate, up, dw9, dwb, wo, xt, mod, l, final_norm_g, n_ctx, True)
        else:
            xt, h1 = _ffn_out_call(gate, up, dw9, dwb, wo, xt, mod, l, norm1_g[l + 1], n_ctx, False)
    return xt[n_ctx:][None]
```

```python
import functools
import math

import jax
import jax.numpy as jnp
from jax import lax
from jax.experimental import pallas as pl
from jax.experimental.pallas import tpu as pltpu

F32 = jnp.float32
BF16 = jnp.bfloat16

D_MODEL = 2048
DEPTH = 2
GRID_W = 64
EPS = 1e-6
GN_EPS = 64e-5
L2_EPS = 1e-12
D_FF = 5504

RET_HEADS = 8
RET_DK = 128
RWKV_HEADS = 16
RWKV_N = 64
RWKV_W = RWKV_HEADS * RWKV_N
RWKV_LORA = 64
RWKV_GATE_LORA = 160
MLSTM_HEADS = 4
MLSTM_DK = 128
MLSTM_DV = 256
GDN_HEADS = 8
GDN_DK = 128
GDN_QKV = 3072

LANES = 128
SUBLANES = 8
MXU_N = 256
VMEM_LIMIT = 56 * 1024 * 1024

F_PAD = 7680
FF_PAD = 5632
ROW_TILE = 768
WIDE_ROW_TILE = 1408
RET_CHUNK = 256
MLSTM_CHUNK = 256
GDN_CHUNK = 64
RWKV_CHUNK = 64
GDN_SUB = 4
RWKV_SUB = 2


def _wide_tile(t):
    return WIDE_ROW_TILE if t % WIDE_ROW_TILE == 0 else ROW_TILE


def _cparams(n_axes):
    return pltpu.CompilerParams(dimension_semantics=("arbitrary",) * n_axes, vmem_limit_bytes=VMEM_LIMIT)


def _bdot(a, b):
    return jnp.dot(a.astype(BF16), b.astype(BF16), preferred_element_type=F32)


def _bdot_nt(a, b):
    return lax.dot_general(a.astype(BF16), b.astype(BF16), (((1,), (1,)), ((), ())), preferred_element_type=F32)


def _bdot_tn(a, b):
    return lax.dot_general(a.astype(BF16), b.astype(BF16), (((0,), (0,)), ((), ())), preferred_element_type=F32)


def _split3(x):
    x1 = x.astype(BF16)
    r1 = x - x1.astype(F32)
    x2 = r1.astype(BF16)
    x3 = (r1 - x2.astype(F32)).astype(BF16)
    return x1, x2, x3


def _dot_mask_lhs(m_bf16, x):
    x1, x2, x3 = _split3(x)
    d = lambda t: jnp.dot(m_bf16, t, preferred_element_type=F32)
    return (d(x3) + d(x2)) + d(x1)


def _dot3(a, b):
    a1 = a.astype(BF16)
    a2 = (a - a1.astype(F32)).astype(BF16)
    b1 = b.astype(BF16)
    b2 = (b - b1.astype(F32)).astype(BF16)
    d = lambda u, v: jnp.dot(u, v, preferred_element_type=F32)
    return (d(a2, b1) + d(a1, b2)) + d(a1, b1)


def _sigmoid(x):
    return 1.0 / (1.0 + jnp.exp(-x))


def _silu(x):
    return x * _sigmoid(x)


def _softplus(x):
    return jnp.maximum(x, 0.0) + jnp.log1p(jnp.exp(-jnp.abs(x)))


def _log_sigmoid(x):
    return -_softplus(-x)


def _order_masks(c, rev):
    i = lax.broadcasted_iota(jnp.int32, (c, c), 0)
    j = lax.broadcasted_iota(jnp.int32, (c, c), 1)
    if rev:
        return j >= i, j > i
    return j <= i, j < i


def _neumann_inverse(n, c, dotf):
    i = lax.broadcasted_iota(jnp.int32, (c, c), 0)
    j = lax.broadcasted_iota(jnp.int32, (c, c), 1)
    x = jnp.where(i == j, 1.0, 0.0).astype(F32) + n
    p = dotf(n, n)
    for _ in range(int(math.log2(c)) - 2):
        r = dotf(jnp.concatenate([x, p], axis=0), p)
        x = x + r[:c]
        p = r[c:]
    return x + dotf(x, p)


def _neumann_inverse_multi(ns, c, dotf):
    i = lax.broadcasted_iota(jnp.int32, (c, c), 0)
    j = lax.broadcasted_iota(jnp.int32, (c, c), 1)
    eye = jnp.where(i == j, 1.0, 0.0).astype(F32)
    xs = [eye + n for n in ns]
    ps = [dotf(n, n) for n in ns]
    for _ in range(int(math.log2(c)) - 2):
        rs = [dotf(jnp.concatenate([x, p], axis=0), p) for x, p in zip(xs, ps)]
        xs = [x + r[:c] for x, r in zip(xs, rs)]
        ps = [r[c:] for r in rs]
    return [x + dotf(x, p) for x, p in zip(xs, ps)]


def _pair_masks(c, rev):
    assert 2 * c == LANES and c == 64
    row = lax.broadcasted_iota(jnp.int32, (c, 2 * c), 0)
    lane = lax.broadcasted_iota(jnp.int32, (c, 2 * c), 1)
    src = jnp.bitwise_and(lane, c - 1)
    incl2 = (src >= row) if rev else (src <= row)
    strict2 = (src > row) if rev else (src < row)
    return incl2, strict2, src == row, lane < c


def _bdiag(x2, anti=False):
    n2 = x2.shape[1]
    blk_r = jnp.right_shift(lax.broadcasted_iota(jnp.int32, (n2, n2), 0), 6)
    blk_l = jnp.right_shift(lax.broadcasted_iota(jnp.int32, (n2, n2), 1), 6)
    stacked = jnp.concatenate([x2, x2], axis=0)
    if anti:
        return jnp.where(blk_r == blk_l, 0.0, stacked)
    return jnp.where(blk_r == blk_l, stacked, 0.0)


def _neumann_inverse_packed(ns, eye2, c, pdot):
    xs = [jnp.where(eye2, 1.0, 0.0) + n for n in ns]
    ps = [pdot(n, n) for n in ns]
    for _ in range(int(math.log2(c)) - 2):
        rs = [pdot(jnp.concatenate([x, p], axis=0), p) for x, p in zip(xs, ps)]
        xs = [x + r[:c] for x, r in zip(xs, rs)]
        ps = [r[c:] for r in rs]
    return [x + pdot(x, p) for x, p in zip(xs, ps)]


def _dot_inv(a, b):
    return _dot3(a, b)


def _dot_state(a, b):
    return _bdot(a, b)


def _chunk_index(s, rev, n_ctx_chunks, n_chunks):
    if not rev:
        return s
    return jnp.where(s < n_ctx_chunks, n_ctx_chunks - 1 - s, n_chunks + n_ctx_chunks - 1 - s)


def _seg_sum(x, seg):
    c, w = x.shape
    if seg == LANES:
        parts = [jnp.broadcast_to(jnp.sum(x[:, b * LANES:(b + 1) * LANES], axis=-1, keepdims=True), (c, LANES))
                 for b in range(w // LANES)]
        return jnp.concatenate(parts, axis=-1)
    assert seg * 2 == LANES
    lane = lax.broadcasted_iota(jnp.int32, (c, LANES), 1)
    low = lane < seg
    parts = []
    for b in range(w // LANES):
        xb = x[:, b * LANES:(b + 1) * LANES]
        s_lo = jnp.sum(jnp.where(low, xb, 0.0), axis=-1, keepdims=True)
        s_hi = jnp.sum(jnp.where(low, 0.0, xb), axis=-1, keepdims=True)
        parts.append(jnp.where(low, s_lo, s_hi))
    return jnp.concatenate(parts, axis=-1)


def _ada_kernel(c_ref, w_ref, b_ref, o_ref):
    cv = c_ref[...]
    s = _silu(cv)
    w = w_ref[0]
    bias = b_ref[0]
    r0 = jnp.sum(w * s[:, 0:1], axis=0, keepdims=True) + bias
    r1 = jnp.sum(w * s[:, 1:2], axis=0, keepdims=True) + bias
    row = lax.broadcasted_iota(jnp.int32, (SUBLANES, w.shape[1]), 0)
    o_ref[0] = jnp.where(row == 0, r0, jnp.where(row == 1, r1, 0.0))


def _ada_call(cvecs, ada_w, ada_b):
    depth, d, n = ada_w.shape
    tn = 1024
    return pl.pallas_call(
        _ada_kernel,
        grid=(depth, n // tn),
        in_specs=[pl.BlockSpec((d, 2), lambda l, j: (0, 0)),
                  pl.BlockSpec((1, d, tn), lambda l, j: (l, 0, j)),
                  pl.BlockSpec((1, 1, tn), lambda l, j: (l, 0, j))],
        out_specs=pl.BlockSpec((1, SUBLANES, tn), lambda l, j: (l, 0, j)),
        out_shape=jax.ShapeDtypeStruct((depth, SUBLANES, n), F32),
        compiler_params=_cparams(2),
        name="ada_modulation",
    )(cvecs, ada_w, ada_b.reshape(depth, 1, n))


def _mod_rows(mod, k, rows, n_ctx):
    d = D_MODEL
    vc = mod[0:1, k * d:(k + 1) * d]
    vl = mod[1:2, k * d:(k + 1) * d]
    return jnp.where(rows < n_ctx, vc, vl)


def _norm_mod(x, g, shift, scale):
    y = x * lax.rsqrt(jnp.mean(x * x, axis=-1, keepdims=True) + EPS) * g
    return y * (1.0 + scale) + shift


def _assemble_kernel(ctx_ref, x_ref, g_ref, mod_ref, xt_ref, h_ref):
    i = pl.program_id(0)
    d = D_MODEL
    mod = mod_ref[0]

    def emit(src, row):
        xt_ref[...] = src
        h = _norm_mod(src, g_ref[...], mod[row:row + 1, 0:d], mod[row:row + 1, d:2 * d])
        h_ref[...] = h.astype(BF16)

    @pl.when(i == 0)
    def _():
        emit(ctx_ref[0], 0)

    @pl.when(i > 0)
    def _():
        emit(x_ref[0], 1)


def _assemble_call(ctx, x, g, mod):
    n_ctx, d = ctx.shape[1], ctx.shape[2]
    t = n_ctx + x.shape[1]
    assert x.shape[1] % n_ctx == 0
    spec_o = pl.BlockSpec((n_ctx, d), lambda i: (i, 0))
    return pl.pallas_call(
        _assemble_kernel,
        grid=(t // n_ctx,),
        in_specs=[pl.BlockSpec((1, n_ctx, d), lambda i: (0, 0, 0)),
                  pl.BlockSpec((1, n_ctx, d), lambda i: (0, jnp.maximum(i - 1, 0), 0)),
                  pl.BlockSpec((1, d), lambda i: (0, 0)),
                  pl.BlockSpec((1, SUBLANES, 6 * d), lambda i: (0, 0, 0))],
        out_specs=[spec_o, spec_o],
        out_shape=[jax.ShapeDtypeStruct((t, d), F32), jax.ShapeDtypeStruct((t, d), BF16)],
        compiler_params=_cparams(1),
        name="assemble_norm1",
    )(ctx, x, g.reshape(1, d), mod)


def _proj_kernel(h_ref, w_ref, o_ref, w_scr, *, tn, n_valid):
    @pl.when(pl.program_id(1) == 0)
    def _():
        w = w_ref[...]
        col = pl.program_id(0) * tn + lax.broadcasted_iota(jnp.int32, w.shape, 1)
        w_scr[...] = jnp.where(col < n_valid, w, 0).astype(BF16)

    o_ref[...] = jnp.dot(h_ref[...], w_scr[...], preferred_element_type=F32)


def _proj_call(h, w, f_pad):
    t, d = h.shape
    tm, tn = _wide_tile(t), 768
    return pl.pallas_call(
        functools.partial(_proj_kernel, tn=tn, n_valid=w.shape[1]),
        grid=(f_pad // tn, t // tm),
        in_specs=[pl.BlockSpec((tm, d), lambda j, i: (i, 0)),
                  pl.BlockSpec((d, tn), lambda j, i: (0, j))],
        out_specs=pl.BlockSpec((tm, tn), lambda j, i: (i, j)),
        out_shape=jax.ShapeDtypeStruct((t, f_pad), F32),
        scratch_shapes=[pltpu.VMEM((d, tn), BF16)],
        compiler_params=_cparams(2),
        name="in_proj",
    )(h, w)


def _outproj_kernel(ya_ref, yb_ref, wa_ref, wb_ref, x_ref, g_ref, mod_ref, xo_ref, h_ref, *, tm, n_ctx):
    i = pl.program_id(0)
    rows = i * tm + lax.broadcasted_iota(jnp.int32, (tm, 1), 0)
    mod = mod_ref[0]
    acc = jnp.dot(ya_ref[...], wa_ref[...], preferred_element_type=F32)
    acc = acc + jnp.dot(yb_ref[...], wb_ref[...], preferred_element_type=F32)
    xn = x_ref[...] + _mod_rows(mod, 2, rows, n_ctx) * acc
    xo_ref[...] = xn
    h = _norm_mod(xn, g_ref[...], _mod_rows(mod, 3, rows, n_ctx), _mod_rows(mod, 4, rows, n_ctx))
    h_ref[...] = h.astype(BF16)


def _outproj_call(ya, yb, wa, wb, x, g2, mod, layer, n_ctx):
    t, d = x.shape
    half = ya.shape[1]
    tm = 384
    return pl.pallas_call(
        functools.partial(_outproj_kernel, tm=tm, n_ctx=n_ctx),
        grid=(t // tm,),
        in_specs=[pl.BlockSpec((tm, half), lambda i: (i, 0)),
                  pl.BlockSpec((tm, half), lambda i: (i, 0)),
                  pl.BlockSpec((half, d), lambda i: (0, 0)),
                  pl.BlockSpec((half, d), lambda i: (0, 0)),
                  pl.BlockSpec((tm, d), lambda i: (i, 0)),
                  pl.BlockSpec((1, d), lambda i: (0, 0)),
                  pl.BlockSpec((1, SUBLANES, 6 * d), lambda i: (layer, 0, 0))],
        out_specs=[pl.BlockSpec((tm, d), lambda i: (i, 0)),
                   pl.BlockSpec((tm, d), lambda i: (i, 0))],
        out_shape=[jax.ShapeDtypeStruct((t, d), F32), jax.ShapeDtypeStruct((t, d), BF16)],
        compiler_params=_cparams(1),
        name="mix_out_proj_norm2",
    )(ya, yb, wa, wb, x, g2.reshape(1, d), mod)


def _ffn_in_kernel(h_ref, wg_ref, wu0_ref, wu1_ref, wu2_ref, wu3_ref, wo_ref, g_ref, u_ref, wob_ref, w_scr, *, tn):
    @pl.when(pl.program_id(1) == 0)
    def _():
        j = pl.program_id(0)
        wu = jnp.concatenate([wu0_ref[0], wu1_ref[0], wu2_ref[0], wu3_ref[0]], axis=1)
        col = j * tn + lax.broadcasted_iota(jnp.int32, wu.shape, 1)
        w_scr[0] = jnp.where(col < D_FF, wg_ref[0], 0.0).astype(BF16)
        w_scr[1] = jnp.where(col < D_FF, wu, 0.0).astype(BF16)
        wo = wo_ref[0]
        row = j * tn + lax.broadcasted_iota(jnp.int32, wo.shape, 0)
        wob_ref[...] = jnp.where(row < D_FF, wo, 0.0).astype(BF16)

    h = h_ref[...]
    g_ref[...] = jnp.dot(h, w_scr[0], preferred_element_type=F32).astype(BF16)
    u_ref[...] = jnp.dot(h, w_scr[1], preferred_element_type=F32).astype(BF16)


def _ffn_in_call(h, ffn_w_in, ffn_w_out, layer):
    t, d = h.shape
    tm, tn = ROW_TILE, 512
    q = tn // LANES
    assert D_FF % LANES == 0 and q == 4
    up0 = D_FF // LANES
    last = 2 * D_FF // LANES - 1
    spec_o = pl.BlockSpec((tm, tn), lambda j, i: (i, j))
    up_specs = [pl.BlockSpec((1, d, LANES), lambda j, i, r=r: (layer, 0, jnp.minimum(up0 + q * j + r, last)))
                for r in range(q)]
    return pl.pallas_call(
        functools.partial(_ffn_in_kernel, tn=tn),
        grid=(FF_PAD // tn, t // tm),
        in_specs=[pl.BlockSpec((tm, d), lambda j, i: (i, 0)),
                  pl.BlockSpec((1, d, tn), lambda j, i: (layer, 0, j))] + up_specs
                 + [pl.BlockSpec((1, tn, d), lambda j, i: (layer, j, 0))],
        out_specs=[spec_o, spec_o, pl.BlockSpec((tn, d), lambda j, i: (j, 0))],
        out_shape=[jax.ShapeDtypeStruct((t, FF_PAD), BF16)] * 2 + [jax.ShapeDtypeStruct((FF_PAD, d), BF16)],
        scratch_shapes=[pltpu.VMEM((2, d, tn), BF16)],
        compiler_params=_cparams(2),
        name="ffn_in_proj",
    )(h, ffn_w_in, ffn_w_in, ffn_w_in, ffn_w_in, ffn_w_in, ffn_w_out)


def _glu_act(conv, bias, up):
    gate = conv + bias
    return (0.5 * gate * (1.0 + lax.erf(gate * (2.0 ** -0.5))) * up.astype(F32)).astype(BF16)


def _ffn_out_kernel(*refs, tm, tk, n_ctx, t_total, final_norm):
    if final_norm:
        (gm_ref, gp_ref, gn_ref, u_ref, dw_ref, db_ref, wo_ref, x_ref, mod_ref, fg_ref,
         o_ref, acc_scr, act_a, act_b) = refs
    else:
        (gm_ref, gp_ref, gn_ref, u_ref, dw_ref, db_ref, wo_ref, x_ref, mod_ref, fg_ref, nmod_ref,
         o_ref, hn_ref, acc_scr, act_a, act_b) = refs
    i = pl.program_id(0)
    k = pl.program_id(1)
    nk = pl.num_programs(1) - 1
    w = GRID_W
    nrow = tm // w
    blk0 = i * nrow
    nb_ctx, nb_tot = n_ctx // w, t_total // w
    sub = lax.broadcasted_iota(jnp.int32, (SUBLANES, LANES), 0)

    def neighbours(g, n):
        gl = pltpu.roll(g, 1, axis=0)
        gr = pltpu.roll(g, n - 1, axis=0)
        gl = jnp.concatenate([jnp.where(sub == 0, 0.0, gl[:SUBLANES]), gl[SUBLANES:]], axis=0)
        gr = jnp.concatenate([gr[:n - SUBLANES], jnp.where(sub == SUBLANES - 1, 0.0, gr[n - SUBLANES:])], axis=0)
        return gl, gr

    def step(dst, src):
        d_out = acc_scr.shape[1]
        row_halves = 2
        hm = tm // row_halves
        n_mm = row_halves * d_out // MXU_N
        n_pieces = (tk // LANES) * (nrow + 2)

        def matmul_chunk(n):
            cs = slice((n // row_halves) * MXU_N, (n // row_halves + 1) * MXU_N)
            rs = slice((n % row_halves) * hm, (n % row_halves + 1) * hm)
            acc_scr[rs, cs] += jnp.dot(src[rs, :], wo_ref[:, cs], preferred_element_type=F32)

        mm_at = {(n * n_pieces) // n_mm: n for n in range(n_mm)}
        assert len(mm_at) == n_mm
        piece = 0
        for lb in range(tk // LANES):
            ls = slice(lb * LANES, (lb + 1) * LANES)
            dwv = dw_ref[:, ls]
            bias = db_ref[:, ls]
            part = [None] * nrow
            for r in range(-1, nrow + 1):
                if piece in mm_at:
                    matmul_chunk(mm_at[piece])
                piece += 1
                if r == -1:
                    g = gp_ref[:, ls]
                elif r == nrow:
                    g = gn_ref[:, ls]
                else:
                    g = gm_ref[r * w:(r + 1) * w, ls]
                g = g.astype(F32)
                gl, gr = neighbours(g, w)
                for kh, ro in ((0, r + 1), (1, r), (2, r - 1)):
                    if not 0 <= ro < nrow:
                        continue
                    wv = dwv[3 * kh:3 * kh + 3]
                    if kh == 0:
                        wv = wv * jnp.where(blk0 + ro >= nb_ctx + 1, 1.0, 0.0)
                    if kh == 2:
                        ok = jnp.logical_and(blk0 + ro >= nb_ctx, blk0 + ro < nb_tot - 1)
                        wv = wv * jnp.where(ok, 1.0, 0.0)
                    c = gl * wv[0:1] + g * wv[1:2] + gr * wv[2:3]
                    part[ro] = c if part[ro] is None else part[ro] + c
                ro = r - 1
                if 0 <= ro < nrow:
                    rs = slice(ro * w, (ro + 1) * w)
                    dst[rs, ls] = _glu_act(part[ro], bias, u_ref[rs, ls])
                    part[ro] = None

        @pl.when(i == 0)
        def _():
            for lb in range(tk // LANES):
                ls = slice(lb * LANES, (lb + 1) * LANES)
                g = gm_ref[0:n_ctx, ls].astype(F32)
                gl, gr = neighbours(g, n_ctx)
                conv = gl * dw_ref[3:4, ls] + g * dw_ref[4:5, ls] + gr * dw_ref[5:6, ls]
                dst[0:n_ctx, ls] = _glu_act(conv, db_ref[:, ls], u_ref[0:n_ctx, ls])

    @pl.when(k == 0)
    def _():
        acc_scr[...] = jnp.zeros_like(acc_scr)
        act_b[...] = jnp.zeros_like(act_b)

    @pl.when(k % 2 == 0)
    def _():
        step(act_a, act_b)

    @pl.when(k % 2 == 1)
    def _():
        step(act_b, act_a)

    @pl.when(k == nk)
    def _():
        rb = 2 * LANES
        for r0 in range(0, tm, rb):
            rs = slice(r0, r0 + rb)
            rws = i * tm + r0 + lax.broadcasted_iota(jnp.int32, (rb, 1), 0)
            xn = x_ref[rs, :] + _mod_rows(mod_ref[0], 5, rws, n_ctx) * acc_scr[rs, :]
            if final_norm:
                xn = xn * lax.rsqrt(jnp.mean(xn * xn, axis=-1, keepdims=True) + EPS) * fg_ref[...]
            else:
                nmod = nmod_ref[0]
                hn = _norm_mod(xn, fg_ref[...], _mod_rows(nmod, 0, rws, n_ctx), _mod_rows(nmod, 1, rws, n_ctx))
                hn_ref[rs, :] = hn.astype(BF16)
            o_ref[rs, :] = xn


def _ffn_out_call(gate, up, dw9, dwb, wo, x, mod, layer, norm_g, n_ctx, final_norm):
    t, d = x.shape
    fp = gate.shape[1]
    tm, tk, w = ROW_TILE, 512, GRID_W
    assert n_ctx <= tm and n_ctx % w == 0 and tm % (2 * LANES) == 0
    rpt = tm // w
    n_rows = t // w
    spec_x = pl.BlockSpec((tm, d), lambda i, k: (i, 0))
    extra_in, extra_args = [], []
    out_specs, out_shape = spec_x, jax.ShapeDtypeStruct((t, d), F32)
    if not final_norm:
        extra_in = [pl.BlockSpec((1, SUBLANES, 6 * d), lambda i, k: (layer + 1, 0, 0))]
        extra_args = [mod]
        out_specs = [spec_x, spec_x]
        out_shape = [out_shape, jax.ShapeDtypeStruct((t, d), BF16)]
    nk = fp // tk
    assert nk % 2 == 1
    kc = lambda k: jnp.minimum(k, nk - 1)
    return pl.pallas_call(
        functools.partial(_ffn_out_kernel, tm=tm, tk=tk, n_ctx=n_ctx, t_total=t, final_norm=final_norm),
        grid=(t // tm, nk + 1),
        in_specs=[pl.BlockSpec((tm, tk), lambda i, k: (i, kc(k))),
                  pl.BlockSpec((w, tk), lambda i, k: (jnp.maximum(i * rpt - 1, 0), kc(k))),
                  pl.BlockSpec((w, tk), lambda i, k: (jnp.minimum((i + 1) * rpt, n_rows - 1), kc(k))),
                  pl.BlockSpec((tm, tk), lambda i, k: (i, kc(k))),
                  pl.BlockSpec((9, tk), lambda i, k: (0, kc(k))),
                  pl.BlockSpec((1, tk), lambda i, k: (0, kc(k))),
                  pl.BlockSpec((tk, d), lambda i, k: (jnp.maximum(k - 1, 0), 0)),
                  spec_x,
                  pl.BlockSpec((1, SUBLANES, 6 * d), lambda i, k: (layer, 0, 0)),
                  pl.BlockSpec((1, d), lambda i, k: (0, 0))] + extra_in,
        out_specs=out_specs,
        out_shape=out_shape,
        scratch_shapes=[pltpu.VMEM((tm, d), F32), pltpu.VMEM((tm, tk), BF16), pltpu.VMEM((tm, tk), BF16)],
        compiler_params=_cparams(2),
        name="ffn_conv_glu_out",
    )(gate, gate, gate, up, dw9, dwb, wo, x, mod, norm_g.reshape(1, d), *extra_args)


def _retention_kernel(*refs, rev, final, c):
    if final:
        q_ref, k_ref, v_ref, lg_ref, gate_ref, ob_ref, o_ref, r_scr, di_scr, dq_scr, dk_scr, dc_scr = refs
    else:
        q_ref, k_ref, v_ref, lg_ref, o_ref, r_scr, di_scr, dq_scr, dk_scr, dc_scr = refs
    d = 1 if rev else 0
    dk = RET_DK

    @pl.when(pl.program_id(0) == 0)
    def _():
        r_scr[...] = jnp.zeros_like(r_scr)
        i = lax.broadcasted_iota(jnp.int32, (c, c), 0)
        j = lax.broadcasted_iota(jnp.int32, (c, c), 1)
        diff = ((j - i) if rev else (i - j)).astype(F32)
        row = lax.broadcasted_iota(jnp.int32, (c, dk), 0)
        pos = ((c - 1 - row) if rev else row).astype(F32)
        for h in range(RET_HEADS):
            lg = _log_sigmoid(lg_ref[d:d + 1, h:h + 1])
            di_scr[h] = jnp.where(diff >= 0, jnp.exp(lg * jnp.maximum(diff, 0.0)), 0.0)
            dq_scr[h] = jnp.exp(lg * (pos + 1.0))
            dk_scr[h] = jnp.exp(lg * (c - 1.0 - pos)) * (dk ** -0.5)
            dc_scr[h] = jnp.exp(jnp.broadcast_to(lg, (1, dk)) * c)

    for h in range(RET_HEADS):
        sl = slice(h * dk, (h + 1) * dk)
        qh = q_ref[:, sl]
        kh = k_ref[:, sl]
        vh = v_ref[:, sl].astype(BF16)
        r_state = r_scr[h]
        scores = _bdot_nt(qh, kh) * (di_scr[h] * (dk ** -0.5))
        out = _bdot(scores, vh) + _bdot(qh * dq_scr[h], r_state)
        r_scr[h] = dc_scr[h] * r_state + _bdot_tn(kh * dk_scr[h], vh)
        if final:
            tot = out + ob_ref[:, sl]
            y = tot * lax.rsqrt(jnp.mean(tot * tot, axis=-1, keepdims=True) + EPS) * _silu(gate_ref[:, sl])
            o_ref[:, sl] = y.astype(o_ref.dtype)
        else:
            o_ref[:, sl] = out


def _retention_call(p, logit, other, rev, n_ctx):
    t = p.shape[0]
    c = RET_CHUNK
    n, nc = t // c, n_ctx // c
    width = RET_HEADS * RET_DK
    final = other is not None
    cm = lambda s: _chunk_index(s, rev, nc, n)
    in_specs = [pl.BlockSpec((c, width), lambda s: (cm(s), 0)),
                pl.BlockSpec((c, width), lambda s: (cm(s), 1)),
                pl.BlockSpec((c, width), lambda s: (cm(s), 2)),
                pl.BlockSpec((2, RET_HEADS), lambda s: (0, 0))]
    args = [p, p, p, logit]
    if final:
        in_specs += [pl.BlockSpec((c, width), lambda s: (cm(s), 3)), pl.BlockSpec((c, width), lambda s: (cm(s), 0))]
        args += [p, other]
    return pl.pallas_call(
        functools.partial(_retention_kernel, rev=rev, final=final, c=c),
        grid=(n,),
        in_specs=in_specs,
        out_specs=pl.BlockSpec((c, width), lambda s: (cm(s), 0)),
        out_shape=jax.ShapeDtypeStruct((t, width), BF16 if final else F32),
        scratch_shapes=[pltpu.VMEM((RET_HEADS, RET_DK, RET_DK), F32),
                        pltpu.VMEM((RET_HEADS, c, c), F32),
                        pltpu.VMEM((RET_HEADS, c, RET_DK), F32),
                        pltpu.VMEM((RET_HEADS, c, RET_DK), F32),
                        pltpu.VMEM((RET_HEADS, 1, RET_DK), F32)],
        compiler_params=_cparams(1),
        name="retention_bwd" if rev else "retention_fwd_merge",
    )(*args)


def _rwkv_kernel(*refs, rev, final, c, nsub, nc):
    if final:
        (r_ref, k_ref, v_ref, wd_ref, ad_ref, mu_ref, mus_ref, vec_ref, wup_ref, aup_ref,
         gl_ref, gup_ref, yb_ref, bb_ref, o_ref, st_scr, carry_scr) = refs
    else:
        (r_ref, k_ref, v_ref, wd_ref, ad_ref, mu_ref, mus_ref, vec_ref, wup_ref, aup_ref,
         y_ref, bonus_ref, st_scr, carry_scr) = refs
    d = 1 if rev else 0
    s = pl.program_id(0)
    n = RWKV_N
    wdt = RWKV_W

    @pl.when(s == 0)
    def _():
        st_scr[...] = jnp.zeros_like(st_scr)

    @pl.when(jnp.logical_or(s == 0, s == nc))
    def _():
        carry_scr[...] = jnp.zeros_like(carry_scr)

    rows = nsub * c
    row = lax.broadcasted_iota(jnp.int32, (rows, 1), 0)
    edge = (row == rows - 1) if rev else (row == 0)
    last = rows - 1 if not rev else 0
    keep = c - 1 if not rev else 0

    def shifted(x, lo):
        width = x.shape[1]
        prev = pltpu.roll(x, (rows - 1) if rev else 1, axis=0)
        prev = jnp.where(edge, carry_scr[0:1, lo:lo + width], prev)
        carry_scr[0:1, lo:lo + width] = x[last:last + 1, :]
        return prev

    def mix(x, lo, mu):
        prev = shifted(x, lo)
        return x + (prev - x) * mu

    r = mix(r_ref[...], 0, mu_ref[0:1, 0:wdt])
    k = mix(k_ref[...], wdt, mu_ref[0:1, wdt:2 * wdt])
    v = mix(v_ref[...], 2 * wdt, mu_ref[0:1, 2 * wdt:3 * wdt])
    wd = mix(wd_ref[...], 3 * wdt, mus_ref[0:1, :])[:, d * RWKV_LORA:(d + 1) * RWKV_LORA]
    ad = mix(ad_ref[...], 3 * wdt + LANES, mus_ref[1:2, :])[:, d * RWKV_LORA:(d + 1) * RWKV_LORA]

    w0, a0 = vec_ref[0:1, :], vec_ref[1:2, :]
    k_k, k_a, r_k = vec_ref[2:3, :], vec_ref[3:4, :], vec_ref[4:5, :]
    w_log = -_softplus(-(w0 + _bdot(jnp.tanh(wd), wup_ref[...]))) - 0.5
    lw = -jnp.exp(w_log)
    a = _sigmoid(a0 + _bdot(ad, aup_ref[...]))
    kk = k * k_k
    kk = kk * lax.rsqrt(_seg_sum(kk * kk, n) + L2_EPS)
    k = k * (1.0 + (a - 1.0) * k_a)
    bonus = _seg_sum(r * k * r_k, n) * v
    b = kk * a

    ri = lax.broadcasted_iota(jnp.int32, (rows, rows), 0)
    ci = lax.broadcasted_iota(jnp.int32, (rows, rows), 1)
    ri_in, ci_in = jnp.bitwise_and(ri, c - 1), jnp.bitwise_and(ci, c - 1)
    before = (ci_in >= ri_in) if rev else (ci_in <= ri_in)
    cmask = jnp.logical_and(jnp.right_shift(ri, 6) == jnp.right_shift(ci, 6), before)
    cum = _dot_mask_lhs(jnp.where(cmask, 1.0, 0.0).astype(BF16), lw)
    cum_end = jnp.concatenate([jnp.broadcast_to(cum[u * c + keep:u * c + keep + 1, :], (c, wdt)) for u in range(nsub)],
                              axis=0)
    e_pos = jnp.exp(cum)
    e_neg = jnp.exp(-cum)
    r_t = r * e_pos
    a_t = -kk * jnp.exp(cum - lw)
    k_t = k * e_neg
    b_t = b * e_neg
    w_end = jnp.exp(cum_end)
    kw = k_t * w_end
    bw = b_t * w_end
    assert c == n
    pw = 2 * n
    incl2, strict2, eye2, lo_c = _pair_masks(c, rev)
    lo_2c = lax.broadcasted_iota(jnp.int32, (2 * c, pw), 1) < n
    bdiag = _bdiag

    adiag = functools.partial(_bdiag, anti=True)

    def pdot_inv(a2, b2):
        return _dot_inv(a2, bdiag(b2))

    n_pairs = RWKV_HEADS // 2
    subs = list(range(nsub))
    inst = [(u, p) for u in subs for p in range(n_pairs)]
    sl = {(u, p): (slice(u * c, (u + 1) * c), slice(p * pw, (p + 1) * pw)) for u, p in inst}
    ar = [jnp.concatenate([a_t[sl[x]], r_t[sl[x]]], axis=0) for x in inst]
    s_e = [_bdot_nt(jnp.where(lo_2c, y, 0.0), jnp.concatenate([b_t[sl[x]], k_t[sl[x]]], axis=0))
           for y, x in zip(ar, inst)]
    s_o = [_bdot_nt(jnp.where(lo_2c, 0.0, y), jnp.concatenate([k_t[sl[x]], b_t[sl[x]]], axis=0))
           for y, x in zip(ar, inst)]
    a_ab = [jnp.where(strict2, jnp.where(lo_c, e[:c], o[:c]), 0.0) for e, o in zip(s_e, s_o)]
    a_rb = [jnp.where(incl2, jnp.where(lo_c, e[c:], o[c:]), 0.0) for e, o in zip(s_e, s_o)]
    a_kk = [jnp.concatenate([jnp.where(strict2, jnp.where(lo_c, o[:c], e[:c]), 0.0),
                             jnp.where(incl2, jnp.where(lo_c, o[c:], e[c:]), 0.0)], axis=0)
            for e, o in zip(s_e, s_o)]
    tms = _neumann_inverse_packed(a_ab, eye2, c, pdot_inv)
    av = [_bdot(a, adiag(v[sl[x]])) for a, x in zip(a_kk, inst)]
    pp = [_dot_inv(t, jnp.concatenate([bdiag(a_t[sl[x]]), bdiag(y[:c])], axis=1))
          for t, x, y in zip(tms, inst, av)]
    rq_y0 = [_bdot(a, jnp.concatenate([bdiag(y[:, :pw]), bdiag(y[:, pw:])], axis=1)) for a, y in zip(a_rb, pp)]
    m_g = [_bdot_tn(bw[sl[x]], y) for x, y in zip(inst, pp)]
    kv = [_bdot_tn(kw[sl[x]], v[sl[x]]) for x in inst]
    idx = {x: i for i, x in enumerate(inst)}
    y_rows = {}
    for u in (subs[::-1] if rev else subs):
        so = []
        for p in range(n_pairs):
            i = idx[(u, p)]
            rq = r_t[sl[(u, p)]] + rq_y0[i][:, :pw]
            w_row = w_end[u * c:u * c + 1, p * pw:(p + 1) * pw]
            m = (jnp.where(lo_c, m_g[i][:n, :pw], m_g[i][n:, :pw])
                 + jnp.where(eye2, jnp.broadcast_to(w_row, (n, pw)), 0.0))
            so.append(_dot_state(jnp.concatenate([rq, m], axis=0), bdiag(st_scr[p])))
        ys = []
        for p in range(n_pairs):
            i = idx[(u, p)]
            g = jnp.where(lo_c, m_g[i][:n, pw:], m_g[i][n:, pw:]) + jnp.where(lo_c, kv[i][:n], kv[i][n:])
            st_scr[p] = so[p][c:] + g
            ys.append(so[p][:c] + (rq_y0[i][:, pw:] + av[i][c:]))
        y_rows[u] = jnp.concatenate(ys, axis=1)
    y = jnp.concatenate([y_rows[u] for u in subs], axis=0)

    if final:
        ysum = y + yb_ref[...]
        mu_h = _seg_sum(ysum, n) * (1.0 / n)
        yc = ysum - mu_h
        var = _seg_sum(yc * yc, n) * (1.0 / n)
        ln_g, ln_b = vec_ref[5:6, :], vec_ref[6:7, :]
        yn = yc * lax.rsqrt(var + GN_EPS) * ln_g + ln_b + bonus + bb_ref[...]
        gate = _bdot(_sigmoid(gl_ref[...]), gup_ref[...])
        o_ref[...] = (yn * gate).astype(o_ref.dtype)
    else:
        y_ref[...] = y
        bonus_ref[...] = bonus


def _rwkv_call(p, prm, other, rev, n_ctx):
    t = p.shape[0]
    c = RWKV_SUB * RWKV_CHUNK
    n, nc = t // c, n_ctx // c
    wdt = RWKV_W
    d = 1 if rev else 0
    final = other is not None
    cm = lambda s: _chunk_index(s, rev, nc, n)
    base = 4096 // wdt
    full = lambda shape: pl.BlockSpec(shape, lambda s: (0,) * len(shape))
    in_specs = [pl.BlockSpec((c, wdt), lambda s: (cm(s), base)),
                pl.BlockSpec((c, wdt), lambda s: (cm(s), base + 1)),
                pl.BlockSpec((c, wdt), lambda s: (cm(s), base + 2)),
                pl.BlockSpec((c, LANES), lambda s: (cm(s), 7168 // LANES)),
                pl.BlockSpec((c, LANES), lambda s: (cm(s), 7168 // LANES + 1)),
                full((1, 3 * wdt)), full((2, LANES)), full((SUBLANES, wdt)),
                full((RWKV_LORA, wdt)), full((RWKV_LORA, wdt))]
    args = [p, p, p, p, p, prm["mu"][d], prm["mu_small"][d], prm["vecs"][d], prm["w_up"][d], prm["a_up"][d]]
    if final:
        in_specs += [pl.BlockSpec((c, 2 * LANES), lambda s: (cm(s), 7424 // (2 * LANES))),
                     full((2 * LANES, wdt)),
                     pl.BlockSpec((c, wdt), lambda s: (cm(s), 0)),
                     pl.BlockSpec((c, wdt), lambda s: (cm(s), 0))]
        args += [p, prm["g_up"], other[0], other[1]]
        out_specs = pl.BlockSpec((c, wdt), lambda s: (cm(s), 0))
        out_shape = jax.ShapeDtypeStruct((t, wdt), BF16)
    else:
        out_specs = [pl.BlockSpec((c, wdt), lambda s: (cm(s), 0))] * 2
        out_shape = [jax.ShapeDtypeStruct((t, wdt), F32)] * 2
    return pl.pallas_call(
        functools.partial(_rwkv_kernel, rev=rev, final=final, c=RWKV_CHUNK, nsub=RWKV_SUB, nc=nc),
        grid=(n,),
        in_specs=in_specs,
        out_specs=out_specs,
        out_shape=out_shape,
        scratch_shapes=[pltpu.VMEM((RWKV_HEADS // 2, RWKV_N, 2 * RWKV_N), F32),
                        pltpu.VMEM((SUBLANES, 3 * wdt + 2 * LANES), F32)],
        compiler_params=_cparams(1),
        name="rwkv7_bwd" if rev else "rwkv7_fwd_merge",
    )(*args)


def _mlstm_kernel(*refs, rev, final, c):
    if final:
        q_ref, k_ref, v_ref, sm_ref, bias_ref, og_ref, ng_ref, hb_ref, o_ref, ct_scr, m_scr = refs
    else:
        q_ref, k_ref, v_ref, sm_ref, bias_ref, o_ref, ct_scr, m_scr = refs
    d = 1 if rev else 0
    dk, dv = MLSTM_DK, MLSTM_DV
    nh = MLSTM_HEADS

    @pl.when(pl.program_id(0) == 0)
    def _():
        ct_scr[...] = jnp.zeros_like(ct_scr)
        m_scr[...] = jnp.zeros_like(m_scr)

    incl, _ = _order_masks(c, rev)
    keep = 0 if rev else c - 1
    sm = sm_ref[...]
    li = sm + bias_ref[0:1, :]
    lf = _log_sigmoid(sm + bias_ref[1:2, :])
    bcum = _dot_mask_lhs(jnp.where(incl, 1.0, 0.0).astype(BF16), lf)
    bcum_t = bcum.T
    li_t = li.T
    ones_col = jnp.where(lax.broadcasted_iota(jnp.int32, (c, LANES), 1) == 0, 1.0, 0.0).astype(BF16)

    heads = range(nh)
    cis = [d * 2 * nh + h for h in heads]
    b_col = [bcum[:, ci + nh:ci + nh + 1] for ci in cis]
    i_col = [li[:, ci:ci + 1] for ci in cis]
    m_prev = [m_scr[h:h + 1, 0:1] for h in heads]
    qs = [q_ref[:, h * dk:(h + 1) * dk] * (dk ** -0.5) for h in heads]
    ks = [k_ref[:, h * dk:(h + 1) * dk] for h in heads]
    v_aug = [jnp.concatenate([v_ref[:, h * dv:(h + 1) * dv].astype(BF16), ones_col], axis=1) for h in heads]
    qk_raw = [_bdot_nt(q, k) for q, k in zip(qs, ks)]
    q_ct = [_bdot(q, ct_scr[h]) for q, h in zip(qs, heads)]
    d_log = [jnp.where(incl, bc - bcum_t[ci + nh:ci + nh + 1, :] + li_t[ci:ci + 1, :], -jnp.inf)
             for bc, ci in zip(b_col, cis)]
    inter = [bc + mp for bc, mp in zip(b_col, m_prev)]
    m_t = [jnp.maximum(jnp.max(dl, axis=-1, keepdims=True), it) for dl, it in zip(d_log, inter)]
    qk = [x * jnp.exp(dl - mt) for x, dl, mt in zip(qk_raw, d_log, m_t)]
    num_aug = [_bdot(x, va) + jnp.exp(it - mt) * qc for x, va, it, mt, qc in zip(qk, v_aug, inter, m_t, q_ct)]
    b_end = [bc[keep:keep + 1, :] for bc in b_col]
    w_log = [be - bc + ic for be, bc, ic in zip(b_end, b_col, i_col)]
    m_new = [jnp.maximum(be + mp, jnp.max(wl, axis=0, keepdims=True)) for be, mp, wl in zip(b_end, m_prev, w_log)]
    kv = [_bdot_tn(k * jnp.exp(wl - mn), va) for k, wl, mn, va in zip(ks, w_log, m_new, v_aug)]
    for h in heads:
        ct_scr[h] = jnp.exp(b_end[h] + m_prev[h] - m_new[h]) * ct_scr[h] + kv[h]
        m_scr[h:h + 1, :] = jnp.broadcast_to(m_new[h], (1, LANES))
        den = num_aug[h][:, dv:dv + 1]
        hout = num_aug[h][:, :dv] / jnp.maximum(jnp.abs(den), jnp.exp(-m_t[h]))
        sl = slice(h * dv, (h + 1) * dv)
        if final:
            tot = hout + hb_ref[:, sl]
            y = tot * lax.rsqrt(jnp.mean(tot * tot, axis=-1, keepdims=True) + EPS)
            o_ref[:, sl] = (y * ng_ref[0:1, sl] * _sigmoid(og_ref[:, sl])).astype(o_ref.dtype)
        else:
            o_ref[:, sl] = hout


def _mlstm_call(p, bias_rows, norm_g, other, rev, n_ctx):
    t = p.shape[0]
    c = MLSTM_CHUNK
    n, nc = t // c, n_ctx // c
    dk, dv, nh = MLSTM_DK, MLSTM_DV, MLSTM_HEADS
    final = other is not None
    cm = lambda s: _chunk_index(s, rev, nc, n)
    full = lambda shape: pl.BlockSpec(shape, lambda s: (0,) * len(shape))
    in_specs = [pl.BlockSpec((c, nh * dk), lambda s: (cm(s), 0)),
                pl.BlockSpec((c, nh * dk), lambda s: (cm(s), 1)),
                pl.BlockSpec((c, nh * dv), lambda s: (cm(s), 1)),
                pl.BlockSpec((c, LANES), lambda s: (cm(s), 7168 // LANES)),
                full((2, LANES))]
    args = [p, p, p, p, bias_rows]
    if final:
        in_specs += [pl.BlockSpec((c, nh * dv), lambda s: (cm(s), 2)), full((1, nh * dv)),
                     pl.BlockSpec((c, nh * dv), lambda s: (cm(s), 0))]
        args += [p, norm_g.reshape(1, nh * dv), other]
    return pl.pallas_call(
        functools.partial(_mlstm_kernel, rev=rev, final=final, c=c),
        grid=(n,),
        in_specs=in_specs,
        out_specs=pl.BlockSpec((c, nh * dv), lambda s: (cm(s), 0)),
        out_shape=jax.ShapeDtypeStruct((t, nh * dv), BF16 if final else F32),
        scratch_shapes=[pltpu.VMEM((nh, dk, dv + LANES), F32), pltpu.VMEM((SUBLANES, LANES), F32)],
        compiler_params=_cparams(1),
        name="mlstm_bwd" if rev else "mlstm_fwd_merge",
    )(*args)


def _gdn_prep_kernel(xm_ref, xp_ref, xn_ref, cw_ref, sm_ref, gp_ref, q_ref, k_ref, v_ref, gb_ref, *, tm, n_ctx, t_total):
    i = pl.program_id(0)
    x = xm_ref[...]
    rows = i * tm + lax.broadcasted_iota(jnp.int32, (tm, 1), 0)
    first = rows == i * tm
    last = rows == i * tm + tm - 1
    prev = jnp.where(first, xp_ref[SUBLANES - 1:SUBLANES, :], pltpu.roll(x, 1, axis=0))
    nxt = jnp.where(last, xn_ref[0:1, :], pltpu.roll(x, tm - 1, axis=0))
    prev = jnp.where(jnp.logical_or(rows == 0, rows == n_ctx), 0.0, prev)
    nxt = jnp.where(jnp.logical_or(rows == n_ctx - 1, rows == t_total - 1), 0.0, nxt)
    y = _silu(prev * cw_ref[0:1, :] + x * cw_ref[1:2, :] + nxt * cw_ref[2:3, :])
    w = GDN_HEADS * GDN_DK
    q, k, v = y[:, :w], y[:, w:2 * w], y[:, 2 * w:]
    q_ref[...] = (q * lax.rsqrt(_seg_sum(q * q, GDN_DK) + L2_EPS) * (GDN_DK ** -0.5)).astype(BF16)
    k_ref[...] = (k * lax.rsqrt(_seg_sum(k * k, GDN_DK) + L2_EPS)).astype(BF16)
    v_ref[...] = v.astype(BF16)
    sm = sm_ref[...]
    lane = lax.broadcasted_iota(jnp.int32, sm.shape, 1)
    log_alpha = -jnp.exp(gp_ref[0:1, :]) * _softplus(sm + gp_ref[1:2, :])
    gb_ref[...] = jnp.where(lane < 32, log_alpha, _sigmoid(sm))


def _gdn_prep_call(p, conv_w, gate_params, n_ctx):
    t = p.shape[0]
    tm = 256
    w = GDN_HEADS * GDN_DK
    qkv_blk = 3072 // GDN_QKV
    nb8 = t // SUBLANES
    r8 = tm // SUBLANES
    spec_o = pl.BlockSpec((tm, w), lambda i: (i, 0))
    return pl.pallas_call(
        functools.partial(_gdn_prep_kernel, tm=tm, n_ctx=n_ctx, t_total=t),
        grid=(t // tm,),
        in_specs=[pl.BlockSpec((tm, GDN_QKV), lambda i: (i, qkv_blk)),
                  pl.BlockSpec((SUBLANES, GDN_QKV), lambda i: (jnp.maximum(i * r8 - 1, 0), qkv_blk)),
                  pl.BlockSpec((SUBLANES, GDN_QKV), lambda i: (jnp.minimum((i + 1) * r8, nb8 - 1), qkv_blk)),
                  pl.BlockSpec((3, GDN_QKV), lambda i: (0, 0)),
                  pl.BlockSpec((tm, LANES), lambda i: (i, 7168 // LANES)),
                  pl.BlockSpec((2, LANES), lambda i: (0, 0))],
        out_specs=[spec_o, spec_o, spec_o, pl.BlockSpec((tm, LANES), lambda i: (i, 0))],
        out_shape=[jax.ShapeDtypeStruct((t, w), BF16)] * 3 + [jax.ShapeDtypeStruct((t, LANES), F32)],
        compiler_params=_cparams(1),
        name="gdn_conv_norm_gates",
    )(p, p, p, conv_w, p, gate_params)


def _gdn_kernel(*refs, rev, final, c, nsub):
    if final:
        q_ref, k_ref, v_ref, gb_ref, gate_ref, ng_ref, ob_ref, o_ref, st_scr = refs
    else:
        q_ref, k_ref, v_ref, gb_ref, o_ref, st_scr = refs
    d = 1 if rev else 0
    dk = GDN_DK
    nh = GDN_HEADS
    rows_all = nsub * c

    @pl.when(pl.program_id(0) == 0)
    def _():
        st_scr[...] = jnp.zeros_like(st_scr)

    keep = 0 if rev else c - 1
    gb = gb_ref[...]
    ri = lax.broadcasted_iota(jnp.int32, (rows_all, rows_all), 0)
    ci = lax.broadcasted_iota(jnp.int32, (rows_all, rows_all), 1)
    ri_in, ci_in = jnp.bitwise_and(ri, c - 1), jnp.bitwise_and(ci, c - 1)
    before = (ci_in >= ri_in) if rev else (ci_in <= ri_in)
    cmask = jnp.logical_and(jnp.right_shift(ri, 6) == jnp.right_shift(ci, 6), before)
    gc = _dot_mask_lhs(jnp.where(cmask, 1.0, 0.0).astype(BF16), gb)
    gc_t = gc.T
    incl2, strict2, eye2, lo_c = _pair_masks(c, rev)
    bdiag = _bdiag
    zeros = jnp.zeros((c, 2 * dk), F32)

    subs = list(range(nsub))
    inst = [(u, h) for u in subs for h in range(nh)]
    rs = {u: slice(u * c, (u + 1) * c) for u in subs}
    sls = [slice(h * dk, (h + 1) * dk) for h in range(nh)]
    cols = [16 + d * nh + h for h in range(nh)]
    g_col = {(u, h): gc[rs[u], cols[h]:cols[h] + 1] for u, h in inst}
    beta = {(u, h): gb[rs[u], cols[h] + 16:cols[h] + 17] for u, h in inst}
    g_end = {x: g_col[x][keep:keep + 1, :] for x in inst}
    e_g = {x: jnp.exp(g_col[x]) for x in inst}
    qs = {(u, h): q_ref[rs[u], sls[h]].astype(F32) for u, h in inst}
    ks = {(u, h): k_ref[rs[u], sls[h]].astype(F32) for u, h in inst}
    vs = {(u, h): v_ref[rs[u], sls[h]].astype(F32) for u, h in inst}

    pairs = [(u, a) for u in subs for a in range(0, nh, 2)]
    kq_e, kq_o, decay2 = [], [], []
    for u, a in pairs:
        e, o = (u, a), (u, a + 1)
        kk2 = jnp.concatenate([ks[e], ks[o]], axis=0)
        kq_e.append(_bdot_nt(jnp.concatenate([ks[e] * beta[e], qs[e]], axis=0), kk2))
        kq_o.append(_bdot_nt(jnp.concatenate([ks[o] * beta[o], qs[o]], axis=0), kk2))
        g_row2 = jnp.concatenate([gc_t[cols[a]:cols[a] + 1, rs[u]], gc_t[cols[a + 1]:cols[a + 1] + 1, rs[u]]], axis=1)
        decay2.append(jnp.exp(jnp.where(incl2, jnp.where(lo_c, g_col[e], g_col[o]) - g_row2, -jnp.inf)))
    l2 = [jnp.where(strict2, -(jnp.where(lo_c, e[:c], o[:c]) * dc), 0.0) for e, o, dc in zip(kq_e, kq_o, decay2)]
    a_qk2 = [jnp.where(lo_c, e[c:], o[c:]) * dc for e, o, dc in zip(kq_e, kq_o, decay2)]
    tinv2 = _neumann_inverse_packed(l2, eye2, c, lambda x2, y2: _dot_inv(x2, bdiag(y2)))
    rhs2 = []
    for u, a in pairs:
        e, o = (u, a), (u, a + 1)
        r_e = jnp.concatenate([vs[e] * beta[e], ks[e] * (beta[e] * e_g[e]), zeros], axis=1)
        r_o = jnp.concatenate([zeros, vs[o] * beta[o], ks[o] * (beta[o] * e_g[o])], axis=1)
        rhs2.append(jnp.concatenate([r_e, r_o], axis=0))
    uw2 = [_dot_inv(t, r) for t, r in zip(tinv2, rhs2)]
    op2 = []
    for aq, x in zip(a_qk2, uw2):
        x_bd = jnp.concatenate([jnp.concatenate([x[:, :2 * dk], zeros], axis=1),
                                jnp.concatenate([zeros, x[:, 2 * dk:]], axis=1)], axis=0)
        op2.append(_bdot(aq, x_bd))
    uw, o_part = {}, {}
    for (u, a), x, o2 in zip(pairs, uw2, op2):
        uw[(u, a)], uw[(u, a + 1)] = x[:, :2 * dk], x[:, 2 * dk:]
        o_part[(u, a)], o_part[(u, a + 1)] = o2[:, :2 * dk], o2[:, 2 * dk:]
    s_part = {x: _bdot_tn(ks[x] * jnp.exp(g_end[x] - g_col[x]), uw[x]) for x in inst}

    for u in (subs[::-1] if rev else subs):
        so = [_dot_state(jnp.concatenate([qs[(u, h)] * e_g[(u, h)] - o_part[(u, h)][:, dk:], s_part[(u, h)][:, dk:]], axis=0),
                         st_scr[h]) for h in range(nh)]
        for h in range(nh):
            x = (u, h)
            out = so[h][:c] + o_part[x][:, :dk]
            st_scr[h] = jnp.exp(g_end[x]) * st_scr[h] - so[h][c:] + s_part[x][:, :dk]
            if final:
                tot = out + ob_ref[rs[u], sls[h]]
                y = tot * lax.rsqrt(jnp.mean(tot * tot, axis=-1, keepdims=True) + EPS)
                o_ref[rs[u], sls[h]] = (y * ng_ref[0:1, sls[h]] * _silu(gate_ref[rs[u], sls[h]])).astype(o_ref.dtype)
            else:
                o_ref[rs[u], sls[h]] = out


def _gdn_call(p, q, k, v, gb, norm_g, other, rev, n_ctx):
    t = q.shape[0]
    c = GDN_SUB * GDN_CHUNK
    n, nc = t // c, n_ctx // c
    w = GDN_HEADS * GDN_DK
    final = other is not None
    cm = lambda s: _chunk_index(s, rev, nc, n)
    blk = pl.BlockSpec((c, w), lambda s: (cm(s), 0))
    in_specs = [blk, blk, blk, pl.BlockSpec((c, LANES), lambda s: (cm(s), 0))]
    args = [q, k, v, gb]
    if final:
        in_specs += [pl.BlockSpec((c, w), lambda s: (cm(s), 6144 // w)), pl.BlockSpec((1, w), lambda s: (0, 0)), blk]
        args += [p, norm_g.reshape(1, w), other]
    return pl.pallas_call(
        functools.partial(_gdn_kernel, rev=rev, final=final, c=GDN_CHUNK, nsub=GDN_SUB),
        grid=(n,),
        in_specs=in_specs,
        out_specs=blk,
        out_shape=jax.ShapeDtypeStruct((t, w), BF16 if final else F32),
        scratch_shapes=[pltpu.VMEM((GDN_HEADS, GDN_DK, GDN_DK), F32)],
        compiler_params=_cparams(1),
        name="gdn_bwd" if rev else "gdn_fwd_merge",
    )(*args)


def _pad_cols(w, width):
    return jnp.pad(w, ((0, 0), (0, width - w.shape[1])))


def _lane_row(pieces):
    row = jnp.zeros((LANES,), F32)
    for off, vec in pieces:
        row = row.at[off:off + vec.shape[0]].set(vec.astype(F32))
    return row


def _odd_weight(w):
    cols = [w[:, 0:3072], w[:, 3088:6160], w[:, 6160:7184], w[:, 3072:3088], w[:, 7184:7216]]
    return _pad_cols(jnp.concatenate(cols, axis=1), F_PAD).astype(BF16)


def _rwkv_params(e, rwkv_mu, rwkv_w0, rwkv_w_up, rwkv_a0, rwkv_a_up, rwkv_k_k, rwkv_k_a, rwkv_r_k,
                 rwkv_ln_g, rwkv_ln_b, rwkv_g_up):
    wdt = RWKV_W
    mu = rwkv_mu[e]
    zeros64 = jnp.zeros((RWKV_LORA,), F32)
    mu_small, vecs = [], []
    for d in range(2):
        m_wd = mu[d, 3 * wdt:3 * wdt + RWKV_LORA]
        m_ad = mu[d, 3 * wdt + RWKV_LORA:]
        lo = [m_wd, zeros64] if d == 0 else [zeros64, m_wd]
        la = [m_ad, zeros64] if d == 0 else [zeros64, m_ad]
        mu_small.append(jnp.stack([jnp.concatenate(lo), jnp.concatenate(la)]))
        vecs.append(jnp.stack([rwkv_w0[e, d], rwkv_a0[e, d], rwkv_k_k[e], rwkv_k_a[e], rwkv_r_k[e].reshape(wdt),
                               rwkv_ln_g[e], rwkv_ln_b[e], jnp.zeros((wdt,), F32)]))
    return {
        "mu": mu[:, None, :3 * wdt],
        "mu_small": jnp.stack(mu_small),
        "vecs": jnp.stack(vecs),
        "w_up": rwkv_w_up[e].astype(BF16),
        "a_up": rwkv_a_up[e].astype(BF16),
        "g_up": jnp.pad(rwkv_g_up[e], ((0, 2 * LANES - RWKV_GATE_LORA), (0, 0))).astype(BF16),
    }


def _even_mixer(p, e, n_ctx, ret_decay_logit, rwkv_prm):
    logit = ret_decay_logit[e].astype(F32)
    ret_b = _retention_call(p, logit, None, True, n_ctx)
    y_ret = _retention_call(p, logit, ret_b, False, n_ctx)
    rw_b = _rwkv_call(p, rwkv_prm, None, True, n_ctx)
    y_rwkv = _rwkv_call(p, rwkv_prm, rw_b, False, n_ctx)
    return y_ret, y_rwkv


def _odd_mixer(p, o, n_ctx, mlstm_gate_b, mlstm_norm_g, gdn_conv, gdn_a_log, gdn_dt_bias, gdn_norm_g):
    gate_b = mlstm_gate_b[o]
    nh = MLSTM_HEADS
    bias_i = _lane_row([(d * 2 * nh, gate_b[d, 0]) for d in range(2)])
    bias_f = _lane_row([(d * 2 * nh + nh, gate_b[d, 1]) for d in range(2)])
    bias_rows = jnp.stack([bias_i, bias_f])
    ml_b = _mlstm_call(p, bias_rows, mlstm_norm_g[o], None, True, n_ctx)
    y_ml = _mlstm_call(p, bias_rows, mlstm_norm_g[o], ml_b, False, n_ctx)

    gate_params = jnp.stack([_lane_row([(16, gdn_a_log[o].reshape(-1))]), _lane_row([(16, gdn_dt_bias[o].reshape(-1))])])
    q, k, v, gb = _gdn_prep_call(p, gdn_conv[o], gate_params, n_ctx)
    gd_b = _gdn_call(p, q, k, v, gb, gdn_norm_g[o], None, True, n_ctx)
    y_gd = _gdn_call(p, q, k, v, gb, gdn_norm_g[o], gd_b, False, n_ctx)
    return y_ml, y_gd


def kernel(x, c, ctx, c_ctx, ada_w, ada_b, norm1_g, norm2_g, mix_w_out, ffn_w_in, ffn_dw, ffn_dw_b, ffn_w_out, final_norm_g, ev_w_in, ret_decay_logit, rwkv_mu, rwkv_w0, rwkv_w_up, rwkv_a0, rwkv_a_up, rwkv_k_k, rwkv_k_a, rwkv_r_k, rwkv_ln_g, rwkv_ln_b, rwkv_g_up, od_w_in, mlstm_gate_b, mlstm_norm_g, gdn_conv, gdn_a_log, gdn_dt_bias, gdn_norm_g):
    assert x.shape[0] == 1 and ctx.shape[0] == 1
    n_ctx = ctx.shape[1]
    n_lat = x.shape[1]
    depth = ada_w.shape[0]
    cvecs = jnp.stack([c_ctx, c[0]], axis=1)
    mod = _ada_call(cvecs, ada_w, ada_b)
    xt, h1 = _assemble_call(ctx, x, norm1_g[0], mod)

    for l in range(depth):
        last = l == depth - 1
        half = D_MODEL // 2
        if l % 2 == 0:
            e = l // 2
            p = _proj_call(h1, ev_w_in[e], F_PAD)
            prm = _rwkv_params(e, rwkv_mu, rwkv_w0, rwkv_w_up, rwkv_a0, rwkv_a_up, rwkv_k_k, rwkv_k_a, rwkv_r_k,
                               rwkv_ln_g, rwkv_ln_b, rwkv_g_up)
            ya, yb = _even_mixer(p, e, n_ctx, ret_decay_logit, prm)
        else:
            o = l // 2
            p = _proj_call(h1, _odd_weight(od_w_in[o]), F_PAD)
            ya, yb = _odd_mixer(p, o, n_ctx, mlstm_gate_b, mlstm_norm_g, gdn_conv, gdn_a_log, gdn_dt_bias, gdn_norm_g)
        w_mix = mix_w_out[l].astype(BF16)
        xt, h2 = _outproj_call(ya, yb, w_mix[:half], w_mix[half:], xt, norm2_g[l], mod, l, n_ctx)
        gate, up, wo = _ffn_in_call(h2, ffn_w_in, ffn_w_out, l)
        dw9 = _pad_cols(ffn_dw[l].reshape(9, D_FF), FF_PAD)
        dwb = _pad_cols(ffn_dw_b[l].reshape(1, D_FF), FF_PAD)
        if last:
            xt = _ffn_out_call(gate, up, dw9, dwb, wo, xt, mod, l, final_norm_g, n_ctx, True)
        else:
            xt, h1 = _ffn_out_call(gate, up, dw9, dwb, wo, xt, mod, l, norm1_g[l + 1], n_ctx, False)
    return xt[n_ctx:][None]
```

```python
import functools
import math

import jax
import jax.numpy as jnp
from jax import lax
from jax.experimental import pallas as pl
from jax.experimental.pallas import tpu as pltpu

F32 = jnp.float32
BF16 = jnp.bfloat16

D_MODEL = 2048
DEPTH = 2
GRID_W = 64
EPS = 1e-6
GN_EPS = 64e-5
L2_EPS = 1e-12
D_FF = 5504

RET_HEADS = 8
RET_DK = 128
RWKV_HEADS = 16
RWKV_N = 64
RWKV_W = RWKV_HEADS * RWKV_N
RWKV_LORA = 64
RWKV_GATE_LORA = 160
MLSTM_HEADS = 4
MLSTM_DK = 128
MLSTM_DV = 256
GDN_HEADS = 8
GDN_DK = 128
GDN_QKV = 3072

LANES = 128
SUBLANES = 8
MXU_N = 256
VMEM_LIMIT = 56 * 1024 * 1024

F_PAD = 7680
FF_PAD = 5632
ROW_TILE = 768
WIDE_ROW_TILE = 1408
RET_CHUNK = 256
MLSTM_CHUNK = 256
GDN_CHUNK = 64
RWKV_CHUNK = 64
GDN_SUB = 4
RWKV_SUB = 2


def _wide_tile(t):
    return WIDE_ROW_TILE if t % WIDE_ROW_TILE == 0 else ROW_TILE


def _cparams(n_axes):
    return pltpu.CompilerParams(dimension_semantics=("arbitrary",) * n_axes, vmem_limit_bytes=VMEM_LIMIT)


def _bdot(a, b):
    return jnp.dot(a.astype(BF16), b.astype(BF16), preferred_element_type=F32)


def _bdot_nt(a, b):
    return lax.dot_general(a.astype(BF16), b.astype(BF16), (((1,), (1,)), ((), ())), preferred_element_type=F32)


def _bdot_tn(a, b):
    return lax.dot_general(a.astype(BF16), b.astype(BF16), (((0,), (0,)), ((), ())), preferred_element_type=F32)


def _split3(x):
    x1 = x.astype(BF16)
    r1 = x - x1.astype(F32)
    x2 = r1.astype(BF16)
    x3 = (r1 - x2.astype(F32)).astype(BF16)
    return x1, x2, x3


def _dot_mask_lhs(m_bf16, x):
    x1, x2, x3 = _split3(x)
    d = lambda t: jnp.dot(m_bf16, t, preferred_element_type=F32)
    return (d(x3) + d(x2)) + d(x1)


def _dot3(a, b):
    a1 = a.astype(BF16)
    a2 = (a - a1.astype(F32)).astype(BF16)
    b1 = b.astype(BF16)
    b2 = (b - b1.astype(F32)).astype(BF16)
    d = lambda u, v: jnp.dot(u, v, preferred_element_type=F32)
    return (d(a2, b1) + d(a1, b2)) + d(a1, b1)


def _sigmoid(x):
    return 1.0 / (1.0 + jnp.exp(-x))


def _silu(x):
    return x * _sigmoid(x)


def _softplus(x):
    return jnp.maximum(x, 0.0) + jnp.log1p(jnp.exp(-jnp.abs(x)))


def _log_sigmoid(x):
    return -_softplus(-x)


def _order_masks(c, rev):
    i = lax.broadcasted_iota(jnp.int32, (c, c), 0)
    j = lax.broadcasted_iota(jnp.int32, (c, c), 1)
    if rev:
        return j >= i, j > i
    return j <= i, j < i


def _neumann_inverse(n, c, dotf):
    i = lax.broadcasted_iota(jnp.int32, (c, c), 0)
    j = lax.broadcasted_iota(jnp.int32, (c, c), 1)
    x = jnp.where(i == j, 1.0, 0.0).astype(F32) + n
    p = dotf(n, n)
    for _ in range(int(math.log2(c)) - 2):
        r = dotf(jnp.concatenate([x, p], axis=0), p)
        x = x + r[:c]
        p = r[c:]
    return x + dotf(x, p)


def _neumann_inverse_multi(ns, c, dotf):
    i = lax.broadcasted_iota(jnp.int32, (c, c), 0)
    j = lax.broadcasted_iota(jnp.int32, (c, c), 1)
    eye = jnp.where(i == j, 1.0, 0.0).astype(F32)
    xs = [eye + n for n in ns]
    ps = [dotf(n, n) for n in ns]
    for _ in range(int(math.log2(c)) - 2):
        rs = [dotf(jnp.concatenate([x, p], axis=0), p) for x, p in zip(xs, ps)]
        xs = [x + r[:c] for x, r in zip(xs, rs)]
        ps = [r[c:] for r in rs]
    return [x + dotf(x, p) for x, p in zip(xs, ps)]


def _pair_masks(c, rev):
    assert 2 * c == LANES and c == 64
    row = lax.broadcasted_iota(jnp.int32, (c, 2 * c), 0)
    lane = lax.broadcasted_iota(jnp.int32, (c, 2 * c), 1)
    src = jnp.bitwise_and(lane, c - 1)
    incl2 = (src >= row) if rev else (src <= row)
    strict2 = (src > row) if rev else (src < row)
    return incl2, strict2, src == row, lane < c


def _bdiag(x2, anti=False):
    n2 = x2.shape[1]
    blk_r = jnp.right_shift(lax.broadcasted_iota(jnp.int32, (n2, n2), 0), 6)
    blk_l = jnp.right_shift(lax.broadcasted_iota(jnp.int32, (n2, n2), 1), 6)
    stacked = jnp.concatenate([x2, x2], axis=0)
    if anti:
        return jnp.where(blk_r == blk_l, 0.0, stacked)
    return jnp.where(blk_r == blk_l, stacked, 0.0)


def _neumann_inverse_packed(ns, eye2, c, pdot):
    xs = [jnp.where(eye2, 1.0, 0.0) + n for n in ns]
    ps = [pdot(n, n) for n in ns]
    for _ in range(int(math.log2(c)) - 2):
        rs = [pdot(jnp.concatenate([x, p], axis=0), p) for x, p in zip(xs, ps)]
        xs = [x + r[:c] for x, r in zip(xs, rs)]
        ps = [r[c:] for r in rs]
    return [x + pdot(x, p) for x, p in zip(xs, ps)]


def _dot_inv(a, b):
    return _dot3(a, b)


def _dot_state(a, b):
    return _bdot(a, b)


def _chunk_index(s, rev, n_ctx_chunks, n_chunks):
    if not rev:
        return s
    return jnp.where(s < n_ctx_chunks, n_ctx_chunks - 1 - s, n_chunks + n_ctx_chunks - 1 - s)


def _seg_sum(x, seg):
    c, w = x.shape
    if seg == LANES:
        parts = [jnp.broadcast_to(jnp.sum(x[:, b * LANES:(b + 1) * LANES], axis=-1, keepdims=True), (c, LANES))
                 for b in range(w // LANES)]
        return jnp.concatenate(parts, axis=-1)
    assert seg * 2 == LANES
    lane = lax.broadcasted_iota(jnp.int32, (c, LANES), 1)
    low = lane < seg
    parts = []
    for b in range(w // LANES):
        xb = x[:, b * LANES:(b + 1) * LANES]
        s_lo = jnp.sum(jnp.where(low, xb, 0.0), axis=-1, keepdims=True)
        s_hi = jnp.sum(jnp.where(low, 0.0, xb), axis=-1, keepdims=True)
        parts.append(jnp.where(low, s_lo, s_hi))
    return jnp.concatenate(parts, axis=-1)


def _ada_kernel(c_ref, w_ref, b_ref, o_ref):
    cv = c_ref[...]
    s = _silu(cv)
    w = w_ref[0]
    bias = b_ref[0]
    r0 = jnp.sum(w * s[:, 0:1], axis=0, keepdims=True) + bias
    r1 = jnp.sum(w * s[:, 1:2], axis=0, keepdims=True) + bias
    row = lax.broadcasted_iota(jnp.int32, (SUBLANES, w.shape[1]), 0)
    o_ref[0] = jnp.where(row == 0, r0, jnp.where(row == 1, r1, 0.0))


def _ada_call(cvecs, ada_w, ada_b):
    depth, d, n = ada_w.shape
    tn = 1024
    return pl.pallas_call(
        _ada_kernel,
        grid=(depth, n // tn),
        in_specs=[pl.BlockSpec((d, 2), lambda l, j: (0, 0)),
                  pl.BlockSpec((1, d, tn), lambda l, j: (l, 0, j)),
                  pl.BlockSpec((1, 1, tn), lambda l, j: (l, 0, j))],
        out_specs=pl.BlockSpec((1, SUBLANES, tn), lambda l, j: (l, 0, j)),
        out_shape=jax.ShapeDtypeStruct((depth, SUBLANES, n), F32),
        compiler_params=_cparams(2),
        name="ada_modulation",
    )(cvecs, ada_w, ada_b.reshape(depth, 1, n))


def _mod_rows(mod, k, rows, n_ctx):
    d = D_MODEL
    vc = mod[0:1, k * d:(k + 1) * d]
    vl = mod[1:2, k * d:(k + 1) * d]
    return jnp.where(rows < n_ctx, vc, vl)


def _norm_mod(x, g, shift, scale):
    y = x * lax.rsqrt(jnp.mean(x * x, axis=-1, keepdims=True) + EPS) * g
    return y * (1.0 + scale) + shift


def _assemble_kernel(ctx_ref, x_ref, g_ref, mod_ref, xt_ref, h_ref):
    i = pl.program_id(0)
    d = D_MODEL
    mod = mod_ref[0]

    def emit(src, row):
        xt_ref[...] = src
        h = _norm_mod(src, g_ref[...], mod[row:row + 1, 0:d], mod[row:row + 1, d:2 * d])
        h_ref[...] = h.astype(BF16)

    @pl.when(i == 0)
    def _():
        emit(ctx_ref[0], 0)

    @pl.when(i > 0)
    def _():
        emit(x_ref[0], 1)


def _assemble_call(ctx, x, g, mod):
    n_ctx, d = ctx.shape[1], ctx.shape[2]
    t = n_ctx + x.shape[1]
    assert x.shape[1] % n_ctx == 0
    spec_o = pl.BlockSpec((n_ctx, d), lambda i: (i, 0))
    return pl.pallas_call(
        _assemble_kernel,
        grid=(t // n_ctx,),
        in_specs=[pl.BlockSpec((1, n_ctx, d), lambda i: (0, 0, 0)),
                  pl.BlockSpec((1, n_ctx, d), lambda i: (0, jnp.maximum(i - 1, 0), 0)),
                  pl.BlockSpec((1, d), lambda i: (0, 0)),
                  pl.BlockSpec((1, SUBLANES, 6 * d), lambda i: (0, 0, 0))],
        out_specs=[spec_o, spec_o],
        out_shape=[jax.ShapeDtypeStruct((t, d), F32), jax.ShapeDtypeStruct((t, d), BF16)],
        compiler_params=_cparams(1),
        name="assemble_norm1",
    )(ctx, x, g.reshape(1, d), mod)


def _proj_kernel(h_ref, w_ref, o_ref, w_scr, *, tn, n_valid):
    @pl.when(pl.program_id(1) == 0)
    def _():
        w = w_ref[...]
        col = pl.program_id(0) * tn + lax.broadcasted_iota(jnp.int32, w.shape, 1)
        w_scr[...] = jnp.where(col < n_valid, w, 0).astype(BF16)

    o_ref[...] = jnp.dot(h_ref[...], w_scr[...], preferred_element_type=F32)


def _proj_call(h, w, f_pad):
    t, d = h.shape
    tm, tn = _wide_tile(t), 768
    return pl.pallas_call(
        functools.partial(_proj_kernel, tn=tn, n_valid=w.shape[1]),
        grid=(f_pad // tn, t // tm),
        in_specs=[pl.BlockSpec((tm, d), lambda j, i: (i, 0)),
                  pl.BlockSpec((d, tn), lambda j, i: (0, j))],
        out_specs=pl.BlockSpec((tm, tn), lambda j, i: (i, j)),
        out_shape=jax.ShapeDtypeStruct((t, f_pad), F32),
        scratch_shapes=[pltpu.VMEM((d, tn), BF16)],
        compiler_params=_cparams(2),
        name="in_proj",
    )(h, w)


ODD_GATES_AT = 3072
ODD_GATES_W = 16
ODD_SMALL_AT = 7168
ODD_WIDTH = 7216


def _proj_odd_kernel(h_ref, w_ref, wn_ref, wg_ref, o_ref, w_scr, *, tn):
    j = pl.program_id(0)
    first_shift = ODD_GATES_AT // tn
    last = ODD_SMALL_AT // tn
    assert ODD_GATES_AT % tn == 0 and (last + 1) * tn == F_PAD

    @pl.when(pl.program_id(1) == 0)
    def _():
        w = w_ref[0]
        wide = jnp.concatenate([w, wn_ref[0]], axis=1)
        shifted = pltpu.roll(wide, wide.shape[1] - ODD_GATES_W, axis=1)[:, :tn]
        lane = lax.broadcasted_iota(jnp.int32, w.shape, 1)

        @pl.when(j < first_shift)
        def _():
            w_scr[...] = w.astype(BF16)

        @pl.when(jnp.logical_and(j >= first_shift, j < last))
        def _():
            w_scr[...] = shifted.astype(BF16)

        @pl.when(j == last)
        def _():
            q0 = ODD_SMALL_AT - last * tn
            gates = jnp.concatenate([wg_ref[0]] * (tn // LANES), axis=1)
            small = jnp.where(lane < q0 + ODD_GATES_W, gates, w)
            tile = jnp.where(lane < q0, shifted, jnp.where(lane < q0 + (ODD_WIDTH - ODD_SMALL_AT), small, 0.0))
            w_scr[...] = tile.astype(BF16)

    o_ref[...] = jnp.dot(h_ref[...], w_scr[...], preferred_element_type=F32)


def _proj_odd_call(h, od_w_in, o):
    t, d = h.shape
    tm, tn = ROW_TILE, 768
    assert od_w_in.shape[2] == ODD_WIDTH and ODD_SMALL_AT % LANES == 0 and ODD_GATES_AT % LANES == 0
    per = tn // LANES
    last_lane_tile = (ODD_WIDTH - 1) // LANES
    return pl.pallas_call(
        functools.partial(_proj_odd_kernel, tn=tn),
        grid=(F_PAD // tn, t // tm),
        in_specs=[pl.BlockSpec((tm, d), lambda j, i: (i, 0)),
                  pl.BlockSpec((1, d, tn), lambda j, i: (o, 0, j)),
                  pl.BlockSpec((1, d, LANES), lambda j, i: (o, 0, jnp.minimum(per * (j + 1), last_lane_tile))),
                  pl.BlockSpec((1, d, LANES), lambda j, i: (o, 0, ODD_GATES_AT // LANES))],
        out_specs=pl.BlockSpec((tm, tn), lambda j, i: (i, j)),
        out_shape=jax.ShapeDtypeStruct((t, F_PAD), F32),
        scratch_shapes=[pltpu.VMEM((d, tn), BF16)],
        compiler_params=_cparams(2),
        name="in_proj_odd",
    )(h, od_w_in, od_w_in, od_w_in)


def _outproj_kernel(ya_ref, yb_ref, wa_ref, wb_ref, x_ref, g_ref, mod_ref, xo_ref, h_ref, *, tm, n_ctx):
    i = pl.program_id(0)
    rows = i * tm + lax.broadcasted_iota(jnp.int32, (tm, 1), 0)
    mod = mod_ref[0]
    acc = jnp.dot(ya_ref[...], wa_ref[...], preferred_element_type=F32)
    acc = acc + jnp.dot(yb_ref[...], wb_ref[...], preferred_element_type=F32)
    xn = x_ref[...] + _mod_rows(mod, 2, rows, n_ctx) * acc
    xo_ref[...] = xn
    h = _norm_mod(xn, g_ref[...], _mod_rows(mod, 3, rows, n_ctx), _mod_rows(mod, 4, rows, n_ctx))
    h_ref[...] = h.astype(BF16)


def _outproj_call(ya, yb, wa, wb, x, g2, mod, layer, n_ctx):
    t, d = x.shape
    half = ya.shape[1]
    tm = 384
    return pl.pallas_call(
        functools.partial(_outproj_kernel, tm=tm, n_ctx=n_ctx),
        grid=(t // tm,),
        in_specs=[pl.BlockSpec((tm, half), lambda i: (i, 0)),
                  pl.BlockSpec((tm, half), lambda i: (i, 0)),
                  pl.BlockSpec((half, d), lambda i: (0, 0)),
                  pl.BlockSpec((half, d), lambda i: (0, 0)),
                  pl.BlockSpec((tm, d), lambda i: (i, 0)),
                  pl.BlockSpec((1, d), lambda i: (0, 0)),
                  pl.BlockSpec((1, SUBLANES, 6 * d), lambda i: (layer, 0, 0))],
        out_specs=[pl.BlockSpec((tm, d), lambda i: (i, 0)),
                   pl.BlockSpec((tm, d), lambda i: (i, 0))],
        out_shape=[jax.ShapeDtypeStruct((t, d), F32), jax.ShapeDtypeStruct((t, d), BF16)],
        compiler_params=_cparams(1),
        name="mix_out_proj_norm2",
    )(ya, yb, wa, wb, x, g2.reshape(1, d), mod)


def _ffn_in_kernel(h_ref, wg_ref, wu0_ref, wu1_ref, wu2_ref, wu3_ref, wo_ref, g_ref, u_ref, wob_ref, w_scr, *, tn):
    @pl.when(pl.program_id(1) == 0)
    def _():
        j = pl.program_id(0)
        wu = jnp.concatenate([wu0_ref[0], wu1_ref[0], wu2_ref[0], wu3_ref[0]], axis=1)
        col = j * tn + lax.broadcasted_iota(jnp.int32, wu.shape, 1)
        w_scr[0] = jnp.where(col < D_FF, wg_ref[0], 0.0).astype(BF16)
        w_scr[1] = jnp.where(col < D_FF, wu, 0.0).astype(BF16)
        wo = wo_ref[0]
        row = j * tn + lax.broadcasted_iota(jnp.int32, wo.shape, 0)
        wob_ref[...] = jnp.where(row < D_FF, 0.5 * wo, 0.0).astype(BF16)

    h = h_ref[...]
    g_ref[...] = jnp.dot(h, w_scr[0], preferred_element_type=F32).astype(BF16)
    u_ref[...] = jnp.dot(h, w_scr[1], preferred_element_type=F32).astype(BF16)


def _ffn_in_call(h, ffn_w_in, ffn_w_out, layer):
    t, d = h.shape
    tm, tn = ROW_TILE, 512
    q = tn // LANES
    assert D_FF % LANES == 0 and q == 4
    up0 = D_FF // LANES
    last = 2 * D_FF // LANES - 1
    spec_o = pl.BlockSpec((tm, tn), lambda j, i: (i, j))
    up_specs = [pl.BlockSpec((1, d, LANES), lambda j, i, r=r: (layer, 0, jnp.minimum(up0 + q * j + r, last)))
                for r in range(q)]
    return pl.pallas_call(
        functools.partial(_ffn_in_kernel, tn=tn),
        grid=(FF_PAD // tn, t // tm),
        in_specs=[pl.BlockSpec((tm, d), lambda j, i: (i, 0)),
                  pl.BlockSpec((1, d, tn), lambda j, i: (layer, 0, j))] + up_specs
                 + [pl.BlockSpec((1, tn, d), lambda j, i: (layer, j, 0))],
        out_specs=[spec_o, spec_o, pl.BlockSpec((tn, d), lambda j, i: (j, 0))],
        out_shape=[jax.ShapeDtypeStruct((t, FF_PAD), BF16)] * 2 + [jax.ShapeDtypeStruct((FF_PAD, d), BF16)],
        scratch_shapes=[pltpu.VMEM((2, d, tn), BF16)],
        compiler_params=_cparams(2),
        name="ffn_in_proj",
    )(h, ffn_w_in, ffn_w_in, ffn_w_in, ffn_w_in, ffn_w_in, ffn_w_out)


def _glu_act(conv, bias, up):
    gate = conv + bias
    return (gate * (1.0 + lax.erf(gate * (2.0 ** -0.5))) * up.astype(F32)).astype(BF16)


def _ffn_out_kernel(*refs, tm, tk, n_ctx, t_total, final_norm):
    if final_norm:
        (gm_ref, gp_ref, gn_ref, u_ref, dw_ref, db_ref, wo_ref, x_ref, mod_ref, fg_ref,
         o_ref, acc_scr, act_a, act_b) = refs
    else:
        (gm_ref, gp_ref, gn_ref, u_ref, dw_ref, db_ref, wo_ref, x_ref, mod_ref, fg_ref, nmod_ref,
         o_ref, hn_ref, acc_scr, act_a, act_b) = refs
    i = pl.program_id(0)
    k = pl.program_id(1)
    nk = pl.num_programs(1) - 1
    w = GRID_W
    nrow = tm // w
    blk0 = i * nrow
    nb_ctx, nb_tot = n_ctx // w, t_total // w
    sub = lax.broadcasted_iota(jnp.int32, (SUBLANES, LANES), 0)

    def neighbours(g, n):
        gl = pltpu.roll(g, 1, axis=0)
        gr = pltpu.roll(g, n - 1, axis=0)
        gl = jnp.concatenate([jnp.where(sub == 0, 0.0, gl[:SUBLANES]), gl[SUBLANES:]], axis=0)
        gr = jnp.concatenate([gr[:n - SUBLANES], jnp.where(sub == SUBLANES - 1, 0.0, gr[n - SUBLANES:])], axis=0)
        return gl, gr

    def step(dst, src):
        d_out = acc_scr.shape[1]
        row_halves = 2
        hm = tm // row_halves
        n_mm = row_halves * d_out // MXU_N
        n_pieces = (tk // LANES) * (nrow + 2)

        def matmul_chunk(n):
            cs = slice((n // row_halves) * MXU_N, (n // row_halves + 1) * MXU_N)
            rs = slice((n % row_halves) * hm, (n % row_halves + 1) * hm)
            acc_scr[rs, cs] += jnp.dot(src[rs, :], wo_ref[:, cs], preferred_element_type=F32)

        mm_at = {(n * n_pieces) // n_mm: n for n in range(n_mm)}
        assert len(mm_at) == n_mm
        piece = 0
        for lb in range(tk // LANES):
            ls = slice(lb * LANES, (lb + 1) * LANES)
            dwv = dw_ref[:, ls]
            bias = db_ref[:, ls]
            part = [None] * nrow
            for r in range(-1, nrow + 1):
                if piece in mm_at:
                    matmul_chunk(mm_at[piece])
                piece += 1
                if r == -1:
                    g = gp_ref[:, ls]
                elif r == nrow:
                    g = gn_ref[:, ls]
                else:
                    g = gm_ref[r * w:(r + 1) * w, ls]
                g = g.astype(F32)
                gl, gr = neighbours(g, w)
                for kh, ro in ((0, r + 1), (1, r), (2, r - 1)):
                    if not 0 <= ro < nrow:
                        continue
                    wv = dwv[3 * kh:3 * kh + 3]
                    if kh == 0:
                        wv = wv * jnp.where(blk0 + ro >= nb_ctx + 1, 1.0, 0.0)
                    if kh == 2:
                        ok = jnp.logical_and(blk0 + ro >= nb_ctx, blk0 + ro < nb_tot - 1)
                        wv = wv * jnp.where(ok, 1.0, 0.0)
                    c = gl * wv[0:1] + g * wv[1:2] + gr * wv[2:3]
                    part[ro] = c if part[ro] is None else part[ro] + c
                ro = r - 1
                if 0 <= ro < nrow:
                    rs = slice(ro * w, (ro + 1) * w)
                    dst[rs, ls] = _glu_act(part[ro], bias, u_ref[rs, ls])
                    part[ro] = None

        @pl.when(i == 0)
        def _():
            for lb in range(tk // LANES):
                ls = slice(lb * LANES, (lb + 1) * LANES)
                g = gm_ref[0:n_ctx, ls].astype(F32)
                gl, gr = neighbours(g, n_ctx)
                conv = gl * dw_ref[3:4, ls] + g * dw_ref[4:5, ls] + gr * dw_ref[5:6, ls]
                dst[0:n_ctx, ls] = _glu_act(conv, db_ref[:, ls], u_ref[0:n_ctx, ls])

    @pl.when(k == 0)
    def _():
        acc_scr[...] = jnp.zeros_like(acc_scr)
        act_b[...] = jnp.zeros_like(act_b)

    @pl.when(k % 2 == 0)
    def _():
        step(act_a, act_b)

    @pl.when(k % 2 == 1)
    def _():
        step(act_b, act_a)

    @pl.when(k == nk)
    def _():
        rb = 2 * LANES
        for r0 in range(0, tm, rb):
            rs = slice(r0, r0 + rb)
            rws = i * tm + r0 + lax.broadcasted_iota(jnp.int32, (rb, 1), 0)
            xn = x_ref[rs, :] + _mod_rows(mod_ref[0], 5, rws, n_ctx) * acc_scr[rs, :]
            if final_norm:
                xn = xn * lax.rsqrt(jnp.mean(xn * xn, axis=-1, keepdims=True) + EPS) * fg_ref[...]
            else:
                nmod = nmod_ref[0]
                hn = _norm_mod(xn, fg_ref[...], _mod_rows(nmod, 0, rws, n_ctx), _mod_rows(nmod, 1, rws, n_ctx))
                hn_ref[rs, :] = hn.astype(BF16)
            o_ref[rs, :] = xn


def _ffn_out_call(gate, up, dw9, dwb, wo, x, mod, layer, norm_g, n_ctx, final_norm):
    t, d = x.shape
    fp = gate.shape[1]
    tm, tk, w = ROW_TILE, 512, GRID_W
    assert n_ctx <= tm and n_ctx % w == 0 and tm % (2 * LANES) == 0
    rpt = tm // w
    n_rows = t // w
    spec_x = pl.BlockSpec((tm, d), lambda i, k: (i, 0))
    extra_in, extra_args = [], []
    out_specs, out_shape = spec_x, jax.ShapeDtypeStruct((t, d), F32)
    if not final_norm:
        extra_in = [pl.BlockSpec((1, SUBLANES, 6 * d), lambda i, k: (layer + 1, 0, 0))]
        extra_args = [mod]
        out_specs = [spec_x, spec_x]
        out_shape = [out_shape, jax.ShapeDtypeStruct((t, d), BF16)]
    nk = fp // tk
    assert nk % 2 == 1
    kc = lambda k: jnp.minimum(k, nk - 1)
    return pl.pallas_call(
        functools.partial(_ffn_out_kernel, tm=tm, tk=tk, n_ctx=n_ctx, t_total=t, final_norm=final_norm),
        grid=(t // tm, nk + 1),
        in_specs=[pl.BlockSpec((tm, tk), lambda i, k: (i, kc(k))),
                  pl.BlockSpec((w, tk), lambda i, k: (jnp.maximum(i * rpt - 1, 0), kc(k))),
                  pl.BlockSpec((w, tk), lambda i, k: (jnp.minimum((i + 1) * rpt, n_rows - 1), kc(k))),
                  pl.BlockSpec((tm, tk), lambda i, k: (i, kc(k))),
                  pl.BlockSpec((9, tk), lambda i, k: (0, kc(k))),
                  pl.BlockSpec((1, tk), lambda i, k: (0, kc(k))),
                  pl.BlockSpec((tk, d), lambda i, k: (jnp.maximum(k - 1, 0), 0)),
                  spec_x,
                  pl.BlockSpec((1, SUBLANES, 6 * d), lambda i, k: (layer, 0, 0)),
                  pl.BlockSpec((1, d), lambda i, k: (0, 0))] + extra_in,
        out_specs=out_specs,
        out_shape=out_shape,
        scratch_shapes=[pltpu.VMEM((tm, d), F32), pltpu.VMEM((tm, tk), BF16), pltpu.VMEM((tm, tk), BF16)],
        compiler_params=_cparams(2),
        name="ffn_conv_glu_out",
    )(gate, gate, gate, up, dw9, dwb, wo, x, mod, norm_g.reshape(1, d), *extra_args)


def _retention_kernel(*refs, rev, final, c):
    if final:
        q_ref, k_ref, v_ref, lg_ref, gate_ref, ob_ref, o_ref, r_scr, di_scr, dq_scr, dk_scr, dc_scr = refs
    else:
        q_ref, k_ref, v_ref, lg_ref, o_ref, r_scr, di_scr, dq_scr, dk_scr, dc_scr = refs
    d = 1 if rev else 0
    dk = RET_DK

    @pl.when(pl.program_id(0) == 0)
    def _():
        r_scr[...] = jnp.zeros_like(r_scr)
        i = lax.broadcasted_iota(jnp.int32, (c, c), 0)
        j = lax.broadcasted_iota(jnp.int32, (c, c), 1)
        diff = ((j - i) if rev else (i - j)).astype(F32)
        row = lax.broadcasted_iota(jnp.int32, (c, dk), 0)
        pos = ((c - 1 - row) if rev else row).astype(F32)
        for h in range(RET_HEADS):
            lg = _log_sigmoid(lg_ref[d:d + 1, h:h + 1])
            di_scr[h] = jnp.where(diff >= 0, jnp.exp(lg * jnp.maximum(diff, 0.0)), 0.0)
            dq_scr[h] = jnp.exp(lg * (pos + 1.0))
            dk_scr[h] = jnp.exp(lg * (c - 1.0 - pos)) * (dk ** -0.5)
            dc_scr[h] = jnp.exp(jnp.broadcast_to(lg, (1, dk)) * c)

    for h in range(RET_HEADS):
        sl = slice(h * dk, (h + 1) * dk)
        qh = q_ref[:, sl]
        kh = k_ref[:, sl]
        vh = v_ref[:, sl].astype(BF16)
        r_state = r_scr[h]
        scores = _bdot_nt(qh, kh) * (di_scr[h] * (dk ** -0.5))
        out = _bdot(scores, vh) + _bdot(qh * dq_scr[h], r_state)
        r_scr[h] = dc_scr[h] * r_state + _bdot_tn(kh * dk_scr[h], vh)
        if final:
            tot = out + ob_ref[:, sl]
            y = tot * lax.rsqrt(jnp.mean(tot * tot, axis=-1, keepdims=True) + EPS) * _silu(gate_ref[:, sl])
            o_ref[:, sl] = y.astype(o_ref.dtype)
        else:
            o_ref[:, sl] = out


def _retention_call(p, logit, other, rev, n_ctx):
    t = p.shape[0]
    c = RET_CHUNK
    n, nc = t // c, n_ctx // c
    width = RET_HEADS * RET_DK
    final = other is not None
    cm = lambda s: _chunk_index(s, rev, nc, n)
    in_specs = [pl.BlockSpec((c, width), lambda s: (cm(s), 0)),
                pl.BlockSpec((c, width), lambda s: (cm(s), 1)),
                pl.BlockSpec((c, width), lambda s: (cm(s), 2)),
                pl.BlockSpec((2, RET_HEADS), lambda s: (0, 0))]
    args = [p, p, p, logit]
    if final:
        in_specs += [pl.BlockSpec((c, width), lambda s: (cm(s), 3)), pl.BlockSpec((c, width), lambda s: (cm(s), 0))]
        args += [p, other]
    return pl.pallas_call(
        functools.partial(_retention_kernel, rev=rev, final=final, c=c),
        grid=(n,),
        in_specs=in_specs,
        out_specs=pl.BlockSpec((c, width), lambda s: (cm(s), 0)),
        out_shape=jax.ShapeDtypeStruct((t, width), BF16 if final else F32),
        scratch_shapes=[pltpu.VMEM((RET_HEADS, RET_DK, RET_DK), F32),
                        pltpu.VMEM((RET_HEADS, c, c), F32),
                        pltpu.VMEM((RET_HEADS, c, RET_DK), F32),
                        pltpu.VMEM((RET_HEADS, c, RET_DK), F32),
                        pltpu.VMEM((RET_HEADS, 1, RET_DK), F32)],
        compiler_params=_cparams(1),
        name="retention_bwd" if rev else "retention_fwd_merge",
    )(*args)


def _rwkv_kernel(*refs, rev, final, c, nsub, nc):
    if final:
        (r_ref, k_ref, v_ref, wd_ref, ad_ref, mu_ref, mus_ref, vec_ref, wup_ref, aup_ref,
         gl_ref, gup_ref, yb_ref, bb_ref, o_ref, st_scr, carry_scr) = refs
    else:
        (r_ref, k_ref, v_ref, wd_ref, ad_ref, mu_ref, mus_ref, vec_ref, wup_ref, aup_ref,
         y_ref, bonus_ref, st_scr, carry_scr) = refs
    d = 1 if rev else 0
    s = pl.program_id(0)
    n = RWKV_N
    wdt = RWKV_W

    @pl.when(s == 0)
    def _():
        st_scr[...] = jnp.zeros_like(st_scr)

    @pl.when(jnp.logical_or(s == 0, s == nc))
    def _():
        carry_scr[...] = jnp.zeros_like(carry_scr)

    rows = nsub * c
    row = lax.broadcasted_iota(jnp.int32, (rows, 1), 0)
    edge = (row == rows - 1) if rev else (row == 0)
    last = rows - 1 if not rev else 0
    keep = c - 1 if not rev else 0

    def shifted(x, lo):
        width = x.shape[1]
        prev = pltpu.roll(x, (rows - 1) if rev else 1, axis=0)
        prev = jnp.where(edge, carry_scr[0:1, lo:lo + width], prev)
        carry_scr[0:1, lo:lo + width] = x[last:last + 1, :]
        return prev

    def mix(x, lo, mu):
        prev = shifted(x, lo)
        return x + (prev - x) * mu

    r = mix(r_ref[...], 0, mu_ref[0:1, 0:wdt])
    k = mix(k_ref[...], wdt, mu_ref[0:1, wdt:2 * wdt])
    v = mix(v_ref[...], 2 * wdt, mu_ref[0:1, 2 * wdt:3 * wdt])
    wd = mix(wd_ref[...], 3 * wdt, mus_ref[0:1, :])[:, d * RWKV_LORA:(d + 1) * RWKV_LORA]
    ad = mix(ad_ref[...], 3 * wdt + LANES, mus_ref[1:2, :])[:, d * RWKV_LORA:(d + 1) * RWKV_LORA]

    w0, a0 = vec_ref[0:1, :], vec_ref[1:2, :]
    k_k, k_a, r_k = vec_ref[2:3, :], vec_ref[3:4, :], vec_ref[4:5, :]
    w_log = -_softplus(-(w0 + _bdot(jnp.tanh(wd), wup_ref[...]))) - 0.5
    lw = -jnp.exp(w_log)
    a = _sigmoid(a0 + _bdot(ad, aup_ref[...]))
    kk = k * k_k
    kk = kk * lax.rsqrt(_seg_sum(kk * kk, n) + L2_EPS)
    k = k * (1.0 + (a - 1.0) * k_a)
    bonus = _seg_sum(r * k * r_k, n) * v
    b = kk * a

    ri = lax.broadcasted_iota(jnp.int32, (rows, rows), 0)
    ci = lax.broadcasted_iota(jnp.int32, (rows, rows), 1)
    ri_in, ci_in = jnp.bitwise_and(ri, c - 1), jnp.bitwise_and(ci, c - 1)
    before = (ci_in >= ri_in) if rev else (ci_in <= ri_in)
    cmask = jnp.logical_and(jnp.right_shift(ri, 6) == jnp.right_shift(ci, 6), before)
    cum = _dot_mask_lhs(jnp.where(cmask, 1.0, 0.0).astype(BF16), lw)
    cum_end = jnp.concatenate([jnp.broadcast_to(cum[u * c + keep:u * c + keep + 1, :], (c, wdt)) for u in range(nsub)],
                              axis=0)
    e_pos = jnp.exp(cum)
    e_neg = jnp.exp(-cum)
    r_t = r * e_pos
    a_t = -kk * jnp.exp(cum - lw)
    k_t = k * e_neg
    b_t = b * e_neg
    w_end = jnp.exp(cum_end)
    kw = k_t * w_end
    bw = b_t * w_end
    assert c == n
    pw = 2 * n
    incl2, strict2, eye2, lo_c = _pair_masks(c, rev)
    lo_2c = lax.broadcasted_iota(jnp.int32, (2 * c, pw), 1) < n
    bdiag = _bdiag

    adiag = functools.partial(_bdiag, anti=True)

    def pdot_inv(a2, b2):
        return _dot_inv(a2, bdiag(b2))

    n_pairs = RWKV_HEADS // 2
    subs = list(range(nsub))
    inst = [(u, p) for u in subs for p in range(n_pairs)]
    sl = {(u, p): (slice(u * c, (u + 1) * c), slice(p * pw, (p + 1) * pw)) for u, p in inst}
    ar = [jnp.concatenate([a_t[sl[x]], r_t[sl[x]]], axis=0) for x in inst]
    s_e = [_bdot_nt(jnp.where(lo_2c, y, 0.0), jnp.concatenate([b_t[sl[x]], k_t[sl[x]]], axis=0))
           for y, x in zip(ar, inst)]
    s_o = [_bdot_nt(jnp.where(lo_2c, 0.0, y), jnp.concatenate([k_t[sl[x]], b_t[sl[x]]], axis=0))
           for y, x in zip(ar, inst)]
    a_ab = [jnp.where(strict2, jnp.where(lo_c, e[:c], o[:c]), 0.0) for e, o in zip(s_e, s_o)]
    a_rb = [jnp.where(incl2, jnp.where(lo_c, e[c:], o[c:]), 0.0) for e, o in zip(s_e, s_o)]
    a_kk = [jnp.concatenate([jnp.where(strict2, jnp.where(lo_c, o[:c], e[:c]), 0.0),
                             jnp.where(incl2, jnp.where(lo_c, o[c:], e[c:]), 0.0)], axis=0)
            for e, o in zip(s_e, s_o)]
    tms = _neumann_inverse_packed(a_ab, eye2, c, pdot_inv)
    av = [_bdot(a, adiag(v[sl[x]])) for a, x in zip(a_kk, inst)]
    pp = [_dot_inv(t, jnp.concatenate([bdiag(a_t[sl[x]]), bdiag(y[:c])], axis=1))
          for t, x, y in zip(tms, inst, av)]
    rq_y0 = [_bdot(a, jnp.concatenate([bdiag(y[:, :pw]), bdiag(y[:, pw:])], axis=1)) for a, y in zip(a_rb, pp)]
    m_g = [_bdot_tn(bw[sl[x]], y) for x, y in zip(inst, pp)]
    kv = [_bdot_tn(kw[sl[x]], v[sl[x]]) for x in inst]
    idx = {x: i for i, x in enumerate(inst)}
    y_rows = {}
    for u in (subs[::-1] if rev else subs):
        so = []
        for p in range(n_pairs):
            i = idx[(u, p)]
            rq = r_t[sl[(u, p)]] + rq_y0[i][:, :pw]
            w_row = w_end[u * c:u * c + 1, p * pw:(p + 1) * pw]
            m = (jnp.where(lo_c, m_g[i][:n, :pw], m_g[i][n:, :pw])
                 + jnp.where(eye2, jnp.broadcast_to(w_row, (n, pw)), 0.0))
            so.append(_dot_state(jnp.concatenate([rq, m], axis=0), bdiag(st_scr[p])))
        ys = []
        for p in range(n_pairs):
            i = idx[(u, p)]
            g = jnp.where(lo_c, m_g[i][:n, pw:], m_g[i][n:, pw:]) + jnp.where(lo_c, kv[i][:n], kv[i][n:])
            st_scr[p] = so[p][c:] + g
            ys.append(so[p][:c] + (rq_y0[i][:, pw:] + av[i][c:]))
        y_rows[u] = jnp.concatenate(ys, axis=1)
    y = jnp.concatenate([y_rows[u] for u in subs], axis=0)

    if final:
        ysum = y + yb_ref[...]
        mu_h = _seg_sum(ysum, n) * (1.0 / n)
        yc = ysum - mu_h
        var = _seg_sum(yc * yc, n) * (1.0 / n)
        ln_g, ln_b = vec_ref[5:6, :], vec_ref[6:7, :]
        yn = yc * lax.rsqrt(var + GN_EPS) * ln_g + ln_b + bonus + bb_ref[...]
        gate = _bdot(_sigmoid(gl_ref[...]), gup_ref[...])
        o_ref[...] = (yn * gate).astype(o_ref.dtype)
    else:
        y_ref[...] = y
        bonus_ref[...] = bonus


def _rwkv_call(p, prm, other, rev, n_ctx):
    t = p.shape[0]
    c = RWKV_SUB * RWKV_CHUNK
    n, nc = t // c, n_ctx // c
    wdt = RWKV_W
    d = 1 if rev else 0
    final = other is not None
    cm = lambda s: _chunk_index(s, rev, nc, n)
    base = 4096 // wdt
    full = lambda shape: pl.BlockSpec(shape, lambda s: (0,) * len(shape))
    in_specs = [pl.BlockSpec((c, wdt), lambda s: (cm(s), base)),
                pl.BlockSpec((c, wdt), lambda s: (cm(s), base + 1)),
                pl.BlockSpec((c, wdt), lambda s: (cm(s), base + 2)),
                pl.BlockSpec((c, LANES), lambda s: (cm(s), 7168 // LANES)),
                pl.BlockSpec((c, LANES), lambda s: (cm(s), 7168 // LANES + 1)),
                full((1, 3 * wdt)), full((2, LANES)), full((SUBLANES, wdt)),
                full((RWKV_LORA, wdt)), full((RWKV_LORA, wdt))]
    args = [p, p, p, p, p, prm["mu"][d], prm["mu_small"][d], prm["vecs"][d], prm["w_up"][d], prm["a_up"][d]]
    if final:
        in_specs += [pl.BlockSpec((c, 2 * LANES), lambda s: (cm(s), 7424 // (2 * LANES))),
                     full((2 * LANES, wdt)),
                     pl.BlockSpec((c, wdt), lambda s: (cm(s), 0)),
                     pl.BlockSpec((c, wdt), lambda s: (cm(s), 0))]
        args += [p, prm["g_up"], other[0], other[1]]
        out_specs = pl.BlockSpec((c, wdt), lambda s: (cm(s), 0))
        out_shape = jax.ShapeDtypeStruct((t, wdt), BF16)
    else:
        out_specs = [pl.BlockSpec((c, wdt), lambda s: (cm(s), 0))] * 2
        out_shape = [jax.ShapeDtypeStruct((t, wdt), F32)] * 2
    return pl.pallas_call(
        functools.partial(_rwkv_kernel, rev=rev, final=final, c=RWKV_CHUNK, nsub=RWKV_SUB, nc=nc),
        grid=(n,),
        in_specs=in_specs,
        out_specs=out_specs,
        out_shape=out_shape,
        scratch_shapes=[pltpu.VMEM((RWKV_HEADS // 2, RWKV_N, 2 * RWKV_N), F32),
                        pltpu.VMEM((SUBLANES, 3 * wdt + 2 * LANES), F32)],
        compiler_params=_cparams(1),
        name="rwkv7_bwd" if rev else "rwkv7_fwd_merge",
    )(*args)


def _mlstm_kernel(*refs, rev, final, c):
    if final:
        q_ref, k_ref, v_ref, sm_ref, bias_ref, og_ref, ng_ref, hb_ref, o_ref, ct_scr, m_scr = refs
    else:
        q_ref, k_ref, v_ref, sm_ref, bias_ref, o_ref, ct_scr, m_scr = refs
    d = 1 if rev else 0
    dk, dv = MLSTM_DK, MLSTM_DV
    nh = MLSTM_HEADS

    @pl.when(pl.program_id(0) == 0)
    def _():
        ct_scr[...] = jnp.zeros_like(ct_scr)
        m_scr[...] = jnp.zeros_like(m_scr)

    incl, _ = _order_masks(c, rev)
    keep = 0 if rev else c - 1
    sm = sm_ref[...]
    li = sm + bias_ref[0:1, :]
    lf = _log_sigmoid(sm + bias_ref[1:2, :])
    bcum = _dot_mask_lhs(jnp.where(incl, 1.0, 0.0).astype(BF16), lf)
    bcum_t = bcum.T
    li_t = li.T
    ones_col = jnp.where(lax.broadcasted_iota(jnp.int32, (c, LANES), 1) == 0, 1.0, 0.0).astype(BF16)

    heads = range(nh)
    cis = [d * 2 * nh + h for h in heads]
    b_col = [bcum[:, ci + nh:ci + nh + 1] for ci in cis]
    i_col = [li[:, ci:ci + 1] for ci in cis]
    m_prev = [m_scr[h:h + 1, 0:1] for h in heads]
    qs = [q_ref[:, h * dk:(h + 1) * dk] * (dk ** -0.5) for h in heads]
    ks = [k_ref[:, h * dk:(h + 1) * dk] for h in heads]
    v_aug = [jnp.concatenate([v_ref[:, h * dv:(h + 1) * dv].astype(BF16), ones_col], axis=1) for h in heads]
    qk_raw = [_bdot_nt(q, k) for q, k in zip(qs, ks)]
    q_ct = [_bdot(q, ct_scr[h]) for q, h in zip(qs, heads)]
    d_log = [jnp.where(incl, bc - bcum_t[ci + nh:ci + nh + 1, :] + li_t[ci:ci + 1, :], -jnp.inf)
             for bc, ci in zip(b_col, cis)]
    inter = [bc + mp for bc, mp in zip(b_col, m_prev)]
    m_t = [jnp.maximum(jnp.max(dl, axis=-1, keepdims=True), it) for dl, it in zip(d_log, inter)]
    qk = [x * jnp.exp(dl - mt) for x, dl, mt in zip(qk_raw, d_log, m_t)]
    num_aug = [_bdot(x, va) + jnp.exp(it - mt) * qc for x, va, it, mt, qc in zip(qk, v_aug, inter, m_t, q_ct)]
    b_end = [bc[keep:keep + 1, :] for bc in b_col]
    w_log = [be - bc + ic for be, bc, ic in zip(b_end, b_col, i_col)]
    m_new = [jnp.maximum(be + mp, jnp.max(wl, axis=0, keepdims=True)) for be, mp, wl in zip(b_end, m_prev, w_log)]
    kv = [_bdot_tn(k * jnp.exp(wl - mn), va) for k, wl, mn, va in zip(ks, w_log, m_new, v_aug)]
    for h in heads:
        ct_scr[h] = jnp.exp(b_end[h] + m_prev[h] - m_new[h]) * ct_scr[h] + kv[h]
        m_scr[h:h + 1, :] = jnp.broadcast_to(m_new[h], (1, LANES))
        den = num_aug[h][:, dv:dv + 1]
        hout = num_aug[h][:, :dv] / jnp.maximum(jnp.abs(den), jnp.exp(-m_t[h]))
        sl = slice(h * dv, (h + 1) * dv)
        if final:
            tot = hout + hb_ref[:, sl]
            y = tot * lax.rsqrt(jnp.mean(tot * tot, axis=-1, keepdims=True) + EPS)
            o_ref[:, sl] = (y * ng_ref[0:1, sl] * _sigmoid(og_ref[:, sl])).astype(o_ref.dtype)
        else:
            o_ref[:, sl] = hout


def _mlstm_call(p, bias_rows, norm_g, other, rev, n_ctx):
    t = p.shape[0]
    c = MLSTM_CHUNK
    n, nc = t // c, n_ctx // c
    dk, dv, nh = MLSTM_DK, MLSTM_DV, MLSTM_HEADS
    final = other is not None
    cm = lambda s: _chunk_index(s, rev, nc, n)
    full = lambda shape: pl.BlockSpec(shape, lambda s: (0,) * len(shape))
    in_specs = [pl.BlockSpec((c, nh * dk), lambda s: (cm(s), 0)),
                pl.BlockSpec((c, nh * dk), lambda s: (cm(s), 1)),
                pl.BlockSpec((c, nh * dv), lambda s: (cm(s), 1)),
                pl.BlockSpec((c, LANES), lambda s: (cm(s), 7168 // LANES)),
                full((2, LANES))]
    args = [p, p, p, p, bias_rows]
    if final:
        in_specs += [pl.BlockSpec((c, nh * dv), lambda s: (cm(s), 2)), full((1, nh * dv)),
                     pl.BlockSpec((c, nh * dv), lambda s: (cm(s), 0))]
        args += [p, norm_g.reshape(1, nh * dv), other]
    return pl.pallas_call(
        functools.partial(_mlstm_kernel, rev=rev, final=final, c=c),
        grid=(n,),
        in_specs=in_specs,
        out_specs=pl.BlockSpec((c, nh * dv), lambda s: (cm(s), 0)),
        out_shape=jax.ShapeDtypeStruct((t, nh * dv), BF16 if final else F32),
        scratch_shapes=[pltpu.VMEM((nh, dk, dv + LANES), F32), pltpu.VMEM((SUBLANES, LANES), F32)],
        compiler_params=_cparams(1),
        name="mlstm_bwd" if rev else "mlstm_fwd_merge",
    )(*args)


def _gdn_prep_kernel(xm_ref, xp_ref, xn_ref, cw_ref, sm_ref, gp_ref, q_ref, k_ref, v_ref, gb_ref, *, tm, n_ctx, t_total):
    i = pl.program_id(0)
    x = xm_ref[...]
    rows = i * tm + lax.broadcasted_iota(jnp.int32, (tm, 1), 0)
    first = rows == i * tm
    last = rows == i * tm + tm - 1
    prev = jnp.where(first, xp_ref[SUBLANES - 1:SUBLANES, :], pltpu.roll(x, 1, axis=0))
    nxt = jnp.where(last, xn_ref[0:1, :], pltpu.roll(x, tm - 1, axis=0))
    prev = jnp.where(jnp.logical_or(rows == 0, rows == n_ctx), 0.0, prev)
    nxt = jnp.where(jnp.logical_or(rows == n_ctx - 1, rows == t_total - 1), 0.0, nxt)
    y = _silu(prev * cw_ref[0:1, :] + x * cw_ref[1:2, :] + nxt * cw_ref[2:3, :])
    w = GDN_HEADS * GDN_DK
    q, k, v = y[:, :w], y[:, w:2 * w], y[:, 2 * w:]
    q_ref[...] = (q * lax.rsqrt(_seg_sum(q * q, GDN_DK) + L2_EPS) * (GDN_DK ** -0.5)).astype(BF16)
    k_ref[...] = (k * lax.rsqrt(_seg_sum(k * k, GDN_DK) + L2_EPS)).astype(BF16)
    v_ref[...] = v.astype(BF16)
    sm = sm_ref[...]
    lane = lax.broadcasted_iota(jnp.int32, sm.shape, 1)
    log_alpha = -jnp.exp(gp_ref[0:1, :]) * _softplus(sm + gp_ref[1:2, :])
    gb_ref[...] = jnp.where(lane < 32, log_alpha, _sigmoid(sm))


def _gdn_prep_call(p, conv_w, gate_params, n_ctx):
    t = p.shape[0]
    tm = 256
    w = GDN_HEADS * GDN_DK
    qkv_blk = 3072 // GDN_QKV
    nb8 = t // SUBLANES
    r8 = tm // SUBLANES
    spec_o = pl.BlockSpec((tm, w), lambda i: (i, 0))
    return pl.pallas_call(
        functools.partial(_gdn_prep_kernel, tm=tm, n_ctx=n_ctx, t_total=t),
        grid=(t // tm,),
        in_specs=[pl.BlockSpec((tm, GDN_QKV), lambda i: (i, qkv_blk)),
                  pl.BlockSpec((SUBLANES, GDN_QKV), lambda i: (jnp.maximum(i * r8 - 1, 0), qkv_blk)),
                  pl.BlockSpec((SUBLANES, GDN_QKV), lambda i: (jnp.minimum((i + 1) * r8, nb8 - 1), qkv_blk)),
                  pl.BlockSpec((3, GDN_QKV), lambda i: (0, 0)),
                  pl.BlockSpec((tm, LANES), lambda i: (i, 7168 // LANES)),
                  pl.BlockSpec((2, LANES), lambda i: (0, 0))],
        out_specs=[spec_o, spec_o, spec_o, pl.BlockSpec((tm, LANES), lambda i: (i, 0))],
        out_shape=[jax.ShapeDtypeStruct((t, w), BF16)] * 3 + [jax.ShapeDtypeStruct((t, LANES), F32)],
        compiler_params=_cparams(1),
        name="gdn_conv_norm_gates",
    )(p, p, p, conv_w, p, gate_params)


def _gdn_kernel(*refs, rev, final, c, nsub):
    if final:
        q_ref, k_ref, v_ref, gb_ref, gate_ref, ng_ref, ob_ref, o_ref, st_scr = refs
    else:
        q_ref, k_ref, v_ref, gb_ref, o_ref, st_scr = refs
    d = 1 if rev else 0
    dk = GDN_DK
    nh = GDN_HEADS
    rows_all = nsub * c

    @pl.when(pl.program_id(0) == 0)
    def _():
        st_scr[...] = jnp.zeros_like(st_scr)

    keep = 0 if rev else c - 1
    gb = gb_ref[...]
    ri = lax.broadcasted_iota(jnp.int32, (rows_all, rows_all), 0)
    ci = lax.broadcasted_iota(jnp.int32, (rows_all, rows_all), 1)
    ri_in, ci_in = jnp.bitwise_and(ri, c - 1), jnp.bitwise_and(ci, c - 1)
    before = (ci_in >= ri_in) if rev else (ci_in <= ri_in)
    cmask = jnp.logical_and(jnp.right_shift(ri, 6) == jnp.right_shift(ci, 6), before)
    gc = _dot_mask_lhs(jnp.where(cmask, 1.0, 0.0).astype(BF16), gb)
    gc_t = gc.T
    incl2, strict2, eye2, lo_c = _pair_masks(c, rev)
    bdiag = _bdiag
    zeros = jnp.zeros((c, 2 * dk), F32)

    subs = list(range(nsub))
    inst = [(u, h) for u in subs for h in range(nh)]
    rs = {u: slice(u * c, (u + 1) * c) for u in subs}
    sls = [slice(h * dk, (h + 1) * dk) for h in range(nh)]
    cols = [16 + d * nh + h for h in range(nh)]
    g_col = {(u, h): gc[rs[u], cols[h]:cols[h] + 1] for u, h in inst}
    beta = {(u, h): gb[rs[u], cols[h] + 16:cols[h] + 17] for u, h in inst}
    g_end = {x: g_col[x][keep:keep + 1, :] for x in inst}
    e_g = {x: jnp.exp(g_col[x]) for x in inst}
    qs = {(u, h): q_ref[rs[u], sls[h]].astype(F32) for u, h in inst}
    ks = {(u, h): k_ref[rs[u], sls[h]].astype(F32) for u, h in inst}
    vs = {(u, h): v_ref[rs[u], sls[h]].astype(F32) for u, h in inst}

    pairs = [(u, a) for u in subs for a in range(0, nh, 2)]
    kq_e, kq_o, decay2 = [], [], []
    for u, a in pairs:
        e, o = (u, a), (u, a + 1)
        kk2 = jnp.concatenate([ks[e], ks[o]], axis=0)
        kq_e.append(_bdot_nt(jnp.concatenate([ks[e] * beta[e], qs[e]], axis=0), kk2))
        kq_o.append(_bdot_nt(jnp.concatenate([ks[o] * beta[o], qs[o]], axis=0), kk2))
        g_row2 = jnp.concatenate([gc_t[cols[a]:cols[a] + 1, rs[u]], gc_t[cols[a + 1]:cols[a + 1] + 1, rs[u]]], axis=1)
        decay2.append(jnp.exp(jnp.where(incl2, jnp.where(lo_c, g_col[e], g_col[o]) - g_row2, -jnp.inf)))
    l2 = [jnp.where(strict2, -(jnp.where(lo_c, e[:c], o[:c]) * dc), 0.0) for e, o, dc in zip(kq_e, kq_o, decay2)]
    a_qk2 = [jnp.where(lo_c, e[c:], o[c:]) * dc for e, o, dc in zip(kq_e, kq_o, decay2)]
    tinv2 = _neumann_inverse_packed(l2, eye2, c, lambda x2, y2: _dot_inv(x2, bdiag(y2)))
    rhs2 = []
    for u, a in pairs:
        e, o = (u, a), (u, a + 1)
        r_e = jnp.concatenate([vs[e] * beta[e], ks[e] * (beta[e] * e_g[e]), zeros], axis=1)
        r_o = jnp.concatenate([zeros, vs[o] * beta[o], ks[o] * (beta[o] * e_g[o])], axis=1)
        rhs2.append(jnp.concatenate([r_e, r_o], axis=0))
    uw2 = [_dot_inv(t, r) for t, r in zip(tinv2, rhs2)]
    op2 = []
    for aq, x in zip(a_qk2, uw2):
        x_bd = jnp.concatenate([jnp.concatenate([x[:, :2 * dk], zeros], axis=1),
                                jnp.concatenate([zeros, x[:, 2 * dk:]], axis=1)], axis=0)
        op2.append(_bdot(aq, x_bd))
    uw, o_part = {}, {}
    for (u, a), x, o2 in zip(pairs, uw2, op2):
        uw[(u, a)], uw[(u, a + 1)] = x[:, :2 * dk], x[:, 2 * dk:]
        o_part[(u, a)], o_part[(u, a + 1)] = o2[:, :2 * dk], o2[:, 2 * dk:]
    s_part = {x: _bdot_tn(ks[x] * jnp.exp(g_end[x] - g_col[x]), uw[x]) for x in inst}

    for u in (subs[::-1] if rev else subs):
        so = [_dot_state(jnp.concatenate([qs[(u, h)] * e_g[(u, h)] - o_part[(u, h)][:, dk:], s_part[(u, h)][:, dk:]], axis=0),
                         st_scr[h]) for h in range(nh)]
        for h in range(nh):
            x = (u, h)
            out = so[h][:c] + o_part[x][:, :dk]
            st_scr[h] = jnp.exp(g_end[x]) * st_scr[h] - so[h][c:] + s_part[x][:, :dk]
            if final:
                tot = out + ob_ref[rs[u], sls[h]]
                y = tot * lax.rsqrt(jnp.mean(tot * tot, axis=-1, keepdims=True) + EPS)
                o_ref[rs[u], sls[h]] = (y * ng_ref[0:1, sls[h]] * _silu(gate_ref[rs[u], sls[h]])).astype(o_ref.dtype)
            else:
                o_ref[rs[u], sls[h]] = out


def _gdn_call(p, q, k, v, gb, norm_g, other, rev, n_ctx):
    t = q.shape[0]
    c = GDN_SUB * GDN_CHUNK
    n, nc = t // c, n_ctx // c
    w = GDN_HEADS * GDN_DK
    final = other is not None
    cm = lambda s: _chunk_index(s, rev, nc, n)
    blk = pl.BlockSpec((c, w), lambda s: (cm(s), 0))
    in_specs = [blk, blk, blk, pl.BlockSpec((c, LANES), lambda s: (cm(s), 0))]
    args = [q, k, v, gb]
    if final:
        in_specs += [pl.BlockSpec((c, w), lambda s: (cm(s), 6144 // w)), pl.BlockSpec((1, w), lambda s: (0, 0)), blk]
        args += [p, norm_g.reshape(1, w), other]
    return pl.pallas_call(
        functools.partial(_gdn_kernel, rev=rev, final=final, c=GDN_CHUNK, nsub=GDN_SUB),
        grid=(n,),
        in_specs=in_specs,
        out_specs=blk,
        out_shape=jax.ShapeDtypeStruct((t, w), BF16 if final else F32),
        scratch_shapes=[pltpu.VMEM((GDN_HEADS, GDN_DK, GDN_DK), F32)],
        compiler_params=_cparams(1),
        name="gdn_bwd" if rev else "gdn_fwd_merge",
    )(*args)


def _pad_cols(w, width):
    return jnp.pad(w, ((0, 0), (0, width - w.shape[1])))


def _lane_row(pieces):
    row = jnp.zeros((LANES,), F32)
    for off, vec in pieces:
        row = row.at[off:off + vec.shape[0]].set(vec.astype(F32))
    return row


def _rwkv_params(e, rwkv_mu, rwkv_w0, rwkv_w_up, rwkv_a0, rwkv_a_up, rwkv_k_k, rwkv_k_a, rwkv_r_k,
                 rwkv_ln_g, rwkv_ln_b, rwkv_g_up):
    wdt = RWKV_W
    mu = rwkv_mu[e]
    zeros64 = jnp.zeros((RWKV_LORA,), F32)
    mu_small, vecs = [], []
    for d in range(2):
        m_wd = mu[d, 3 * wdt:3 * wdt + RWKV_LORA]
        m_ad = mu[d, 3 * wdt + RWKV_LORA:]
        lo = [m_wd, zeros64] if d == 0 else [zeros64, m_wd]
        la = [m_ad, zeros64] if d == 0 else [zeros64, m_ad]
        mu_small.append(jnp.stack([jnp.concatenate(lo), jnp.concatenate(la)]))
        vecs.append(jnp.stack([rwkv_w0[e, d], rwkv_a0[e, d], rwkv_k_k[e], rwkv_k_a[e], rwkv_r_k[e].reshape(wdt),
                               rwkv_ln_g[e], rwkv_ln_b[e], jnp.zeros((wdt,), F32)]))
    return {
        "mu": mu[:, None, :3 * wdt],
        "mu_small": jnp.stack(mu_small),
        "vecs": jnp.stack(vecs),
        "w_up": rwkv_w_up[e].astype(BF16),
        "a_up": rwkv_a_up[e].astype(BF16),
        "g_up": jnp.pad(rwkv_g_up[e], ((0, 2 * LANES - RWKV_GATE_LORA), (0, 0))).astype(BF16),
    }


def _even_mixer(p, e, n_ctx, ret_decay_logit, rwkv_prm):
    logit = ret_decay_logit[e].astype(F32)
    ret_b = _retention_call(p, logit, None, True, n_ctx)
    y_ret = _retention_call(p, logit, ret_b, False, n_ctx)
    rw_b = _rwkv_call(p, rwkv_prm, None, True, n_ctx)
    y_rwkv = _rwkv_call(p, rwkv_prm, rw_b, False, n_ctx)
    return y_ret, y_rwkv


def _odd_mixer(p, o, n_ctx, mlstm_gate_b, mlstm_norm_g, gdn_conv, gdn_a_log, gdn_dt_bias, gdn_norm_g):
    gate_b = mlstm_gate_b[o]
    nh = MLSTM_HEADS
    bias_i = _lane_row([(d * 2 * nh, gate_b[d, 0]) for d in range(2)])
    bias_f = _lane_row([(d * 2 * nh + nh, gate_b[d, 1]) for d in range(2)])
    bias_rows = jnp.stack([bias_i, bias_f])
    ml_b = _mlstm_call(p, bias_rows, mlstm_norm_g[o], None, True, n_ctx)
    y_ml = _mlstm_call(p, bias_rows, mlstm_norm_g[o], ml_b, False, n_ctx)

    gate_params = jnp.stack([_lane_row([(16, gdn_a_log[o].reshape(-1))]), _lane_row([(16, gdn_dt_bias[o].reshape(-1))])])
    q, k, v, gb = _gdn_prep_call(p, gdn_conv[o], gate_params, n_ctx)
    gd_b = _gdn_call(p, q, k, v, gb, gdn_norm_g[o], None, True, n_ctx)
    y_gd = _gdn_call(p, q, k, v, gb, gdn_norm_g[o], gd_b, False, n_ctx)
    return y_ml, y_gd


def kernel(x, c, ctx, c_ctx, ada_w, ada_b, norm1_g, norm2_g, mix_w_out, ffn_w_in, ffn_dw, ffn_dw_b, ffn_w_out, final_norm_g, ev_w_in, ret_decay_logit, rwkv_mu, rwkv_w0, rwkv_w_up, rwkv_a0, rwkv_a_up, rwkv_k_k, rwkv_k_a, rwkv_r_k, rwkv_ln_g, rwkv_ln_b, rwkv_g_up, od_w_in, mlstm_gate_b, mlstm_norm_g, gdn_conv, gdn_a_log, gdn_dt_bias, gdn_norm_g):
    assert x.shape[0] == 1 and ctx.shape[0] == 1
    n_ctx = ctx.shape[1]
    n_lat = x.shape[1]
    depth = ada_w.shape[0]
    cvecs = jnp.stack([c_ctx, c[0]], axis=1)
    mod = _ada_call(cvecs, ada_w, ada_b)
    xt, h1 = _assemble_call(ctx, x, norm1_g[0], mod)

    for l in range(depth):
        last = l == depth - 1
        half = D_MODEL // 2
        if l % 2 == 0:
            e = l // 2
            p = _proj_call(h1, ev_w_in[e], F_PAD)
            prm = _rwkv_params(e, rwkv_mu, rwkv_w0, rwkv_w_up, rwkv_a0, rwkv_a_up, rwkv_k_k, rwkv_k_a, rwkv_r_k,
                               rwkv_ln_g, rwkv_ln_b, rwkv_g_up)
            ya, yb = _even_mixer(p, e, n_ctx, ret_decay_logit, prm)
        else:
            o = l // 2
            p = _proj_odd_call(h1, od_w_in, o)
            ya, yb = _odd_mixer(p, o, n_ctx, mlstm_gate_b, mlstm_norm_g, gdn_conv, gdn_a_log, gdn_dt_bias, gdn_norm_g)
        w_mix = mix_w_out[l].astype(BF16)
        xt, h2 = _outproj_call(ya, yb, w_mix[:half], w_mix[half:], xt, norm2_g[l], mod, l, n_ctx)
        gate, up, wo = _ffn_in_call(h2, ffn_w_in, ffn_w_out, l)
        dw9 = _pad_cols(ffn_dw[l].reshape(9, D_FF), FF_PAD)
        dwb = _pad_cols(ffn_dw_b[l].reshape(1, D_FF), FF_PAD)
        if last:
            xt = _ffn_out_call(gate, up, dw9, dwb, wo, xt, mod, l, final_norm_g, n_ctx, True)
        else:
            xt, h1 = _ffn_out_call(gate, up, dw9, dwb, wo, xt, mod, l, norm1_g[l + 1], n_ctx, False)
    return xt[n_ctx:][None]
```

```python
import functools
import math

import jax
import jax.numpy as jnp
from jax import lax
from jax.experimental import pallas as pl
from jax.experimental.pallas import tpu as pltpu

F32 = jnp.float32
BF16 = jnp.bfloat16

D_MODEL = 2048
DEPTH = 2
GRID_W = 64
EPS = 1e-6
GN_EPS = 64e-5
L2_EPS = 1e-12
D_FF = 5504

RET_HEADS = 8
RET_DK = 128
RWKV_HEADS = 16
RWKV_N = 64
RWKV_W = RWKV_HEADS * RWKV_N
RWKV_LORA = 64
RWKV_GATE_LORA = 160
MLSTM_HEADS = 4
MLSTM_DK = 128
MLSTM_DV = 256
GDN_HEADS = 8
GDN_DK = 128
GDN_QKV = 3072

LANES = 128
SUBLANES = 8
MXU_N = 256
HALO_ROWS = 16
P_DTYPE = jnp.bfloat16
VMEM_LIMIT = 56 * 1024 * 1024

F_PAD = 7680
FF_PAD = 5632
ROW_TILE = 768
WIDE_ROW_TILE = 1408
RET_CHUNK = 256
MLSTM_CHUNK = 256
GDN_CHUNK = 64
RWKV_CHUNK = 64
GDN_SUB = 4
RWKV_SUB = 2


def _wide_tile(t):
    return WIDE_ROW_TILE if t % WIDE_ROW_TILE == 0 else ROW_TILE


def _cparams(n_axes):
    return pltpu.CompilerParams(dimension_semantics=("arbitrary",) * n_axes, vmem_limit_bytes=VMEM_LIMIT)


def _bdot(a, b):
    return jnp.dot(a.astype(BF16), b.astype(BF16), preferred_element_type=F32)


def _bdot_nt(a, b):
    return lax.dot_general(a.astype(BF16), b.astype(BF16), (((1,), (1,)), ((), ())), preferred_element_type=F32)


def _bdot_tn(a, b):
    return lax.dot_general(a.astype(BF16), b.astype(BF16), (((0,), (0,)), ((), ())), preferred_element_type=F32)


def _split3(x):
    x1 = x.astype(BF16)
    r1 = x - x1.astype(F32)
    x2 = r1.astype(BF16)
    x3 = (r1 - x2.astype(F32)).astype(BF16)
    return x1, x2, x3


def _dot_mask_lhs(m_bf16, x):
    x1, x2, x3 = _split3(x)
    d = lambda t: jnp.dot(m_bf16, t, preferred_element_type=F32)
    return (d(x3) + d(x2)) + d(x1)


def _dot3(a, b):
    a1 = a.astype(BF16)
    a2 = (a - a1.astype(F32)).astype(BF16)
    b1 = b.astype(BF16)
    b2 = (b - b1.astype(F32)).astype(BF16)
    d = lambda u, v: jnp.dot(u, v, preferred_element_type=F32)
    return (d(a2, b1) + d(a1, b2)) + d(a1, b1)


def _sigmoid(x):
    return 1.0 / (1.0 + jnp.exp(-x))


def _silu(x):
    return x * _sigmoid(x)


def _softplus(x):
    return jnp.maximum(x, 0.0) + jnp.log1p(jnp.exp(-jnp.abs(x)))


def _log_sigmoid(x):
    return -_softplus(-x)


def _order_masks(c, rev):
    i = lax.broadcasted_iota(jnp.int32, (c, c), 0)
    j = lax.broadcasted_iota(jnp.int32, (c, c), 1)
    if rev:
        return j >= i, j > i
    return j <= i, j < i


def _neumann_inverse(n, c, dotf):
    i = lax.broadcasted_iota(jnp.int32, (c, c), 0)
    j = lax.broadcasted_iota(jnp.int32, (c, c), 1)
    x = jnp.where(i == j, 1.0, 0.0).astype(F32) + n
    p = dotf(n, n)
    for _ in range(int(math.log2(c)) - 2):
        r = dotf(jnp.concatenate([x, p], axis=0), p)
        x = x + r[:c]
        p = r[c:]
    return x + dotf(x, p)


def _neumann_inverse_multi(ns, c, dotf):
    i = lax.broadcasted_iota(jnp.int32, (c, c), 0)
    j = lax.broadcasted_iota(jnp.int32, (c, c), 1)
    eye = jnp.where(i == j, 1.0, 0.0).astype(F32)
    xs = [eye + n for n in ns]
    ps = [dotf(n, n) for n in ns]
    for _ in range(int(math.log2(c)) - 2):
        rs = [dotf(jnp.concatenate([x, p], axis=0), p) for x, p in zip(xs, ps)]
        xs = [x + r[:c] for x, r in zip(xs, rs)]
        ps = [r[c:] for r in rs]
    return [x + dotf(x, p) for x, p in zip(xs, ps)]


def _pair_masks(c, rev):
    assert 2 * c == LANES and c == 64
    row = lax.broadcasted_iota(jnp.int32, (c, 2 * c), 0)
    lane = lax.broadcasted_iota(jnp.int32, (c, 2 * c), 1)
    src = jnp.bitwise_and(lane, c - 1)
    incl2 = (src >= row) if rev else (src <= row)
    strict2 = (src > row) if rev else (src < row)
    return incl2, strict2, src == row, lane < c


def _bdiag(x2, anti=False):
    n2 = x2.shape[1]
    blk_r = jnp.right_shift(lax.broadcasted_iota(jnp.int32, (n2, n2), 0), 6)
    blk_l = jnp.right_shift(lax.broadcasted_iota(jnp.int32, (n2, n2), 1), 6)
    stacked = jnp.concatenate([x2, x2], axis=0)
    if anti:
        return jnp.where(blk_r == blk_l, 0.0, stacked)
    return jnp.where(blk_r == blk_l, stacked, 0.0)


def _split2(x):
    hi = x.astype(BF16)
    return hi, (x - hi.astype(F32)).astype(BF16)


def _dot3_split(a, b):
    d = lambda u, v: jnp.dot(u, v, preferred_element_type=F32)
    return (d(a[1], b[0]) + d(a[0], b[1])) + d(a[0], b[0])


def _neumann_inverse_packed(ns, eye2, c):
    n2 = 2 * c
    blk_r = jnp.right_shift(lax.broadcasted_iota(jnp.int32, (n2, n2), 0), 6)
    blk_l = jnp.right_shift(lax.broadcasted_iota(jnp.int32, (n2, n2), 1), 6)
    pattern = jnp.where(blk_r == blk_l, 1.0, 0.0).astype(BF16)
    bd = lambda s: tuple(jnp.concatenate([h, h], axis=0) * pattern for h in s)
    cat = lambda s, t: tuple(jnp.concatenate([u, v], axis=0) for u, v in zip(s, t))
    xs = [jnp.where(eye2, 1.0, 0.0) + n for n in ns]
    nsp = [_split2(n) for n in ns]
    ps = [_dot3_split(s, bd(s)) for s in nsp]
    for _ in range(int(math.log2(c)) - 2):
        psp = [_split2(p) for p in ps]
        xsp = [_split2(x) for x in xs]
        rs = [_dot3_split(cat(x, p), bd(p)) for x, p in zip(xsp, psp)]
        xs = [x + r[:c] for x, r in zip(xs, rs)]
        ps = [r[c:] for r in rs]
    return [x + _dot3_split(_split2(x), bd(_split2(p))) for x, p in zip(xs, ps)]


def _dot_inv(a, b):
    return _dot3(a, b)


def _dot_state(a, b):
    return _bdot(a, b)


def _chunk_index(s, rev, n_ctx_chunks, n_chunks):
    if not rev:
        return s
    return jnp.where(s < n_ctx_chunks, n_ctx_chunks - 1 - s, n_chunks + n_ctx_chunks - 1 - s)


def _seg_sum(x, seg):
    c, w = x.shape
    if seg == LANES:
        parts = [jnp.broadcast_to(jnp.sum(x[:, b * LANES:(b + 1) * LANES], axis=-1, keepdims=True), (c, LANES))
                 for b in range(w // LANES)]
        return jnp.concatenate(parts, axis=-1)
    assert seg * 2 == LANES
    lane = lax.broadcasted_iota(jnp.int32, (c, LANES), 1)
    low = lane < seg
    parts = []
    for b in range(w // LANES):
        xb = x[:, b * LANES:(b + 1) * LANES]
        s_lo = jnp.sum(jnp.where(low, xb, 0.0), axis=-1, keepdims=True)
        s_hi = jnp.sum(jnp.where(low, 0.0, xb), axis=-1, keepdims=True)
        parts.append(jnp.where(low, s_lo, s_hi))
    return jnp.concatenate(parts, axis=-1)


def _ada_kernel(c_ref, w_ref, b_ref, o_ref):
    cv = c_ref[...]
    s = _silu(cv)
    w = w_ref[0]
    bias = b_ref[0]
    r0 = jnp.sum(w * s[:, 0:1], axis=0, keepdims=True) + bias
    r1 = jnp.sum(w * s[:, 1:2], axis=0, keepdims=True) + bias
    row = lax.broadcasted_iota(jnp.int32, (SUBLANES, w.shape[1]), 0)
    o_ref[0] = jnp.where(row == 0, r0, jnp.where(row == 1, r1, 0.0))


def _ada_call(cvecs, ada_w, ada_b):
    depth, d, n = ada_w.shape
    tn = 1024
    return pl.pallas_call(
        _ada_kernel,
        grid=(depth, n // tn),
        in_specs=[pl.BlockSpec((d, 2), lambda l, j: (0, 0)),
                  pl.BlockSpec((1, d, tn), lambda l, j: (l, 0, j)),
                  pl.BlockSpec((1, 1, tn), lambda l, j: (l, 0, j))],
        out_specs=pl.BlockSpec((1, SUBLANES, tn), lambda l, j: (l, 0, j)),
        out_shape=jax.ShapeDtypeStruct((depth, SUBLANES, n), F32),
        compiler_params=_cparams(2),
        name="ada_modulation",
    )(cvecs, ada_w, ada_b.reshape(depth, 1, n))


def _mod_rows(mod, k, rows, n_ctx):
    d = D_MODEL
    vc = mod[0:1, k * d:(k + 1) * d]
    vl = mod[1:2, k * d:(k + 1) * d]
    return jnp.where(rows < n_ctx, vc, vl)


def _norm_mod(x, g, shift, scale):
    y = x * lax.rsqrt(jnp.mean(x * x, axis=-1, keepdims=True) + EPS) * g
    return y * (1.0 + scale) + shift


def _assemble_kernel(ctx_ref, x_ref, g_ref, mod_ref, xt_ref, h_ref):
    i = pl.program_id(0)
    d = D_MODEL
    mod = mod_ref[0]

    def emit(src, row):
        xt_ref[...] = src
        h = _norm_mod(src, g_ref[...], mod[row:row + 1, 0:d], mod[row:row + 1, d:2 * d])
        h_ref[...] = h.astype(BF16)

    @pl.when(i == 0)
    def _():
        emit(ctx_ref[0], 0)

    @pl.when(i > 0)
    def _():
        emit(x_ref[0], 1)


def _assemble_call(ctx, x, g, mod):
    n_ctx, d = ctx.shape[1], ctx.shape[2]
    t = n_ctx + x.shape[1]
    assert x.shape[1] % n_ctx == 0
    spec_o = pl.BlockSpec((n_ctx, d), lambda i: (i, 0))
    return pl.pallas_call(
        _assemble_kernel,
        grid=(t // n_ctx,),
        in_specs=[pl.BlockSpec((1, n_ctx, d), lambda i: (0, 0, 0)),
                  pl.BlockSpec((1, n_ctx, d), lambda i: (0, jnp.maximum(i - 1, 0), 0)),
                  pl.BlockSpec((1, d), lambda i: (0, 0)),
                  pl.BlockSpec((1, SUBLANES, 6 * d), lambda i: (0, 0, 0))],
        out_specs=[spec_o, spec_o],
        out_shape=[jax.ShapeDtypeStruct((t, d), F32), jax.ShapeDtypeStruct((t, d), BF16)],
        compiler_params=_cparams(1),
        name="assemble_norm1",
    )(ctx, x, g.reshape(1, d), mod)


def _proj_kernel(h_ref, w_ref, o_ref, w_scr, *, tn, n_valid):
    @pl.when(pl.program_id(1) == 0)
    def _():
        w = w_ref[...]
        col = pl.program_id(0) * tn + lax.broadcasted_iota(jnp.int32, w.shape, 1)
        w_scr[...] = jnp.where(col < n_valid, w, 0).astype(BF16)

    o_ref[...] = jnp.dot(h_ref[...], w_scr[...], preferred_element_type=F32).astype(o_ref.dtype)


def _proj_call(h, w, f_pad):
    t, d = h.shape
    tm, tn = _wide_tile(t), 768
    return pl.pallas_call(
        functools.partial(_proj_kernel, tn=tn, n_valid=w.shape[1]),
        grid=(f_pad // tn, t // tm),
        in_specs=[pl.BlockSpec((tm, d), lambda j, i: (i, 0)),
                  pl.BlockSpec((d, tn), lambda j, i: (0, j))],
        out_specs=pl.BlockSpec((tm, tn), lambda j, i: (i, j)),
        out_shape=jax.ShapeDtypeStruct((t, f_pad), P_DTYPE),
        scratch_shapes=[pltpu.VMEM((d, tn), BF16)],
        compiler_params=_cparams(2),
        name="in_proj",
    )(h, w)


ODD_GATES_AT = 3072
ODD_GATES_W = 16
ODD_SMALL_AT = 7168
ODD_WIDTH = 7216


def _proj_odd_kernel(h_ref, w_ref, wn_ref, wg_ref, o_ref, w_scr, *, tn):
    j = pl.program_id(0)
    first_shift = ODD_GATES_AT // tn
    last = ODD_SMALL_AT // tn
    assert ODD_GATES_AT % tn == 0 and (last + 1) * tn == F_PAD

    @pl.when(pl.program_id(1) == 0)
    def _():
        w = w_ref[0]
        wide = jnp.concatenate([w, wn_ref[0]], axis=1)
        shifted = pltpu.roll(wide, wide.shape[1] - ODD_GATES_W, axis=1)[:, :tn]
        lane = lax.broadcasted_iota(jnp.int32, w.shape, 1)

        @pl.when(j < first_shift)
        def _():
            w_scr[...] = w.astype(BF16)

        @pl.when(jnp.logical_and(j >= first_shift, j < last))
        def _():
            w_scr[...] = shifted.astype(BF16)

        @pl.when(j == last)
        def _():
            q0 = ODD_SMALL_AT - last * tn
            gates = jnp.concatenate([wg_ref[0]] * (tn // LANES), axis=1)
            small = jnp.where(lane < q0 + ODD_GATES_W, gates, w)
            tile = jnp.where(lane < q0, shifted, jnp.where(lane < q0 + (ODD_WIDTH - ODD_SMALL_AT), small, 0.0))
            w_scr[...] = tile.astype(BF16)

    o_ref[...] = jnp.dot(h_ref[...], w_scr[...], preferred_element_type=F32).astype(o_ref.dtype)


def _proj_odd_call(h, od_w_in, o):
    t, d = h.shape
    tm, tn = ROW_TILE, 768
    assert od_w_in.shape[2] == ODD_WIDTH and ODD_SMALL_AT % LANES == 0 and ODD_GATES_AT % LANES == 0
    per = tn // LANES
    last_lane_tile = (ODD_WIDTH - 1) // LANES
    return pl.pallas_call(
        functools.partial(_proj_odd_kernel, tn=tn),
        grid=(F_PAD // tn, t // tm),
        in_specs=[pl.BlockSpec((tm, d), lambda j, i: (i, 0)),
                  pl.BlockSpec((1, d, tn), lambda j, i: (o, 0, j)),
                  pl.BlockSpec((1, d, LANES), lambda j, i: (o, 0, jnp.minimum(per * (j + 1), last_lane_tile))),
                  pl.BlockSpec((1, d, LANES), lambda j, i: (o, 0, ODD_GATES_AT // LANES))],
        out_specs=pl.BlockSpec((tm, tn), lambda j, i: (i, j)),
        out_shape=jax.ShapeDtypeStruct((t, F_PAD), P_DTYPE),
        scratch_shapes=[pltpu.VMEM((d, tn), BF16)],
        compiler_params=_cparams(2),
        name="in_proj_odd",
    )(h, od_w_in, od_w_in, od_w_in)


def _outproj_kernel(ya_ref, yb_ref, wa_ref, wb_ref, x_ref, g_ref, mod_ref, xo_ref, h_ref, *, tm, n_ctx):
    i = pl.program_id(0)
    mod = mod_ref[0]
    rb = tm // 2
    for r0 in range(0, tm, rb):
        rs = slice(r0, r0 + rb)
        rows = i * tm + r0 + lax.broadcasted_iota(jnp.int32, (rb, 1), 0)
        acc = jnp.dot(ya_ref[rs, :], wa_ref[...], preferred_element_type=F32)
        acc = acc + jnp.dot(yb_ref[rs, :], wb_ref[...], preferred_element_type=F32)
        xn = x_ref[rs, :] + _mod_rows(mod, 2, rows, n_ctx) * acc
        xo_ref[rs, :] = xn
        h = _norm_mod(xn, g_ref[...], _mod_rows(mod, 3, rows, n_ctx), _mod_rows(mod, 4, rows, n_ctx))
        h_ref[rs, :] = h.astype(BF16)


def _outproj_call(ya, yb, wa, wb, x, g2, mod, layer, n_ctx):
    t, d = x.shape
    half = ya.shape[1]
    tm = 384
    return pl.pallas_call(
        functools.partial(_outproj_kernel, tm=tm, n_ctx=n_ctx),
        grid=(t // tm,),
        in_specs=[pl.BlockSpec((tm, half), lambda i: (i, 0)),
                  pl.BlockSpec((tm, half), lambda i: (i, 0)),
                  pl.BlockSpec((half, d), lambda i: (0, 0)),
                  pl.BlockSpec((half, d), lambda i: (0, 0)),
                  pl.BlockSpec((tm, d), lambda i: (i, 0)),
                  pl.BlockSpec((1, d), lambda i: (0, 0)),
                  pl.BlockSpec((1, SUBLANES, 6 * d), lambda i: (layer, 0, 0))],
        out_specs=[pl.BlockSpec((tm, d), lambda i: (i, 0)),
                   pl.BlockSpec((tm, d), lambda i: (i, 0))],
        out_shape=[jax.ShapeDtypeStruct((t, d), F32), jax.ShapeDtypeStruct((t, d), BF16)],
        compiler_params=_cparams(1),
        name="mix_out_proj_norm2",
    )(ya, yb, wa, wb, x, g2.reshape(1, d), mod)


def _ffn_in_kernel(h_ref, wg_ref, wu0_ref, wu1_ref, wu2_ref, wu3_ref, wo_ref, g_ref, u_ref, wob_ref, w_scr, *, tn):
    @pl.when(pl.program_id(1) == 0)
    def _():
        j = pl.program_id(0)
        wu = jnp.concatenate([wu0_ref[0], wu1_ref[0], wu2_ref[0], wu3_ref[0]], axis=1)
        col = j * tn + lax.broadcasted_iota(jnp.int32, wu.shape, 1)
        w_scr[0] = jnp.where(col < D_FF, wg_ref[0], 0.0).astype(BF16)
        w_scr[1] = jnp.where(col < D_FF, wu, 0.0).astype(BF16)
        wo = wo_ref[0]
        row = j * tn + lax.broadcasted_iota(jnp.int32, wo.shape, 0)
        wob_ref[...] = jnp.where(row < D_FF, 0.5 * wo, 0.0).astype(BF16)

    h = h_ref[...]
    g_ref[...] = jnp.dot(h, w_scr[0], preferred_element_type=F32).astype(BF16)
    u_ref[...] = jnp.dot(h, w_scr[1], preferred_element_type=F32).astype(BF16)


def _ffn_in_call(h, ffn_w_in, ffn_w_out, layer):
    t, d = h.shape
    tm, tn = ROW_TILE, 512
    q = tn // LANES
    assert D_FF % LANES == 0 and q == 4
    up0 = D_FF // LANES
    last = 2 * D_FF // LANES - 1
    spec_o = pl.BlockSpec((tm, tn), lambda j, i: (i, j))
    up_specs = [pl.BlockSpec((1, d, LANES), lambda j, i, r=r: (layer, 0, jnp.minimum(up0 + q * j + r, last)))
                for r in range(q)]
    return pl.pallas_call(
        functools.partial(_ffn_in_kernel, tn=tn),
        grid=(FF_PAD // tn, t // tm),
        in_specs=[pl.BlockSpec((tm, d), lambda j, i: (i, 0)),
                  pl.BlockSpec((1, d, tn), lambda j, i: (layer, 0, j))] + up_specs
                 + [pl.BlockSpec((1, tn, d), lambda j, i: (layer, j, 0))],
        out_specs=[spec_o, spec_o, pl.BlockSpec((tn, d), lambda j, i: (j, 0))],
        out_shape=[jax.ShapeDtypeStruct((t, FF_PAD), BF16)] * 2 + [jax.ShapeDtypeStruct((FF_PAD, d), BF16)],
        scratch_shapes=[pltpu.VMEM((2, d, tn), BF16)],
        compiler_params=_cparams(2),
        name="ffn_in_proj",
    )(h, ffn_w_in, ffn_w_in, ffn_w_in, ffn_w_in, ffn_w_in, ffn_w_out)


def _glu_act(conv, bias, up):
    gate = conv + bias
    return (gate * (1.0 + lax.erf(gate * (2.0 ** -0.5))) * up.astype(F32)).astype(BF16)


def _ffn_out_kernel(*refs, tm, tk, n_ctx, t_total, final_norm):
    if final_norm:
        (gm_ref, gp_ref, gn_ref, u_ref, dw_ref, db_ref, wo_ref, x_ref, mod_ref, fg_ref,
         o_ref, acc_scr, act_a, act_b) = refs
    else:
        (gm_ref, gp_ref, gn_ref, u_ref, dw_ref, db_ref, wo_ref, x_ref, mod_ref, fg_ref, nmod_ref,
         o_ref, hn_ref, acc_scr, act_a, act_b) = refs
    i = pl.program_id(0)
    k = pl.program_id(1)
    nk = pl.num_programs(1) - 1
    w = GRID_W
    nrow = tm // w
    blk0 = i * nrow
    nb_ctx, nb_tot = n_ctx // w, t_total // w
    sub = lax.broadcasted_iota(jnp.int32, (SUBLANES, LANES), 0)

    def neighbours(g, n):
        gl = pltpu.roll(g, 1, axis=0)
        gr = pltpu.roll(g, n - 1, axis=0)
        gl = jnp.concatenate([jnp.where(sub == 0, 0.0, gl[:SUBLANES]), gl[SUBLANES:]], axis=0)
        gr = jnp.concatenate([gr[:n - SUBLANES], jnp.where(sub == SUBLANES - 1, 0.0, gr[n - SUBLANES:])], axis=0)
        return gl, gr

    def step(dst, src):
        d_out = acc_scr.shape[1]
        row_halves = 2
        hm = tm // row_halves
        n_mm = row_halves * d_out // MXU_N
        n_pieces = (tk // LANES) * (nrow + 2)

        def matmul_chunk(n):
            cs = slice((n // row_halves) * MXU_N, (n // row_halves + 1) * MXU_N)
            rs = slice((n % row_halves) * hm, (n % row_halves + 1) * hm)
            acc_scr[rs, cs] += jnp.dot(src[rs, :], wo_ref[:, cs], preferred_element_type=F32)

        mm_at = {(n * n_pieces) // n_mm: n for n in range(n_mm)}
        assert len(mm_at) == n_mm
        piece = 0
        for lb in range(tk // LANES):
            ls = slice(lb * LANES, (lb + 1) * LANES)
            dwv = dw_ref[:, ls]
            bias = db_ref[:, ls]
            part = [None] * nrow
            for r in range(-1, nrow + 1):
                if piece in mm_at:
                    matmul_chunk(mm_at[piece])
                piece += 1
                if r == -1:
                    g = gp_ref[:, ls]
                elif r == nrow:
                    g = gn_ref[:, ls]
                else:
                    g = gm_ref[r * w:(r + 1) * w, ls]
                g = g.astype(F32)
                gl, gr = neighbours(g, w)
                for kh, ro in ((0, r + 1), (1, r), (2, r - 1)):
                    if not 0 <= ro < nrow:
                        continue
                    wv = dwv[3 * kh:3 * kh + 3]
                    if kh == 0:
                        wv = wv * jnp.where(blk0 + ro >= nb_ctx + 1, 1.0, 0.0)
                    if kh == 2:
                        ok = jnp.logical_and(blk0 + ro >= nb_ctx, blk0 + ro < nb_tot - 1)
                        wv = wv * jnp.where(ok, 1.0, 0.0)
                    c = gl * wv[0:1] + g * wv[1:2] + gr * wv[2:3]
                    part[ro] = c if part[ro] is None else part[ro] + c
                ro = r - 1
                if 0 <= ro < nrow:
                    rs = slice(ro * w, (ro + 1) * w)
                    dst[rs, ls] = _glu_act(part[ro], bias, u_ref[rs, ls])
                    part[ro] = None

        @pl.when(i == 0)
        def _():
            for lb in range(tk // LANES):
                ls = slice(lb * LANES, (lb + 1) * LANES)
                g = gm_ref[0:n_ctx, ls].astype(F32)
                gl, gr = neighbours(g, n_ctx)
                conv = gl * dw_ref[3:4, ls] + g * dw_ref[4:5, ls] + gr * dw_ref[5:6, ls]
                dst[0:n_ctx, ls] = _glu_act(conv, db_ref[:, ls], u_ref[0:n_ctx, ls])

    @pl.when(k == 0)
    def _():
        acc_scr[...] = jnp.zeros_like(acc_scr)
        act_b[...] = jnp.zeros_like(act_b)

    @pl.when(k % 2 == 0)
    def _():
        step(act_a, act_b)

    @pl.when(k % 2 == 1)
    def _():
        step(act_b, act_a)

    @pl.when(k == nk)
    def _():
        rb = 2 * LANES
        for r0 in range(0, tm, rb):
            rs = slice(r0, r0 + rb)
            rws = i * tm + r0 + lax.broadcasted_iota(jnp.int32, (rb, 1), 0)
            xn = x_ref[rs, :] + _mod_rows(mod_ref[0], 5, rws, n_ctx) * acc_scr[rs, :]
            if final_norm:
                xn = xn * lax.rsqrt(jnp.mean(xn * xn, axis=-1, keepdims=True) + EPS) * fg_ref[...]
            else:
                nmod = nmod_ref[0]
                hn = _norm_mod(xn, fg_ref[...], _mod_rows(nmod, 0, rws, n_ctx), _mod_rows(nmod, 1, rws, n_ctx))
                hn_ref[rs, :] = hn.astype(BF16)
            o_ref[rs, :] = xn


def _ffn_out_call(gate, up, dw9, dwb, wo, x, mod, layer, norm_g, n_ctx, final_norm):
    t, d = x.shape
    fp = gate.shape[1]
    tm, tk, w = ROW_TILE, 512, GRID_W
    assert n_ctx <= tm and n_ctx % w == 0 and tm % (2 * LANES) == 0
    rpt = tm // w
    n_rows = t // w
    spec_x = pl.BlockSpec((tm, d), lambda i, k: (i, 0))
    extra_in, extra_args = [], []
    out_specs, out_shape = spec_x, jax.ShapeDtypeStruct((t, d), F32)
    if not final_norm:
        extra_in = [pl.BlockSpec((1, SUBLANES, 6 * d), lambda i, k: (layer + 1, 0, 0))]
        extra_args = [mod]
        out_specs = [spec_x, spec_x]
        out_shape = [out_shape, jax.ShapeDtypeStruct((t, d), BF16)]
    nk = fp // tk
    assert nk % 2 == 1
    kc = lambda k: jnp.minimum(k, nk - 1)
    return pl.pallas_call(
        functools.partial(_ffn_out_kernel, tm=tm, tk=tk, n_ctx=n_ctx, t_total=t, final_norm=final_norm),
        grid=(t // tm, nk + 1),
        in_specs=[pl.BlockSpec((tm, tk), lambda i, k: (i, kc(k))),
                  pl.BlockSpec((w, tk), lambda i, k: (jnp.maximum(i * rpt - 1, 0), kc(k))),
                  pl.BlockSpec((w, tk), lambda i, k: (jnp.minimum((i + 1) * rpt, n_rows - 1), kc(k))),
                  pl.BlockSpec((tm, tk), lambda i, k: (i, kc(k))),
                  pl.BlockSpec((9, tk), lambda i, k: (0, kc(k))),
                  pl.BlockSpec((1, tk), lambda i, k: (0, kc(k))),
                  pl.BlockSpec((tk, d), lambda i, k: (jnp.maximum(k - 1, 0), 0)),
                  spec_x,
                  pl.BlockSpec((1, SUBLANES, 6 * d), lambda i, k: (layer, 0, 0)),
                  pl.BlockSpec((1, d), lambda i, k: (0, 0))] + extra_in,
        out_specs=out_specs,
        out_shape=out_shape,
        scratch_shapes=[pltpu.VMEM((tm, d), F32), pltpu.VMEM((tm, tk), BF16), pltpu.VMEM((tm, tk), BF16)],
        compiler_params=_cparams(2),
        name="ffn_conv_glu_out",
    )(gate, gate, gate, up, dw9, dwb, wo, x, mod, norm_g.reshape(1, d), *extra_args)


def _retention_kernel(*refs, rev, final, c):
    if final:
        q_ref, k_ref, v_ref, lg_ref, gate_ref, ob_ref, o_ref, r_scr, di_scr, dq_scr, dk_scr, dc_scr = refs
    else:
        q_ref, k_ref, v_ref, lg_ref, o_ref, r_scr, di_scr, dq_scr, dk_scr, dc_scr = refs
    d = 1 if rev else 0
    dk = RET_DK

    @pl.when(pl.program_id(0) == 0)
    def _():
        r_scr[...] = jnp.zeros_like(r_scr)
        i = lax.broadcasted_iota(jnp.int32, (c, c), 0)
        j = lax.broadcasted_iota(jnp.int32, (c, c), 1)
        diff = ((j - i) if rev else (i - j)).astype(F32)
        row = lax.broadcasted_iota(jnp.int32, (c, dk), 0)
        pos = ((c - 1 - row) if rev else row).astype(F32)
        for h in range(RET_HEADS):
            lg = _log_sigmoid(lg_ref[d:d + 1, h:h + 1])
            di_scr[h] = jnp.where(diff >= 0, jnp.exp(lg * jnp.maximum(diff, 0.0)), 0.0)
            dq_scr[h] = jnp.exp(lg * (pos + 1.0))
            dk_scr[h] = jnp.exp(lg * (c - 1.0 - pos)) * (dk ** -0.5)
            dc_scr[h] = jnp.exp(jnp.broadcast_to(lg, (1, dk)) * c)

    for h in range(RET_HEADS):
        sl = slice(h * dk, (h + 1) * dk)
        qh = q_ref[:, sl].astype(F32)
        kh = k_ref[:, sl].astype(F32)
        vh = v_ref[:, sl].astype(BF16)
        r_state = r_scr[h]
        scores = _bdot_nt(qh, kh) * (di_scr[h] * (dk ** -0.5))
        out = _bdot(scores, vh) + _bdot(qh * dq_scr[h], r_state)
        r_scr[h] = dc_scr[h] * r_state + _bdot_tn(kh * dk_scr[h], vh)
        if final:
            tot = out + ob_ref[:, sl]
            y = tot * lax.rsqrt(jnp.mean(tot * tot, axis=-1, keepdims=True) + EPS) * _silu(gate_ref[:, sl].astype(F32))
            o_ref[:, sl] = y.astype(o_ref.dtype)
        else:
            o_ref[:, sl] = out


def _retention_call(p, logit, other, rev, n_ctx):
    t = p.shape[0]
    c = RET_CHUNK
    n, nc = t // c, n_ctx // c
    width = RET_HEADS * RET_DK
    final = other is not None
    cm = lambda s: _chunk_index(s, rev, nc, n)
    in_specs = [pl.BlockSpec((c, width), lambda s: (cm(s), 0)),
                pl.BlockSpec((c, width), lambda s: (cm(s), 1)),
                pl.BlockSpec((c, width), lambda s: (cm(s), 2)),
                pl.BlockSpec((2, RET_HEADS), lambda s: (0, 0))]
    args = [p, p, p, logit]
    if final:
        in_specs += [pl.BlockSpec((c, width), lambda s: (cm(s), 3)), pl.BlockSpec((c, width), lambda s: (cm(s), 0))]
        args += [p, other]
    return pl.pallas_call(
        functools.partial(_retention_kernel, rev=rev, final=final, c=c),
        grid=(n,),
        in_specs=in_specs,
        out_specs=pl.BlockSpec((c, width), lambda s: (cm(s), 0)),
        out_shape=jax.ShapeDtypeStruct((t, width), BF16 if final else F32),
        scratch_shapes=[pltpu.VMEM((RET_HEADS, RET_DK, RET_DK), F32),
                        pltpu.VMEM((RET_HEADS, c, c), F32),
                        pltpu.VMEM((RET_HEADS, c, RET_DK), F32),
                        pltpu.VMEM((RET_HEADS, c, RET_DK), F32),
                        pltpu.VMEM((RET_HEADS, 1, RET_DK), F32)],
        compiler_params=_cparams(1),
        name="retention_bwd" if rev else "retention_fwd_merge",
    )(*args)


def _rwkv_kernel(*refs, rev, final, c, nsub, nc):
    if final:
        (r_ref, k_ref, v_ref, wd_ref, ad_ref, mu_ref, mus_ref, vec_ref, wup_ref, aup_ref,
         gl_ref, gup_ref, yb_ref, bb_ref, o_ref, st_scr, carry_scr) = refs
    else:
        (r_ref, k_ref, v_ref, wd_ref, ad_ref, mu_ref, mus_ref, vec_ref, wup_ref, aup_ref,
         y_ref, bonus_ref, st_scr, carry_scr) = refs
    d = 1 if rev else 0
    s = pl.program_id(0)
    n = RWKV_N
    wdt = RWKV_W

    @pl.when(s == 0)
    def _():
        st_scr[...] = jnp.zeros_like(st_scr)

    @pl.when(jnp.logical_or(s == 0, s == nc))
    def _():
        carry_scr[...] = jnp.zeros_like(carry_scr)

    rows = nsub * c
    row = lax.broadcasted_iota(jnp.int32, (rows, 1), 0)
    edge = (row == rows - 1) if rev else (row == 0)
    last = rows - 1 if not rev else 0
    keep = c - 1 if not rev else 0

    def shifted(x, lo):
        width = x.shape[1]
        prev = pltpu.roll(x, (rows - 1) if rev else 1, axis=0)
        prev = jnp.where(edge, carry_scr[0:1, lo:lo + width], prev)
        carry_scr[0:1, lo:lo + width] = x[last:last + 1, :]
        return prev

    def mix(x, lo, mu):
        prev = shifted(x, lo)
        return x + (prev - x) * mu

    r = mix(r_ref[...].astype(F32), 0, mu_ref[0:1, 0:wdt])
    k = mix(k_ref[...].astype(F32), wdt, mu_ref[0:1, wdt:2 * wdt])
    v = mix(v_ref[...].astype(F32), 2 * wdt, mu_ref[0:1, 2 * wdt:3 * wdt])
    wd = mix(wd_ref[...].astype(F32), 3 * wdt, mus_ref[0:1, :])[:, d * RWKV_LORA:(d + 1) * RWKV_LORA]
    ad = mix(ad_ref[...].astype(F32), 3 * wdt + LANES, mus_ref[1:2, :])[:, d * RWKV_LORA:(d + 1) * RWKV_LORA]

    w0, a0 = vec_ref[0:1, :], vec_ref[1:2, :]
    k_k, k_a, r_k = vec_ref[2:3, :], vec_ref[3:4, :], vec_ref[4:5, :]
    w_log = -_softplus(-(w0 + _bdot(jnp.tanh(wd), wup_ref[...]))) - 0.5
    lw = -jnp.exp(w_log)
    a = _sigmoid(a0 + _bdot(ad, aup_ref[...]))
    kk = k * k_k
    kk = kk * lax.rsqrt(_seg_sum(kk * kk, n) + L2_EPS)
    k = k * (1.0 + (a - 1.0) * k_a)
    b = kk * a

    ri = lax.broadcasted_iota(jnp.int32, (rows, rows), 0)
    ci = lax.broadcasted_iota(jnp.int32, (rows, rows), 1)
    ri_in, ci_in = jnp.bitwise_and(ri, c - 1), jnp.bitwise_and(ci, c - 1)
    before = (ci_in >= ri_in) if rev else (ci_in <= ri_in)
    cmask = jnp.logical_and(jnp.right_shift(ri, 6) == jnp.right_shift(ci, 6), before)
    cum = _dot_mask_lhs(jnp.where(cmask, 1.0, 0.0).astype(BF16), lw)
    e_pos = jnp.exp(cum)
    e_neg = jnp.exp(-cum)
    r_t = r * e_pos
    a_t = -kk * jnp.exp(cum - lw)
    k_t = k * e_neg
    b_t = b * e_neg
    assert c == n
    pw = 2 * n
    incl2, strict2, eye2, lo_c = _pair_masks(c, rev)
    lo_2c = lax.broadcasted_iota(jnp.int32, (2 * c, pw), 1) < n
    bdiag = _bdiag

    adiag = functools.partial(_bdiag, anti=True)

    def pdot_inv(a2, b2):
        return _dot_inv(a2, bdiag(b2))

    n_pairs = RWKV_HEADS // 2
    subs = list(range(nsub))
    inst = [(u, p) for u in subs for p in range(n_pairs)]
    sl = {(u, p): (slice(u * c, (u + 1) * c), slice(p * pw, (p + 1) * pw)) for u, p in inst}
    ar = [jnp.concatenate([a_t[sl[x]], r_t[sl[x]]], axis=0) for x in inst]
    s_e = [_bdot_nt(jnp.where(lo_2c, y, 0.0), jnp.concatenate([b_t[sl[x]], k_t[sl[x]]], axis=0))
           for y, x in zip(ar, inst)]
    s_o = [_bdot_nt(jnp.where(lo_2c, 0.0, y), jnp.concatenate([k_t[sl[x]], b_t[sl[x]]], axis=0))
           for y, x in zip(ar, inst)]
    a_ab = [jnp.where(strict2, jnp.where(lo_c, e[:c], o[:c]), 0.0) for e, o in zip(s_e, s_o)]
    a_rb = [jnp.where(incl2, jnp.where(lo_c, e[c:], o[c:]), 0.0) for e, o in zip(s_e, s_o)]
    a_kk = [jnp.concatenate([jnp.where(strict2, jnp.where(lo_c, o[:c], e[:c]), 0.0),
                             jnp.where(incl2, jnp.where(lo_c, o[c:], e[c:]), 0.0)], axis=0)
            for e, o in zip(s_e, s_o)]
    tms = _neumann_inverse_packed(a_ab, eye2, c)
    bonus = _seg_sum(r * k * r_k, n) * v
    cum_end = jnp.concatenate([jnp.broadcast_to(cum[u * c + keep:u * c + keep + 1, :], (c, wdt)) for u in range(nsub)],
                              axis=0)
    w_end = jnp.exp(cum_end)
    kw = k_t * w_end
    bw = b_t * w_end
    av = [_bdot(a, adiag(v[sl[x]])) for a, x in zip(a_kk, inst)]
    pp = [_dot_inv(t, jnp.concatenate([bdiag(a_t[sl[x]]), bdiag(y[:c])], axis=1))
          for t, x, y in zip(tms, inst, av)]
    rq_y0 = [_bdot(a, jnp.concatenate([bdiag(y[:, :pw]), bdiag(y[:, pw:])], axis=1)) for a, y in zip(a_rb, pp)]
    m_g = [_bdot_tn(bw[sl[x]], y) for x, y in zip(inst, pp)]
    kv = [_bdot_tn(kw[sl[x]], v[sl[x]]) for x in inst]
    idx = {x: i for i, x in enumerate(inst)}
    y_rows = {}
    for u in (subs[::-1] if rev else subs):
        so = []
        for p in range(n_pairs):
            i = idx[(u, p)]
            rq = r_t[sl[(u, p)]] + rq_y0[i][:, :pw]
            w_row = w_end[u * c:u * c + 1, p * pw:(p + 1) * pw]
            m = (jnp.where(lo_c, m_g[i][:n, :pw], m_g[i][n:, :pw])
                 + jnp.where(eye2, jnp.broadcast_to(w_row, (n, pw)), 0.0))
            so.append(_dot_state(jnp.concatenate([rq, m], axis=0), bdiag(st_scr[p])))
        ys = []
        for p in range(n_pairs):
            i = idx[(u, p)]
            g = jnp.where(lo_c, m_g[i][:n, pw:], m_g[i][n:, pw:]) + jnp.where(lo_c, kv[i][:n], kv[i][n:])
            st_scr[p] = so[p][c:] + g
            ys.append(so[p][:c] + (rq_y0[i][:, pw:] + av[i][c:]))
        y_rows[u] = jnp.concatenate(ys, axis=1)
    y = jnp.concatenate([y_rows[u] for u in subs], axis=0)

    if final:
        ysum = y + yb_ref[...]
        mu_h = _seg_sum(ysum, n) * (1.0 / n)
        yc = ysum - mu_h
        var = _seg_sum(yc * yc, n) * (1.0 / n)
        ln_g, ln_b = vec_ref[5:6, :], vec_ref[6:7, :]
        yn = yc * lax.rsqrt(var + GN_EPS) * ln_g + ln_b + bonus + bb_ref[...]
        gate = _bdot(_sigmoid(gl_ref[...].astype(F32)), gup_ref[...])
        o_ref[...] = (yn * gate).astype(o_ref.dtype)
    else:
        y_ref[...] = y
        bonus_ref[...] = bonus


def _rwkv_call(p, prm, other, rev, n_ctx):
    t = p.shape[0]
    c = RWKV_SUB * RWKV_CHUNK
    n, nc = t // c, n_ctx // c
    wdt = RWKV_W
    d = 1 if rev else 0
    final = other is not None
    cm = lambda s: _chunk_index(s, rev, nc, n)
    base = 4096 // wdt
    full = lambda shape: pl.BlockSpec(shape, lambda s: (0,) * len(shape))
    in_specs = [pl.BlockSpec((c, wdt), lambda s: (cm(s), base)),
                pl.BlockSpec((c, wdt), lambda s: (cm(s), base + 1)),
                pl.BlockSpec((c, wdt), lambda s: (cm(s), base + 2)),
                pl.BlockSpec((c, LANES), lambda s: (cm(s), 7168 // LANES)),
                pl.BlockSpec((c, LANES), lambda s: (cm(s), 7168 // LANES + 1)),
                full((1, 3 * wdt)), full((2, LANES)), full((SUBLANES, wdt)),
                full((RWKV_LORA, wdt)), full((RWKV_LORA, wdt))]
    args = [p, p, p, p, p, prm["mu"][d], prm["mu_small"][d], prm["vecs"][d], prm["w_up"][d], prm["a_up"][d]]
    if final:
        in_specs += [pl.BlockSpec((c, 2 * LANES), lambda s: (cm(s), 7424 // (2 * LANES))),
                     full((2 * LANES, wdt)),
                     pl.BlockSpec((c, wdt), lambda s: (cm(s), 0)),
                     pl.BlockSpec((c, wdt), lambda s: (cm(s), 0))]
        args += [p, prm["g_up"], other[0], other[1]]
        out_specs = pl.BlockSpec((c, wdt), lambda s: (cm(s), 0))
        out_shape = jax.ShapeDtypeStruct((t, wdt), BF16)
    else:
        out_specs = [pl.BlockSpec((c, wdt), lambda s: (cm(s), 0))] * 2
        out_shape = [jax.ShapeDtypeStruct((t, wdt), F32)] * 2
    return pl.pallas_call(
        functools.partial(_rwkv_kernel, rev=rev, final=final, c=RWKV_CHUNK, nsub=RWKV_SUB, nc=nc),
        grid=(n,),
        in_specs=in_specs,
        out_specs=out_specs,
        out_shape=out_shape,
        scratch_shapes=[pltpu.VMEM((RWKV_HEADS // 2, RWKV_N, 2 * RWKV_N), F32),
                        pltpu.VMEM((SUBLANES, 3 * wdt + 2 * LANES), F32)],
        compiler_params=_cparams(1),
        name="rwkv7_bwd" if rev else "rwkv7_fwd_merge",
    )(*args)


def _mlstm_kernel(*refs, rev, final, c):
    if final:
        q_ref, k_ref, v_ref, sm_ref, bias_ref, og_ref, ng_ref, hb_ref, o_ref, ct_scr, m_scr = refs
    else:
        q_ref, k_ref, v_ref, sm_ref, bias_ref, o_ref, ct_scr, m_scr = refs
    d = 1 if rev else 0
    dk, dv = MLSTM_DK, MLSTM_DV
    nh = MLSTM_HEADS

    @pl.when(pl.program_id(0) == 0)
    def _():
        ct_scr[...] = jnp.zeros_like(ct_scr)
        m_scr[...] = jnp.zeros_like(m_scr)

    incl, _ = _order_masks(c, rev)
    keep = 0 if rev else c - 1
    sm = sm_ref[...].astype(F32)
    li = sm + bias_ref[0:1, :]
    lf = _log_sigmoid(sm + bias_ref[1:2, :])
    bcum = _dot_mask_lhs(jnp.where(incl, 1.0, 0.0).astype(BF16), lf)
    bcum_t = bcum.T
    li_t = li.T
    ones_col = jnp.where(lax.broadcasted_iota(jnp.int32, (c, LANES), 1) == 0, 1.0, 0.0).astype(BF16)

    heads = range(nh)
    cis = [d * 2 * nh + h for h in heads]
    b_col = [bcum[:, ci + nh:ci + nh + 1] for ci in cis]
    i_col = [li[:, ci:ci + 1] for ci in cis]
    m_prev = [m_scr[h:h + 1, 0:1] for h in heads]
    qs = [q_ref[:, h * dk:(h + 1) * dk].astype(F32) * (dk ** -0.5) for h in heads]
    ks = [k_ref[:, h * dk:(h + 1) * dk].astype(F32) for h in heads]
    v_aug = [jnp.concatenate([v_ref[:, h * dv:(h + 1) * dv].astype(BF16), ones_col], axis=1) for h in heads]
    qk_raw = [_bdot_nt(q, k) for q, k in zip(qs, ks)]
    q_ct = [_bdot(q, ct_scr[h]) for q, h in zip(qs, heads)]
    d_log = [jnp.where(incl, bc - bcum_t[ci + nh:ci + nh + 1, :] + li_t[ci:ci + 1, :], -jnp.inf)
             for bc, ci in zip(b_col, cis)]
    inter = [bc + mp for bc, mp in zip(b_col, m_prev)]
    m_t = [jnp.maximum(jnp.max(dl, axis=-1, keepdims=True), it) for dl, it in zip(d_log, inter)]
    qk = [x * jnp.exp(dl - mt) for x, dl, mt in zip(qk_raw, d_log, m_t)]
    num_aug = [_bdot(x, va) + jnp.exp(it - mt) * qc for x, va, it, mt, qc in zip(qk, v_aug, inter, m_t, q_ct)]
    b_end = [bc[keep:keep + 1, :] for bc in b_col]
    w_log = [be - bc + ic for be, bc, ic in zip(b_end, b_col, i_col)]
    m_new = [jnp.maximum(be + mp, jnp.max(wl, axis=0, keepdims=True)) for be, mp, wl in zip(b_end, m_prev, w_log)]
    kv = [_bdot_tn(k * jnp.exp(wl - mn), va) for k, wl, mn, va in zip(ks, w_log, m_new, v_aug)]
    for h in heads:
        ct_scr[h] = jnp.exp(b_end[h] + m_prev[h] - m_new[h]) * ct_scr[h] + kv[h]
        m_scr[h:h + 1, :] = jnp.broadcast_to(m_new[h], (1, LANES))
        den = num_aug[h][:, dv:dv + 1]
        hout = num_aug[h][:, :dv] / jnp.maximum(jnp.abs(den), jnp.exp(-m_t[h]))
        sl = slice(h * dv, (h + 1) * dv)
        if final:
            tot = hout + hb_ref[:, sl]
            y = tot * lax.rsqrt(jnp.mean(tot * tot, axis=-1, keepdims=True) + EPS)
            o_ref[:, sl] = (y * ng_ref[0:1, sl] * _sigmoid(og_ref[:, sl].astype(F32))).astype(o_ref.dtype)
        else:
            o_ref[:, sl] = hout


def _mlstm_call(p, bias_rows, norm_g, other, rev, n_ctx):
    t = p.shape[0]
    c = MLSTM_CHUNK
    n, nc = t // c, n_ctx // c
    dk, dv, nh = MLSTM_DK, MLSTM_DV, MLSTM_HEADS
    final = other is not None
    cm = lambda s: _chunk_index(s, rev, nc, n)
    full = lambda shape: pl.BlockSpec(shape, lambda s: (0,) * len(shape))
    in_specs = [pl.BlockSpec((c, nh * dk), lambda s: (cm(s), 0)),
                pl.BlockSpec((c, nh * dk), lambda s: (cm(s), 1)),
                pl.BlockSpec((c, nh * dv), lambda s: (cm(s), 1)),
                pl.BlockSpec((c, LANES), lambda s: (cm(s), 7168 // LANES)),
                full((2, LANES))]
    args = [p, p, p, p, bias_rows]
    if final:
        in_specs += [pl.BlockSpec((c, nh * dv), lambda s: (cm(s), 2)), full((1, nh * dv)),
                     pl.BlockSpec((c, nh * dv), lambda s: (cm(s), 0))]
        args += [p, norm_g.reshape(1, nh * dv), other]
    return pl.pallas_call(
        functools.partial(_mlstm_kernel, rev=rev, final=final, c=c),
        grid=(n,),
        in_specs=in_specs,
        out_specs=pl.BlockSpec((c, nh * dv), lambda s: (cm(s), 0)),
        out_shape=jax.ShapeDtypeStruct((t, nh * dv), BF16 if final else F32),
        scratch_shapes=[pltpu.VMEM((nh, dk, dv + LANES), F32), pltpu.VMEM((SUBLANES, LANES), F32)],
        compiler_params=_cparams(1),
        name="mlstm_bwd" if rev else "mlstm_fwd_merge",
    )(*args)


def _gdn_prep_kernel(xm_ref, xp_ref, xn_ref, cw_ref, sm_ref, gp_ref, q_ref, k_ref, v_ref, gb_ref, *, tm, n_ctx, t_total):
    i = pl.program_id(0)
    x = xm_ref[...].astype(F32)
    rows = i * tm + lax.broadcasted_iota(jnp.int32, (tm, 1), 0)
    first = rows == i * tm
    last = rows == i * tm + tm - 1
    prev = jnp.where(first, xp_ref[HALO_ROWS - 1:HALO_ROWS, :].astype(F32), pltpu.roll(x, 1, axis=0))
    nxt = jnp.where(last, xn_ref[0:1, :].astype(F32), pltpu.roll(x, tm - 1, axis=0))
    prev = jnp.where(jnp.logical_or(rows == 0, rows == n_ctx), 0.0, prev)
    nxt = jnp.where(jnp.logical_or(rows == n_ctx - 1, rows == t_total - 1), 0.0, nxt)
    y = _silu(prev * cw_ref[0:1, :] + x * cw_ref[1:2, :] + nxt * cw_ref[2:3, :])
    w = GDN_HEADS * GDN_DK
    q, k, v = y[:, :w], y[:, w:2 * w], y[:, 2 * w:]
    q_ref[...] = (q * lax.rsqrt(_seg_sum(q * q, GDN_DK) + L2_EPS) * (GDN_DK ** -0.5)).astype(BF16)
    k_ref[...] = (k * lax.rsqrt(_seg_sum(k * k, GDN_DK) + L2_EPS)).astype(BF16)
    v_ref[...] = v.astype(BF16)
    sm = sm_ref[...].astype(F32)
    lane = lax.broadcasted_iota(jnp.int32, sm.shape, 1)
    log_alpha = -jnp.exp(gp_ref[0:1, :]) * _softplus(sm + gp_ref[1:2, :])
    gb_ref[...] = jnp.where(lane < 32, log_alpha, _sigmoid(sm))


def _gdn_prep_call(p, conv_w, gate_params, n_ctx):
    t = p.shape[0]
    tm = 256
    w = GDN_HEADS * GDN_DK
    qkv_blk = 3072 // GDN_QKV
    nb8 = t // HALO_ROWS
    r8 = tm // HALO_ROWS
    spec_o = pl.BlockSpec((tm, w), lambda i: (i, 0))
    return pl.pallas_call(
        functools.partial(_gdn_prep_kernel, tm=tm, n_ctx=n_ctx, t_total=t),
        grid=(t // tm,),
        in_specs=[pl.BlockSpec((tm, GDN_QKV), lambda i: (i, qkv_blk)),
                  pl.BlockSpec((HALO_ROWS, GDN_QKV), lambda i: (jnp.maximum(i * r8 - 1, 0), qkv_blk)),
                  pl.BlockSpec((HALO_ROWS, GDN_QKV), lambda i: (jnp.minimum((i + 1) * r8, nb8 - 1), qkv_blk)),
                  pl.BlockSpec((3, GDN_QKV), lambda i: (0, 0)),
                  pl.BlockSpec((tm, LANES), lambda i: (i, 7168 // LANES)),
                  pl.BlockSpec((2, LANES), lambda i: (0, 0))],
        out_specs=[spec_o, spec_o, spec_o, pl.BlockSpec((tm, LANES), lambda i: (i, 0))],
        out_shape=[jax.ShapeDtypeStruct((t, w), BF16)] * 3 + [jax.ShapeDtypeStruct((t, LANES), F32)],
        compiler_params=_cparams(1),
        name="gdn_conv_norm_gates",
    )(p, p, p, conv_w, p, gate_params)


def _gdn_kernel(*refs, rev, final, c, nsub):
    if final:
        q_ref, k_ref, v_ref, gb_ref, gate_ref, ng_ref, ob_ref, o_ref, st_scr = refs
    else:
        q_ref, k_ref, v_ref, gb_ref, o_ref, st_scr = refs
    d = 1 if rev else 0
    dk = GDN_DK
    nh = GDN_HEADS
    rows_all = nsub * c

    @pl.when(pl.program_id(0) == 0)
    def _():
        st_scr[...] = jnp.zeros_like(st_scr)

    keep = 0 if rev else c - 1
    gb = gb_ref[...]
    ri = lax.broadcasted_iota(jnp.int32, (rows_all, rows_all), 0)
    ci = lax.broadcasted_iota(jnp.int32, (rows_all, rows_all), 1)
    ri_in, ci_in = jnp.bitwise_and(ri, c - 1), jnp.bitwise_and(ci, c - 1)
    before = (ci_in >= ri_in) if rev else (ci_in <= ri_in)
    cmask = jnp.logical_and(jnp.right_shift(ri, 6) == jnp.right_shift(ci, 6), before)
    gc = _dot_mask_lhs(jnp.where(cmask, 1.0, 0.0).astype(BF16), gb)
    gc_t = gc.T
    incl2, strict2, eye2, lo_c = _pair_masks(c, rev)
    bdiag = _bdiag
    zeros = jnp.zeros((c, 2 * dk), F32)

    subs = list(range(nsub))
    inst = [(u, h) for u in subs for h in range(nh)]
    rs = {u: slice(u * c, (u + 1) * c) for u in subs}
    sls = [slice(h * dk, (h + 1) * dk) for h in range(nh)]
    cols = [16 + d * nh + h for h in range(nh)]
    g_col = {(u, h): gc[rs[u], cols[h]:cols[h] + 1] for u, h in inst}
    beta = {(u, h): gb[rs[u], cols[h] + 16:cols[h] + 17] for u, h in inst}
    g_end = {x: g_col[x][keep:keep + 1, :] for x in inst}
    e_g = {x: jnp.exp(g_col[x]) for x in inst}
    qs = {(u, h): q_ref[rs[u], sls[h]].astype(F32) for u, h in inst}
    ks = {(u, h): k_ref[rs[u], sls[h]].astype(F32) for u, h in inst}
    vs = {(u, h): v_ref[rs[u], sls[h]].astype(F32) for u, h in inst}

    pairs = [(u, a) for u in subs for a in range(0, nh, 2)]
    kq_e, kq_o, decay2 = [], [], []
    for u, a in pairs:
        e, o = (u, a), (u, a + 1)
        kk2 = jnp.concatenate([ks[e], ks[o]], axis=0)
        kq_e.append(_bdot_nt(jnp.concatenate([ks[e] * beta[e], qs[e]], axis=0), kk2))
        kq_o.append(_bdot_nt(jnp.concatenate([ks[o] * beta[o], qs[o]], axis=0), kk2))
        g_row2 = jnp.concatenate([gc_t[cols[a]:cols[a] + 1, rs[u]], gc_t[cols[a + 1]:cols[a + 1] + 1, rs[u]]], axis=1)
        decay2.append(jnp.exp(jnp.where(incl2, jnp.where(lo_c, g_col[e], g_col[o]) - g_row2, -jnp.inf)))
    l2 = [jnp.where(strict2, -(jnp.where(lo_c, e[:c], o[:c]) * dc), 0.0) for e, o, dc in zip(kq_e, kq_o, decay2)]
    a_qk2 = [jnp.where(lo_c, e[c:], o[c:]) * dc for e, o, dc in zip(kq_e, kq_o, decay2)]
    tinv2 = _neumann_inverse_packed(l2, eye2, c)
    rhs2 = []
    for u, a in pairs:
        e, o = (u, a), (u, a + 1)
        r_e = jnp.concatenate([vs[e] * beta[e], ks[e] * (beta[e] * e_g[e]), zeros], axis=1)
        r_o = jnp.concatenate([zeros, vs[o] * beta[o], ks[o] * (beta[o] * e_g[o])], axis=1)
        rhs2.append(jnp.concatenate([r_e, r_o], axis=0))
    uw2 = [_dot_inv(t, r) for t, r in zip(tinv2, rhs2)]
    op2 = []
    for aq, x in zip(a_qk2, uw2):
        x_bd = jnp.concatenate([jnp.concatenate([x[:, :2 * dk], zeros], axis=1),
                                jnp.concatenate([zeros, x[:, 2 * dk:]], axis=1)], axis=0)
        op2.append(_bdot(aq, x_bd))
    uw, o_part = {}, {}
    for (u, a), x, o2 in zip(pairs, uw2, op2):
        uw[(u, a)], uw[(u, a + 1)] = x[:, :2 * dk], x[:, 2 * dk:]
        o_part[(u, a)], o_part[(u, a + 1)] = o2[:, :2 * dk], o2[:, 2 * dk:]
    s_part = {x: _bdot_tn(ks[x] * jnp.exp(g_end[x] - g_col[x]), uw[x]) for x in inst}

    for u in (subs[::-1] if rev else subs):
        so = [_dot_state(jnp.concatenate([qs[(u, h)] * e_g[(u, h)] - o_part[(u, h)][:, dk:], s_part[(u, h)][:, dk:]], axis=0),
                         st_scr[h]) for h in range(nh)]
        for h in range(nh):
            x = (u, h)
            out = so[h][:c] + o_part[x][:, :dk]
            st_scr[h] = jnp.exp(g_end[x]) * st_scr[h] - so[h][c:] + s_part[x][:, :dk]
            if final:
                tot = out + ob_ref[rs[u], sls[h]]
                y = tot * lax.rsqrt(jnp.mean(tot * tot, axis=-1, keepdims=True) + EPS)
                o_ref[rs[u], sls[h]] = (y * ng_ref[0:1, sls[h]] * _silu(gate_ref[rs[u], sls[h]].astype(F32))).astype(o_ref.dtype)
            else:
                o_ref[rs[u], sls[h]] = out


def _gdn_call(p, q, k, v, gb, norm_g, other, rev, n_ctx):
    t = q.shape[0]
    c = GDN_SUB * GDN_CHUNK
    n, nc = t // c, n_ctx // c
    w = GDN_HEADS * GDN_DK
    final = other is not None
    cm = lambda s: _chunk_index(s, rev, nc, n)
    blk = pl.BlockSpec((c, w), lambda s: (cm(s), 0))
    in_specs = [blk, blk, blk, pl.BlockSpec((c, LANES), lambda s: (cm(s), 0))]
    args = [q, k, v, gb]
    if final:
        in_specs += [pl.BlockSpec((c, w), lambda s: (cm(s), 6144 // w)), pl.BlockSpec((1, w), lambda s: (0, 0)), blk]
        args += [p, norm_g.reshape(1, w), other]
    return pl.pallas_call(
        functools.partial(_gdn_kernel, rev=rev, final=final, c=GDN_CHUNK, nsub=GDN_SUB),
        grid=(n,),
        in_specs=in_specs,
        out_specs=blk,
        out_shape=jax.ShapeDtypeStruct((t, w), BF16 if final else F32),
        scratch_shapes=[pltpu.VMEM((GDN_HEADS, GDN_DK, GDN_DK), F32)],
        compiler_params=_cparams(1),
        name="gdn_bwd" if rev else "gdn_fwd_merge",
    )(*args)


def _pad_cols(w, width):
    return jnp.pad(w, ((0, 0), (0, width - w.shape[1])))


def _lane_row(pieces):
    row = jnp.zeros((LANES,), F32)
    for off, vec in pieces:
        row = row.at[off:off + vec.shape[0]].set(vec.astype(F32))
    return row


def _rwkv_params(e, rwkv_mu, rwkv_w0, rwkv_w_up, rwkv_a0, rwkv_a_up, rwkv_k_k, rwkv_k_a, rwkv_r_k,
                 rwkv_ln_g, rwkv_ln_b, rwkv_g_up):
    wdt = RWKV_W
    mu = rwkv_mu[e]
    zeros64 = jnp.zeros((RWKV_LORA,), F32)
    mu_small, vecs = [], []
    for d in range(2):
        m_wd = mu[d, 3 * wdt:3 * wdt + RWKV_LORA]
        m_ad = mu[d, 3 * wdt + RWKV_LORA:]
        lo = [m_wd, zeros64] if d == 0 else [zeros64, m_wd]
        la = [m_ad, zeros64] if d == 0 else [zeros64, m_ad]
        mu_small.append(jnp.stack([jnp.concatenate(lo), jnp.concatenate(la)]))
        vecs.append(jnp.stack([rwkv_w0[e, d], rwkv_a0[e, d], rwkv_k_k[e], rwkv_k_a[e], rwkv_r_k[e].reshape(wdt),
                               rwkv_ln_g[e], rwkv_ln_b[e], jnp.zeros((wdt,), F32)]))
    return {
        "mu": mu[:, None, :3 * wdt],
        "mu_small": jnp.stack(mu_small),
        "vecs": jnp.stack(vecs),
        "w_up": rwkv_w_up[e].astype(BF16),
        "a_up": rwkv_a_up[e].astype(BF16),
        "g_up": jnp.pad(rwkv_g_up[e], ((0, 2 * LANES - RWKV_GATE_LORA), (0, 0))).astype(BF16),
    }


def _even_mixer(p, e, n_ctx, ret_decay_logit, rwkv_prm):
    logit = ret_decay_logit[e].astype(F32)
    ret_b = _retention_call(p, logit, None, True, n_ctx)
    y_ret = _retention_call(p, logit, ret_b, False, n_ctx)
    rw_b = _rwkv_call(p, rwkv_prm, None, True, n_ctx)
    y_rwkv = _rwkv_call(p, rwkv_prm, rw_b, False, n_ctx)
    return y_ret, y_rwkv


def _odd_mixer(p, o, n_ctx, mlstm_gate_b, mlstm_norm_g, gdn_conv, gdn_a_log, gdn_dt_bias, gdn_norm_g):
    gate_b = mlstm_gate_b[o]
    nh = MLSTM_HEADS
    bias_i = _lane_row([(d * 2 * nh, gate_b[d, 0]) for d in range(2)])
    bias_f = _lane_row([(d * 2 * nh + nh, gate_b[d, 1]) for d in range(2)])
    bias_rows = jnp.stack([bias_i, bias_f])
    ml_b = _mlstm_call(p, bias_rows, mlstm_norm_g[o], None, True, n_ctx)
    y_ml = _mlstm_call(p, bias_rows, mlstm_norm_g[o], ml_b, False, n_ctx)

    gate_params = jnp.stack([_lane_row([(16, gdn_a_log[o].reshape(-1))]), _lane_row([(16, gdn_dt_bias[o].reshape(-1))])])
    q, k, v, gb = _gdn_prep_call(p, gdn_conv[o], gate_params, n_ctx)
    gd_b = _gdn_call(p, q, k, v, gb, gdn_norm_g[o], None, True, n_ctx)
    y_gd = _gdn_call(p, q, k, v, gb, gdn_norm_g[o], gd_b, False, n_ctx)
    return y_ml, y_gd


def kernel(x, c, ctx, c_ctx, ada_w, ada_b, norm1_g, norm2_g, mix_w_out, ffn_w_in, ffn_dw, ffn_dw_b, ffn_w_out, final_norm_g, ev_w_in, ret_decay_logit, rwkv_mu, rwkv_w0, rwkv_w_up, rwkv_a0, rwkv_a_up, rwkv_k_k, rwkv_k_a, rwkv_r_k, rwkv_ln_g, rwkv_ln_b, rwkv_g_up, od_w_in, mlstm_gate_b, mlstm_norm_g, gdn_conv, gdn_a_log, gdn_dt_bias, gdn_norm_g):
    assert x.shape[0] == 1 and ctx.shape[0] == 1
    n_ctx = ctx.shape[1]
    n_lat = x.shape[1]
    depth = ada_w.shape[0]
    cvecs = jnp.stack([c_ctx, c[0]], axis=1)
    mod = _ada_call(cvecs, ada_w, ada_b)
    xt, h1 = _assemble_call(ctx, x, norm1_g[0], mod)

    for l in range(depth):
        last = l == depth - 1
        half = D_MODEL // 2
        if l % 2 == 0:
            e = l // 2
            p = _proj_call(h1, ev_w_in[e], F_PAD)
            prm = _rwkv_params(e, rwkv_mu, rwkv_w0, rwkv_w_up, rwkv_a0, rwkv_a_up, rwkv_k_k, rwkv_k_a, rwkv_r_k,
                               rwkv_ln_g, rwkv_ln_b, rwkv_g_up)
            ya, yb = _even_mixer(p, e, n_ctx, ret_decay_logit, prm)
        else:
            o = l // 2
            p = _proj_odd_call(h1, od_w_in, o)
            ya, yb = _odd_mixer(p, o, n_ctx, mlstm_gate_b, mlstm_norm_g, gdn_conv, gdn_a_log, gdn_dt_bias, gdn_norm_g)
        w_mix = mix_w_out[l].astype(BF16)
        xt, h2 = _outproj_call(ya, yb, w_mix[:half], w_mix[half:], xt, norm2_g[l], mod, l, n_ctx)
        gate, up, wo = _ffn_in_call(h2, ffn_w_in, ffn_w_out, l)
        dw9 = _pad_cols(ffn_dw[l].reshape(9, D_FF), FF_PAD)
        dwb = _pad_cols(ffn_dw_b[l].reshape(1, D_FF), FF_PAD)
        if last:
            xt = _ffn_out_call(gate, up, dw9, dwb, wo, xt, mod, l, final_norm_g, n_ctx, True)
        else:
            xt, h1 = _ffn_out_call(gate, up, dw9, dwb, wo, xt, mod, l, norm1_g[l + 1], n_ctx, False)
    return xt[n_ctx:][None]
```

```python
import functools
import math

import jax
import jax.numpy as jnp
from jax import lax
from jax.experimental import pallas as pl
from jax.experimental.pallas import tpu as pltpu

F32 = jnp.float32
BF16 = jnp.bfloat16

D_MODEL = 2048
DEPTH = 2
GRID_W = 64
EPS = 1e-6
GN_EPS = 64e-5
L2_EPS = 1e-12
D_FF = 5504

RET_HEADS = 8
RET_DK = 128
RWKV_HEADS = 16
RWKV_N = 64
RWKV_W = RWKV_HEADS * RWKV_N
RWKV_LORA = 64
RWKV_GATE_LORA = 160
MLSTM_HEADS = 4
MLSTM_DK = 128
MLSTM_DV = 256
GDN_HEADS = 8
GDN_DK = 128
GDN_QKV = 3072

LANES = 128
SUBLANES = 8
MXU_N = 256
HALO_ROWS = 16
P_DTYPE = jnp.bfloat16
VMEM_LIMIT = 56 * 1024 * 1024

F_PAD = 7680
FF_PAD = 5632
ROW_TILE = 768
WIDE_ROW_TILE = 1408
RET_CHUNK = 256
MLSTM_CHUNK = 256
GDN_CHUNK = 64
RWKV_CHUNK = 64
GDN_SUB = 4
RWKV_SUB = 2


def _wide_tile(t):
    return WIDE_ROW_TILE if t % WIDE_ROW_TILE == 0 else ROW_TILE


def _cparams(n_axes):
    return pltpu.CompilerParams(dimension_semantics=("arbitrary",) * n_axes, vmem_limit_bytes=VMEM_LIMIT)


def _bdot(a, b):
    return jnp.dot(a.astype(BF16), b.astype(BF16), preferred_element_type=F32)


def _bdot_nt(a, b):
    return lax.dot_general(a.astype(BF16), b.astype(BF16), (((1,), (1,)), ((), ())), preferred_element_type=F32)


def _bdot_tn(a, b):
    return lax.dot_general(a.astype(BF16), b.astype(BF16), (((0,), (0,)), ((), ())), preferred_element_type=F32)


def _split3(x):
    x1 = x.astype(BF16)
    r1 = x - x1.astype(F32)
    x2 = r1.astype(BF16)
    x3 = (r1 - x2.astype(F32)).astype(BF16)
    return x1, x2, x3


def _dot_mask_lhs(m_bf16, x):
    x1, x2, x3 = _split3(x)
    d = lambda t: jnp.dot(m_bf16, t, preferred_element_type=F32)
    return (d(x3) + d(x2)) + d(x1)


def _dot3(a, b):
    a1 = a.astype(BF16)
    a2 = (a - a1.astype(F32)).astype(BF16)
    b1 = b.astype(BF16)
    b2 = (b - b1.astype(F32)).astype(BF16)
    d = lambda u, v: jnp.dot(u, v, preferred_element_type=F32)
    return (d(a2, b1) + d(a1, b2)) + d(a1, b1)


def _sigmoid(x):
    return 1.0 / (1.0 + jnp.exp(-x))


def _silu(x):
    return x * _sigmoid(x)


def _softplus(x):
    return jnp.maximum(x, 0.0) + jnp.log1p(jnp.exp(-jnp.abs(x)))


def _log_sigmoid(x):
    return -_softplus(-x)


def _order_masks(c, rev):
    i = lax.broadcasted_iota(jnp.int32, (c, c), 0)
    j = lax.broadcasted_iota(jnp.int32, (c, c), 1)
    if rev:
        return j >= i, j > i
    return j <= i, j < i


def _neumann_inverse(n, c, dotf):
    i = lax.broadcasted_iota(jnp.int32, (c, c), 0)
    j = lax.broadcasted_iota(jnp.int32, (c, c), 1)
    x = jnp.where(i == j, 1.0, 0.0).astype(F32) + n
    p = dotf(n, n)
    for _ in range(int(math.log2(c)) - 2):
        r = dotf(jnp.concatenate([x, p], axis=0), p)
        x = x + r[:c]
        p = r[c:]
    return x + dotf(x, p)


def _neumann_inverse_multi(ns, c, dotf):
    i = lax.broadcasted_iota(jnp.int32, (c, c), 0)
    j = lax.broadcasted_iota(jnp.int32, (c, c), 1)
    eye = jnp.where(i == j, 1.0, 0.0).astype(F32)
    xs = [eye + n for n in ns]
    ps = [dotf(n, n) for n in ns]
    for _ in range(int(math.log2(c)) - 2):
        rs = [dotf(jnp.concatenate([x, p], axis=0), p) for x, p in zip(xs, ps)]
        xs = [x + r[:c] for x, r in zip(xs, rs)]
        ps = [r[c:] for r in rs]
    return [x + dotf(x, p) for x, p in zip(xs, ps)]


def _pair_masks(c, rev):
    assert 2 * c == LANES and c == 64
    row = lax.broadcasted_iota(jnp.int32, (c, 2 * c), 0)
    lane = lax.broadcasted_iota(jnp.int32, (c, 2 * c), 1)
    src = jnp.bitwise_and(lane, c - 1)
    incl2 = (src >= row) if rev else (src <= row)
    strict2 = (src > row) if rev else (src < row)
    return incl2, strict2, src == row, lane < c


def _bdiag(x2, anti=False):
    n2 = x2.shape[1]
    blk_r = jnp.right_shift(lax.broadcasted_iota(jnp.int32, (n2, n2), 0), 6)
    blk_l = jnp.right_shift(lax.broadcasted_iota(jnp.int32, (n2, n2), 1), 6)
    stacked = jnp.concatenate([x2, x2], axis=0)
    if anti:
        return jnp.where(blk_r == blk_l, 0.0, stacked)
    return jnp.where(blk_r == blk_l, stacked, 0.0)


def _split2(x):
    hi = x.astype(BF16)
    return hi, (x - hi.astype(F32)).astype(BF16)


def _dot3_split(a, b):
    d = lambda u, v: jnp.dot(u, v, preferred_element_type=F32)
    return (d(a[1], b[0]) + d(a[0], b[1])) + d(a[0], b[0])


def _neumann_inverse_packed(ns, eye2, c):
    n2 = 2 * c
    blk_r = jnp.right_shift(lax.broadcasted_iota(jnp.int32, (n2, n2), 0), 6)
    blk_l = jnp.right_shift(lax.broadcasted_iota(jnp.int32, (n2, n2), 1), 6)
    pattern = jnp.where(blk_r == blk_l, 1.0, 0.0).astype(BF16)
    bd = lambda s: tuple(jnp.concatenate([h, h], axis=0) * pattern for h in s)
    cat = lambda s, t: tuple(jnp.concatenate([u, v], axis=0) for u, v in zip(s, t))
    xs = [jnp.where(eye2, 1.0, 0.0) + n for n in ns]
    nsp = [_split2(n) for n in ns]
    ps = [_dot3_split(s, bd(s)) for s in nsp]
    for _ in range(int(math.log2(c)) - 2):
        psp = [_split2(p) for p in ps]
        xsp = [_split2(x) for x in xs]
        rs = [_dot3_split(cat(x, p), bd(p)) for x, p in zip(xsp, psp)]
        xs = [x + r[:c] for x, r in zip(xs, rs)]
        ps = [r[c:] for r in rs]
    return [x + _dot3_split(_split2(x), bd(_split2(p))) for x, p in zip(xs, ps)]


def _dot_inv(a, b):
    return _dot3(a, b)


def _dot_state(a, b):
    return _bdot(a, b)


def _chunk_index(s, rev, n_ctx_chunks, n_chunks):
    if not rev:
        return s
    return jnp.where(s < n_ctx_chunks, n_ctx_chunks - 1 - s, n_chunks + n_ctx_chunks - 1 - s)


def _seg_sum(x, seg):
    c, w = x.shape
    if seg == LANES:
        parts = [jnp.broadcast_to(jnp.sum(x[:, b * LANES:(b + 1) * LANES], axis=-1, keepdims=True), (c, LANES))
                 for b in range(w // LANES)]
        return jnp.concatenate(parts, axis=-1)
    assert seg * 2 == LANES
    lane = lax.broadcasted_iota(jnp.int32, (c, LANES), 1)
    low = lane < seg
    parts = []
    for b in range(w // LANES):
        xb = x[:, b * LANES:(b + 1) * LANES]
        s_lo = jnp.sum(jnp.where(low, xb, 0.0), axis=-1, keepdims=True)
        s_hi = jnp.sum(jnp.where(low, 0.0, xb), axis=-1, keepdims=True)
        parts.append(jnp.where(low, s_lo, s_hi))
    return jnp.concatenate(parts, axis=-1)


def _ada_kernel(c_ref, w_ref, b_ref, o_ref):
    cv = c_ref[...]
    s = _silu(cv)
    w = w_ref[0]
    bias = b_ref[0]
    r0 = jnp.sum(w * s[:, 0:1], axis=0, keepdims=True) + bias
    r1 = jnp.sum(w * s[:, 1:2], axis=0, keepdims=True) + bias
    row = lax.broadcasted_iota(jnp.int32, (SUBLANES, w.shape[1]), 0)
    o_ref[0] = jnp.where(row == 0, r0, jnp.where(row == 1, r1, 0.0))


def _ada_call(cvecs, ada_w, ada_b):
    depth, d, n = ada_w.shape
    tn = 1024
    return pl.pallas_call(
        _ada_kernel,
        grid=(depth, n // tn),
        in_specs=[pl.BlockSpec((d, 2), lambda l, j: (0, 0)),
                  pl.BlockSpec((1, d, tn), lambda l, j: (l, 0, j)),
                  pl.BlockSpec((1, 1, tn), lambda l, j: (l, 0, j))],
        out_specs=pl.BlockSpec((1, SUBLANES, tn), lambda l, j: (l, 0, j)),
        out_shape=jax.ShapeDtypeStruct((depth, SUBLANES, n), F32),
        compiler_params=_cparams(2),
        name="ada_modulation",
    )(cvecs, ada_w, ada_b.reshape(depth, 1, n))


def _mod_rows(mod, k, rows, n_ctx):
    d = D_MODEL
    vc = mod[0:1, k * d:(k + 1) * d]
    vl = mod[1:2, k * d:(k + 1) * d]
    return jnp.where(rows < n_ctx, vc, vl)


def _norm_mod(x, g, shift, scale):
    y = x * lax.rsqrt(jnp.mean(x * x, axis=-1, keepdims=True) + EPS) * g
    return y * (1.0 + scale) + shift


def _assemble_kernel(ctx_ref, x_ref, g_ref, mod_ref, xt_ref, h_ref):
    i = pl.program_id(0)
    d = D_MODEL
    mod = mod_ref[0]

    def emit(src, row):
        xt_ref[...] = src
        h = _norm_mod(src, g_ref[...], mod[row:row + 1, 0:d], mod[row:row + 1, d:2 * d])
        h_ref[...] = h.astype(BF16)

    @pl.when(i == 0)
    def _():
        emit(ctx_ref[0], 0)

    @pl.when(i > 0)
    def _():
        emit(x_ref[0], 1)


def _assemble_call(ctx, x, g, mod):
    n_ctx, d = ctx.shape[1], ctx.shape[2]
    t = n_ctx + x.shape[1]
    assert x.shape[1] % n_ctx == 0
    spec_o = pl.BlockSpec((n_ctx, d), lambda i: (i, 0))
    return pl.pallas_call(
        _assemble_kernel,
        grid=(t // n_ctx,),
        in_specs=[pl.BlockSpec((1, n_ctx, d), lambda i: (0, 0, 0)),
                  pl.BlockSpec((1, n_ctx, d), lambda i: (0, jnp.maximum(i - 1, 0), 0)),
                  pl.BlockSpec((1, d), lambda i: (0, 0)),
                  pl.BlockSpec((1, SUBLANES, 6 * d), lambda i: (0, 0, 0))],
        out_specs=[spec_o, spec_o],
        out_shape=[jax.ShapeDtypeStruct((t, d), F32), jax.ShapeDtypeStruct((t, d), BF16)],
        compiler_params=_cparams(1),
        name="assemble_norm1",
    )(ctx, x, g.reshape(1, d), mod)


def _proj_kernel(h_ref, w_ref, o_ref, w_scr, *, tn, n_valid):
    @pl.when(pl.program_id(1) == 0)
    def _():
        w = w_ref[...]
        col = pl.program_id(0) * tn + lax.broadcasted_iota(jnp.int32, w.shape, 1)
        w_scr[...] = jnp.where(col < n_valid, w, 0).astype(BF16)

    o_ref[...] = jnp.dot(h_ref[...], w_scr[...], preferred_element_type=F32).astype(o_ref.dtype)


def _proj_call(h, w, f_pad):
    t, d = h.shape
    tm, tn = _wide_tile(t), 768
    return pl.pallas_call(
        functools.partial(_proj_kernel, tn=tn, n_valid=w.shape[1]),
        grid=(f_pad // tn, t // tm),
        in_specs=[pl.BlockSpec((tm, d), lambda j, i: (i, 0)),
                  pl.BlockSpec((d, tn), lambda j, i: (0, j))],
        out_specs=pl.BlockSpec((tm, tn), lambda j, i: (i, j)),
        out_shape=jax.ShapeDtypeStruct((t, f_pad), P_DTYPE),
        scratch_shapes=[pltpu.VMEM((d, tn), BF16)],
        compiler_params=_cparams(2),
        name="in_proj",
    )(h, w)


ODD_GATES_AT = 3072
ODD_GATES_W = 16
ODD_SMALL_AT = 7168
ODD_WIDTH = 7216


def _proj_odd_kernel(h_ref, w_ref, wn_ref, wg_ref, o_ref, w_scr, *, tn):
    j = pl.program_id(0)
    first_shift = ODD_GATES_AT // tn
    last = ODD_SMALL_AT // tn
    assert ODD_GATES_AT % tn == 0 and (last + 1) * tn == F_PAD

    @pl.when(pl.program_id(1) == 0)
    def _():
        w = w_ref[0]
        wide = jnp.concatenate([w, wn_ref[0]], axis=1)
        shifted = pltpu.roll(wide, wide.shape[1] - ODD_GATES_W, axis=1)[:, :tn]
        lane = lax.broadcasted_iota(jnp.int32, w.shape, 1)

        @pl.when(j < first_shift)
        def _():
            w_scr[...] = w.astype(BF16)

        @pl.when(jnp.logical_and(j >= first_shift, j < last))
        def _():
            w_scr[...] = shifted.astype(BF16)

        @pl.when(j == last)
        def _():
            q0 = ODD_SMALL_AT - last * tn
            gates = jnp.concatenate([wg_ref[0]] * (tn // LANES), axis=1)
            small = jnp.where(lane < q0 + ODD_GATES_W, gates, w)
            tile = jnp.where(lane < q0, shifted, jnp.where(lane < q0 + (ODD_WIDTH - ODD_SMALL_AT), small, 0.0))
            w_scr[...] = tile.astype(BF16)

    o_ref[...] = jnp.dot(h_ref[...], w_scr[...], preferred_element_type=F32).astype(o_ref.dtype)


def _proj_odd_call(h, od_w_in, o):
    t, d = h.shape
    tm, tn = ROW_TILE, 768
    assert od_w_in.shape[2] == ODD_WIDTH and ODD_SMALL_AT % LANES == 0 and ODD_GATES_AT % LANES == 0
    per = tn // LANES
    last_lane_tile = (ODD_WIDTH - 1) // LANES
    return pl.pallas_call(
        functools.partial(_proj_odd_kernel, tn=tn),
        grid=(F_PAD // tn, t // tm),
        in_specs=[pl.BlockSpec((tm, d), lambda j, i: (i, 0)),
                  pl.BlockSpec((1, d, tn), lambda j, i: (o, 0, j)),
                  pl.BlockSpec((1, d, LANES), lambda j, i: (o, 0, jnp.minimum(per * (j + 1), last_lane_tile))),
                  pl.BlockSpec((1, d, LANES), lambda j, i: (o, 0, ODD_GATES_AT // LANES))],
        out_specs=pl.BlockSpec((tm, tn), lambda j, i: (i, j)),
        out_shape=jax.ShapeDtypeStruct((t, F_PAD), P_DTYPE),
        scratch_shapes=[pltpu.VMEM((d, tn), BF16)],
        compiler_params=_cparams(2),
        name="in_proj_odd",
    )(h, od_w_in, od_w_in, od_w_in)


def _outproj_kernel(ya_ref, yb_ref, wa_ref, wb_ref, x_ref, g_ref, mod_ref, xo_ref, h_ref, *, tm, n_ctx):
    i = pl.program_id(0)
    mod = mod_ref[0]
    rb = tm // 2
    for r0 in range(0, tm, rb):
        rs = slice(r0, r0 + rb)
        rows = i * tm + r0 + lax.broadcasted_iota(jnp.int32, (rb, 1), 0)
        acc = jnp.dot(ya_ref[rs, :], wa_ref[...], preferred_element_type=F32)
        acc = acc + jnp.dot(yb_ref[rs, :], wb_ref[...], preferred_element_type=F32)
        xn = x_ref[rs, :] + _mod_rows(mod, 2, rows, n_ctx) * acc
        xo_ref[rs, :] = xn
        h = _norm_mod(xn, g_ref[...], _mod_rows(mod, 3, rows, n_ctx), _mod_rows(mod, 4, rows, n_ctx))
        h_ref[rs, :] = h.astype(BF16)


def _outproj_call(ya, yb, wa, wb, x, g2, mod, layer, n_ctx):
    t, d = x.shape
    half = ya.shape[1]
    tm = 384
    return pl.pallas_call(
        functools.partial(_outproj_kernel, tm=tm, n_ctx=n_ctx),
        grid=(t // tm,),
        in_specs=[pl.BlockSpec((tm, half), lambda i: (i, 0)),
                  pl.BlockSpec((tm, half), lambda i: (i, 0)),
                  pl.BlockSpec((half, d), lambda i: (0, 0)),
                  pl.BlockSpec((half, d), lambda i: (0, 0)),
                  pl.BlockSpec((tm, d), lambda i: (i, 0)),
                  pl.BlockSpec((1, d), lambda i: (0, 0)),
                  pl.BlockSpec((1, SUBLANES, 6 * d), lambda i: (layer, 0, 0))],
        out_specs=[pl.BlockSpec((tm, d), lambda i: (i, 0)),
                   pl.BlockSpec((tm, d), lambda i: (i, 0))],
        out_shape=[jax.ShapeDtypeStruct((t, d), F32), jax.ShapeDtypeStruct((t, d), BF16)],
        compiler_params=_cparams(1),
        name="mix_out_proj_norm2",
    )(ya, yb, wa, wb, x, g2.reshape(1, d), mod)


def _ffn_in_kernel(h_ref, wg_ref, wu0_ref, wu1_ref, wu2_ref, wu3_ref, wo_ref, g_ref, u_ref, wob_ref, w_scr, *, tn):
    @pl.when(pl.program_id(1) == 0)
    def _():
        j = pl.program_id(0)
        wu = jnp.concatenate([wu0_ref[0], wu1_ref[0], wu2_ref[0], wu3_ref[0]], axis=1)
        col = j * tn + lax.broadcasted_iota(jnp.int32, wu.shape, 1)
        w_scr[0] = jnp.where(col < D_FF, wg_ref[0], 0.0).astype(BF16)
        w_scr[1] = jnp.where(col < D_FF, wu, 0.0).astype(BF16)
        wo = wo_ref[0]
        row = j * tn + lax.broadcasted_iota(jnp.int32, wo.shape, 0)
        wob_ref[...] = jnp.where(row < D_FF, 0.5 * wo, 0.0).astype(BF16)

    h = h_ref[...]
    g_ref[...] = jnp.dot(h, w_scr[0], preferred_element_type=F32).astype(BF16)
    u_ref[...] = jnp.dot(h, w_scr[1], preferred_element_type=F32).astype(BF16)


def _ffn_in_call(h, ffn_w_in, ffn_w_out, layer):
    t, d = h.shape
    tm, tn = ROW_TILE, 512
    q = tn // LANES
    assert D_FF % LANES == 0 and q == 4
    up0 = D_FF // LANES
    last = 2 * D_FF // LANES - 1
    spec_o = pl.BlockSpec((tm, tn), lambda j, i: (i, j))
    up_specs = [pl.BlockSpec((1, d, LANES), lambda j, i, r=r: (layer, 0, jnp.minimum(up0 + q * j + r, last)))
                for r in range(q)]
    return pl.pallas_call(
        functools.partial(_ffn_in_kernel, tn=tn),
        grid=(FF_PAD // tn, t // tm),
        in_specs=[pl.BlockSpec((tm, d), lambda j, i: (i, 0)),
                  pl.BlockSpec((1, d, tn), lambda j, i: (layer, 0, j))] + up_specs
                 + [pl.BlockSpec((1, tn, d), lambda j, i: (layer, j, 0))],
        out_specs=[spec_o, spec_o, pl.BlockSpec((tn, d), lambda j, i: (j, 0))],
        out_shape=[jax.ShapeDtypeStruct((t, FF_PAD), BF16)] * 2 + [jax.ShapeDtypeStruct((FF_PAD, d), BF16)],
        scratch_shapes=[pltpu.VMEM((2, d, tn), BF16)],
        compiler_params=_cparams(2),
        name="ffn_in_proj",
    )(h, ffn_w_in, ffn_w_in, ffn_w_in, ffn_w_in, ffn_w_in, ffn_w_out)


def _glu_act(conv, bias, up):
    gate = conv + bias
    return (gate * (1.0 + lax.erf(gate * (2.0 ** -0.5))) * up.astype(F32)).astype(BF16)


def _ffn_out_kernel(*refs, tm, tk, n_ctx, t_total, final_norm):
    if final_norm:
        (gm_ref, gp_ref, gn_ref, u_ref, dw_ref, db_ref, wo_ref, x_ref, mod_ref, fg_ref,
         o_ref, acc_scr, act_a, act_b) = refs
    else:
        (gm_ref, gp_ref, gn_ref, u_ref, dw_ref, db_ref, wo_ref, x_ref, mod_ref, fg_ref, nmod_ref,
         o_ref, hn_ref, acc_scr, act_a, act_b) = refs
    i = pl.program_id(0)
    k = pl.program_id(1)
    nk = pl.num_programs(1) - 1
    w = GRID_W
    nrow = tm // w
    blk0 = i * nrow
    nb_ctx, nb_tot = n_ctx // w, t_total // w
    sub = lax.broadcasted_iota(jnp.int32, (SUBLANES, LANES), 0)
    sh_i = lax.broadcasted_iota(jnp.int32, (2 * w, w), 0)
    sh_j = lax.broadcasted_iota(jnp.int32, (2 * w, w), 1)
    shift2 = jnp.where(jnp.where(sh_i < w, sh_i - 1, sh_i - w + 1) == sh_j, 1.0, 0.0).astype(BF16)

    def neighbours(g, n):
        gl = pltpu.roll(g, 1, axis=0)
        gr = pltpu.roll(g, n - 1, axis=0)
        gl = jnp.concatenate([jnp.where(sub == 0, 0.0, gl[:SUBLANES]), gl[SUBLANES:]], axis=0)
        gr = jnp.concatenate([gr[:n - SUBLANES], jnp.where(sub == SUBLANES - 1, 0.0, gr[n - SUBLANES:])], axis=0)
        return gl, gr

    def step(dst, src):
        d_out = acc_scr.shape[1]
        row_halves = 2
        hm = tm // row_halves
        n_mm = row_halves * d_out // MXU_N
        n_pieces = (tk // LANES) * (nrow + 2)

        def matmul_chunk(n):
            cs = slice((n // row_halves) * MXU_N, (n // row_halves + 1) * MXU_N)
            rs = slice((n % row_halves) * hm, (n % row_halves + 1) * hm)
            acc_scr[rs, cs] += jnp.dot(src[rs, :], wo_ref[:, cs], preferred_element_type=F32)

        mm_at = {(n * n_pieces) // n_mm: n for n in range(n_mm)}
        assert len(mm_at) == n_mm
        piece = 0
        for lb in range(tk // LANES):
            ls = slice(lb * LANES, (lb + 1) * LANES)
            dwv = dw_ref[:, ls]
            bias = db_ref[:, ls]
            part = [None] * nrow
            for r in range(-1, nrow + 1):
                if piece in mm_at:
                    matmul_chunk(mm_at[piece])
                piece += 1
                if r == -1:
                    g = gp_ref[:, ls]
                elif r == nrow:
                    g = gn_ref[:, ls]
                else:
                    g = gm_ref[r * w:(r + 1) * w, ls]
                shifted = jnp.dot(shift2, g, preferred_element_type=F32)
                gl, gr = shifted[:w], shifted[w:]
                g = g.astype(F32)
                for kh, ro in ((0, r + 1), (1, r), (2, r - 1)):
                    if not 0 <= ro < nrow:
                        continue
                    wv = dwv[3 * kh:3 * kh + 3]
                    if kh == 0:
                        wv = wv * jnp.where(blk0 + ro >= nb_ctx + 1, 1.0, 0.0)
                    if kh == 2:
                        ok = jnp.logical_and(blk0 + ro >= nb_ctx, blk0 + ro < nb_tot - 1)
                        wv = wv * jnp.where(ok, 1.0, 0.0)
                    c = gl * wv[0:1] + g * wv[1:2] + gr * wv[2:3]
                    part[ro] = c if part[ro] is None else part[ro] + c
                ro = r - 1
                if 0 <= ro < nrow:
                    rs = slice(ro * w, (ro + 1) * w)
                    dst[rs, ls] = _glu_act(part[ro], bias, u_ref[rs, ls])
                    part[ro] = None

        @pl.when(i == 0)
        def _():
            for lb in range(tk // LANES):
                ls = slice(lb * LANES, (lb + 1) * LANES)
                g = gm_ref[0:n_ctx, ls].astype(F32)
                gl, gr = neighbours(g, n_ctx)
                conv = gl * dw_ref[3:4, ls] + g * dw_ref[4:5, ls] + gr * dw_ref[5:6, ls]
                dst[0:n_ctx, ls] = _glu_act(conv, db_ref[:, ls], u_ref[0:n_ctx, ls])

    @pl.when(k == 0)
    def _():
        acc_scr[...] = jnp.zeros_like(acc_scr)
        act_b[...] = jnp.zeros_like(act_b)

    @pl.when(k % 2 == 0)
    def _():
        step(act_a, act_b)

    @pl.when(k % 2 == 1)
    def _():
        step(act_b, act_a)

    @pl.when(k == nk)
    def _():
        rb = 2 * LANES
        for r0 in range(0, tm, rb):
            rs = slice(r0, r0 + rb)
            rws = i * tm + r0 + lax.broadcasted_iota(jnp.int32, (rb, 1), 0)
            xn = x_ref[rs, :] + _mod_rows(mod_ref[0], 5, rws, n_ctx) * acc_scr[rs, :]
            if final_norm:
                xn = xn * lax.rsqrt(jnp.mean(xn * xn, axis=-1, keepdims=True) + EPS) * fg_ref[...]
            else:
                nmod = nmod_ref[0]
                hn = _norm_mod(xn, fg_ref[...], _mod_rows(nmod, 0, rws, n_ctx), _mod_rows(nmod, 1, rws, n_ctx))
                hn_ref[rs, :] = hn.astype(BF16)
            o_ref[rs, :] = xn


def _ffn_out_call(gate, up, dw9, dwb, wo, x, mod, layer, norm_g, n_ctx, final_norm):
    t, d = x.shape
    fp = gate.shape[1]
    tm, tk, w = ROW_TILE, 512, GRID_W
    assert n_ctx <= tm and n_ctx % w == 0 and tm % (2 * LANES) == 0
    rpt = tm // w
    n_rows = t // w
    spec_x = pl.BlockSpec((tm, d), lambda i, k: (i, 0))
    extra_in, extra_args = [], []
    out_specs, out_shape = spec_x, jax.ShapeDtypeStruct((t, d), F32)
    if not final_norm:
        extra_in = [pl.BlockSpec((1, SUBLANES, 6 * d), lambda i, k: (layer + 1, 0, 0))]
        extra_args = [mod]
        out_specs = [spec_x, spec_x]
        out_shape = [out_shape, jax.ShapeDtypeStruct((t, d), BF16)]
    nk = fp // tk
    assert nk % 2 == 1
    kc = lambda k: jnp.minimum(k, nk - 1)
    return pl.pallas_call(
        functools.partial(_ffn_out_kernel, tm=tm, tk=tk, n_ctx=n_ctx, t_total=t, final_norm=final_norm),
        grid=(t // tm, nk + 1),
        in_specs=[pl.BlockSpec((tm, tk), lambda i, k: (i, kc(k))),
                  pl.BlockSpec((w, tk), lambda i, k: (jnp.maximum(i * rpt - 1, 0), kc(k))),
                  pl.BlockSpec((w, tk), lambda i, k: (jnp.minimum((i + 1) * rpt, n_rows - 1), kc(k))),
                  pl.BlockSpec((tm, tk), lambda i, k: (i, kc(k))),
                  pl.BlockSpec((9, tk), lambda i, k: (0, kc(k))),
                  pl.BlockSpec((1, tk), lambda i, k: (0, kc(k))),
                  pl.BlockSpec((tk, d), lambda i, k: (jnp.maximum(k - 1, 0), 0)),
                  spec_x,
                  pl.BlockSpec((1, SUBLANES, 6 * d), lambda i, k: (layer, 0, 0)),
                  pl.BlockSpec((1, d), lambda i, k: (0, 0))] + extra_in,
        out_specs=out_specs,
        out_shape=out_shape,
        scratch_shapes=[pltpu.VMEM((tm, d), F32), pltpu.VMEM((tm, tk), BF16), pltpu.VMEM((tm, tk), BF16)],
        compiler_params=_cparams(2),
        name="ffn_conv_glu_out",
    )(gate, gate, gate, up, dw9, dwb, wo, x, mod, norm_g.reshape(1, d), *extra_args)


def _retention_kernel(*refs, rev, final, c):
    if final:
        q_ref, k_ref, v_ref, lg_ref, gate_ref, ob_ref, o_ref, r_scr, di_scr, dq_scr, dk_scr, dc_scr = refs
    else:
        q_ref, k_ref, v_ref, lg_ref, o_ref, r_scr, di_scr, dq_scr, dk_scr, dc_scr = refs
    d = 1 if rev else 0
    dk = RET_DK

    @pl.when(pl.program_id(0) == 0)
    def _():
        r_scr[...] = jnp.zeros_like(r_scr)
        i = lax.broadcasted_iota(jnp.int32, (c, c), 0)
        j = lax.broadcasted_iota(jnp.int32, (c, c), 1)
        diff = ((j - i) if rev else (i - j)).astype(F32)
        row = lax.broadcasted_iota(jnp.int32, (c, dk), 0)
        pos = ((c - 1 - row) if rev else row).astype(F32)
        for h in range(RET_HEADS):
            lg = _log_sigmoid(lg_ref[d:d + 1, h:h + 1])
            di_scr[h] = jnp.where(diff >= 0, jnp.exp(lg * jnp.maximum(diff, 0.0)), 0.0)
            dq_scr[h] = jnp.exp(lg * (pos + 1.0))
            dk_scr[h] = jnp.exp(lg * (c - 1.0 - pos)) * (dk ** -0.5)
            dc_scr[h] = jnp.exp(jnp.broadcast_to(lg, (1, dk)) * c)

    for h in range(RET_HEADS):
        sl = slice(h * dk, (h + 1) * dk)
        qh = q_ref[:, sl].astype(F32)
        kh = k_ref[:, sl].astype(F32)
        vh = v_ref[:, sl].astype(BF16)
        r_state = r_scr[h]
        scores = _bdot_nt(qh, kh) * (di_scr[h] * (dk ** -0.5))
        out = _bdot(scores, vh) + _bdot(qh * dq_scr[h], r_state)
        r_scr[h] = dc_scr[h] * r_state + _bdot_tn(kh * dk_scr[h], vh)
        if final:
            tot = out + ob_ref[:, sl]
            y = tot * lax.rsqrt(jnp.mean(tot * tot, axis=-1, keepdims=True) + EPS) * _silu(gate_ref[:, sl].astype(F32))
            o_ref[:, sl] = y.astype(o_ref.dtype)
        else:
            o_ref[:, sl] = out


def _retention_call(p, logit, other, rev, n_ctx):
    t = p.shape[0]
    c = RET_CHUNK
    n, nc = t // c, n_ctx // c
    width = RET_HEADS * RET_DK
    final = other is not None
    cm = lambda s: _chunk_index(s, rev, nc, n)
    in_specs = [pl.BlockSpec((c, width), lambda s: (cm(s), 0)),
                pl.BlockSpec((c, width), lambda s: (cm(s), 1)),
                pl.BlockSpec((c, width), lambda s: (cm(s), 2)),
                pl.BlockSpec((2, RET_HEADS), lambda s: (0, 0))]
    args = [p, p, p, logit]
    if final:
        in_specs += [pl.BlockSpec((c, width), lambda s: (cm(s), 3)), pl.BlockSpec((c, width), lambda s: (cm(s), 0))]
        args += [p, other]
    return pl.pallas_call(
        functools.partial(_retention_kernel, rev=rev, final=final, c=c),
        grid=(n,),
        in_specs=in_specs,
        out_specs=pl.BlockSpec((c, width), lambda s: (cm(s), 0)),
        out_shape=jax.ShapeDtypeStruct((t, width), BF16 if final else F32),
        scratch_shapes=[pltpu.VMEM((RET_HEADS, RET_DK, RET_DK), F32),
                        pltpu.VMEM((RET_HEADS, c, c), F32),
                        pltpu.VMEM((RET_HEADS, c, RET_DK), F32),
                        pltpu.VMEM((RET_HEADS, c, RET_DK), F32),
                        pltpu.VMEM((RET_HEADS, 1, RET_DK), F32)],
        compiler_params=_cparams(1),
        name="retention_bwd" if rev else "retention_fwd_merge",
    )(*args)


def _rwkv_kernel(*refs, rev, final, c, nsub, nc):
    if final:
        (r_ref, k_ref, v_ref, wd_ref, ad_ref, mu_ref, mus_ref, vec_ref, wup_ref, aup_ref,
         gl_ref, gup_ref, yb_ref, bb_ref, o_ref, st_scr, carry_scr) = refs
    else:
        (r_ref, k_ref, v_ref, wd_ref, ad_ref, mu_ref, mus_ref, vec_ref, wup_ref, aup_ref,
         y_ref, bonus_ref, st_scr, carry_scr) = refs
    d = 1 if rev else 0
    s = pl.program_id(0)
    n = RWKV_N
    wdt = RWKV_W

    @pl.when(s == 0)
    def _():
        st_scr[...] = jnp.zeros_like(st_scr)

    @pl.when(jnp.logical_or(s == 0, s == nc))
    def _():
        carry_scr[...] = jnp.zeros_like(carry_scr)

    rows = nsub * c
    row = lax.broadcasted_iota(jnp.int32, (rows, 1), 0)
    edge = (row == rows - 1) if rev else (row == 0)
    last = rows - 1 if not rev else 0
    keep = c - 1 if not rev else 0

    def shifted(x, lo):
        width = x.shape[1]
        prev = pltpu.roll(x, (rows - 1) if rev else 1, axis=0)
        prev = jnp.where(edge, carry_scr[0:1, lo:lo + width], prev)
        carry_scr[0:1, lo:lo + width] = x[last:last + 1, :]
        return prev

    def mix(x, lo, mu):
        prev = shifted(x, lo)
        return x + (prev - x) * mu

    r = mix(r_ref[...].astype(F32), 0, mu_ref[0:1, 0:wdt])
    k = mix(k_ref[...].astype(F32), wdt, mu_ref[0:1, wdt:2 * wdt])
    v = mix(v_ref[...].astype(F32), 2 * wdt, mu_ref[0:1, 2 * wdt:3 * wdt])
    wd = mix(wd_ref[...].astype(F32), 3 * wdt, mus_ref[0:1, :])[:, d * RWKV_LORA:(d + 1) * RWKV_LORA]
    ad = mix(ad_ref[...].astype(F32), 3 * wdt + LANES, mus_ref[1:2, :])[:, d * RWKV_LORA:(d + 1) * RWKV_LORA]

    w0, a0 = vec_ref[0:1, :], vec_ref[1:2, :]
    k_k, k_a, r_k = vec_ref[2:3, :], vec_ref[3:4, :], vec_ref[4:5, :]
    w_log = -_softplus(-(w0 + _bdot(jnp.tanh(wd), wup_ref[...]))) - 0.5
    lw = -jnp.exp(w_log)
    a = _sigmoid(a0 + _bdot(ad, aup_ref[...]))
    kk = k * k_k
    kk = kk * lax.rsqrt(_seg_sum(kk * kk, n) + L2_EPS)
    k = k * (1.0 + (a - 1.0) * k_a)
    b = kk * a

    ri = lax.broadcasted_iota(jnp.int32, (rows, rows), 0)
    ci = lax.broadcasted_iota(jnp.int32, (rows, rows), 1)
    ri_in, ci_in = jnp.bitwise_and(ri, c - 1), jnp.bitwise_and(ci, c - 1)
    before = (ci_in >= ri_in) if rev else (ci_in <= ri_in)
    cmask = jnp.logical_and(jnp.right_shift(ri, 6) == jnp.right_shift(ci, 6), before)
    cum = _dot_mask_lhs(jnp.where(cmask, 1.0, 0.0).astype(BF16), lw)
    e_pos = jnp.exp(cum)
    e_neg = jnp.exp(-cum)
    r_t = r * e_pos
    a_t = -kk * jnp.exp(cum - lw)
    k_t = k * e_neg
    b_t = b * e_neg
    assert c == n
    pw = 2 * n
    incl2, strict2, eye2, lo_c = _pair_masks(c, rev)
    lo_2c = lax.broadcasted_iota(jnp.int32, (2 * c, pw), 1) < n
    bdiag = _bdiag

    adiag = functools.partial(_bdiag, anti=True)

    def pdot_inv(a2, b2):
        return _dot_inv(a2, bdiag(b2))

    n_pairs = RWKV_HEADS // 2
    subs = list(range(nsub))
    inst = [(u, p) for u in subs for p in range(n_pairs)]
    sl = {(u, p): (slice(u * c, (u + 1) * c), slice(p * pw, (p + 1) * pw)) for u, p in inst}
    ar = [jnp.concatenate([a_t[sl[x]], r_t[sl[x]]], axis=0) for x in inst]
    s_e = [_bdot_nt(jnp.where(lo_2c, y, 0.0), jnp.concatenate([b_t[sl[x]], k_t[sl[x]]], axis=0))
           for y, x in zip(ar, inst)]
    s_o = [_bdot_nt(jnp.where(lo_2c, 0.0, y), jnp.concatenate([k_t[sl[x]], b_t[sl[x]]], axis=0))
           for y, x in zip(ar, inst)]
    a_ab = [jnp.where(strict2, jnp.where(lo_c, e[:c], o[:c]), 0.0) for e, o in zip(s_e, s_o)]
    a_rb = [jnp.where(incl2, jnp.where(lo_c, e[c:], o[c:]), 0.0) for e, o in zip(s_e, s_o)]
    a_kk = [jnp.concatenate([jnp.where(strict2, jnp.where(lo_c, o[:c], e[:c]), 0.0),
                             jnp.where(incl2, jnp.where(lo_c, o[c:], e[c:]), 0.0)], axis=0)
            for e, o in zip(s_e, s_o)]
    tms = _neumann_inverse_packed(a_ab, eye2, c)
    bonus = _seg_sum(r * k * r_k, n) * v
    cum_end = jnp.concatenate([jnp.broadcast_to(cum[u * c + keep:u * c + keep + 1, :], (c, wdt)) for u in range(nsub)],
                              axis=0)
    w_end = jnp.exp(cum_end)
    kw = k_t * w_end
    bw = b_t * w_end
    av = [_bdot(a, adiag(v[sl[x]])) for a, x in zip(a_kk, inst)]
    pp = [_dot_inv(t, jnp.concatenate([bdiag(a_t[sl[x]]), bdiag(y[:c])], axis=1))
          for t, x, y in zip(tms, inst, av)]
    rq_y0 = [_bdot(a, jnp.concatenate([bdiag(y[:, :pw]), bdiag(y[:, pw:])], axis=1)) for a, y in zip(a_rb, pp)]
    m_g = [_bdot_tn(bw[sl[x]], y) for x, y in zip(inst, pp)]
    kv = [_bdot_tn(kw[sl[x]], v[sl[x]]) for x in inst]
    idx = {x: i for i, x in enumerate(inst)}
    y_rows = {}
    for u in (subs[::-1] if rev else subs):
        so = []
        for p in range(n_pairs):
            i = idx[(u, p)]
            rq = r_t[sl[(u, p)]] + rq_y0[i][:, :pw]
            w_row = w_end[u * c:u * c + 1, p * pw:(p + 1) * pw]
            m = (jnp.where(lo_c, m_g[i][:n, :pw], m_g[i][n:, :pw])
                 + jnp.where(eye2, jnp.broadcast_to(w_row, (n, pw)), 0.0))
            so.append(_dot_state(jnp.concatenate([rq, m], axis=0), bdiag(st_scr[p])))
        ys = []
        for p in range(n_pairs):
            i = idx[(u, p)]
            g = jnp.where(lo_c, m_g[i][:n, pw:], m_g[i][n:, pw:]) + jnp.where(lo_c, kv[i][:n], kv[i][n:])
            st_scr[p] = so[p][c:] + g
            ys.append(so[p][:c] + (rq_y0[i][:, pw:] + av[i][c:]))
        y_rows[u] = jnp.concatenate(ys, axis=1)
    y = jnp.concatenate([y_rows[u] for u in subs], axis=0)

    if final:
        ysum = y + yb_ref[...]
        mu_h = _seg_sum(ysum, n) * (1.0 / n)
        yc = ysum - mu_h
        var = _seg_sum(yc * yc, n) * (1.0 / n)
        ln_g, ln_b = vec_ref[5:6, :], vec_ref[6:7, :]
        yn = yc * lax.rsqrt(var + GN_EPS) * ln_g + ln_b + bonus + bb_ref[...]
        gate = _bdot(_sigmoid(gl_ref[...].astype(F32)), gup_ref[...])
        o_ref[...] = (yn * gate).astype(o_ref.dtype)
    else:
        y_ref[...] = y
        bonus_ref[...] = bonus


def _rwkv_call(p, prm, other, rev, n_ctx):
    t = p.shape[0]
    c = RWKV_SUB * RWKV_CHUNK
    n, nc = t // c, n_ctx // c
    wdt = RWKV_W
    d = 1 if rev else 0
    final = other is not None
    cm = lambda s: _chunk_index(s, rev, nc, n)
    base = 4096 // wdt
    full = lambda shape: pl.BlockSpec(shape, lambda s: (0,) * len(shape))
    in_specs = [pl.BlockSpec((c, wdt), lambda s: (cm(s), base)),
                pl.BlockSpec((c, wdt), lambda s: (cm(s), base + 1)),
                pl.BlockSpec((c, wdt), lambda s: (cm(s), base + 2)),
                pl.BlockSpec((c, LANES), lambda s: (cm(s), 7168 // LANES)),
                pl.BlockSpec((c, LANES), lambda s: (cm(s), 7168 // LANES + 1)),
                full((1, 3 * wdt)), full((2, LANES)), full((SUBLANES, wdt)),
                full((RWKV_LORA, wdt)), full((RWKV_LORA, wdt))]
    args = [p, p, p, p, p, prm["mu"][d], prm["mu_small"][d], prm["vecs"][d], prm["w_up"][d], prm["a_up"][d]]
    if final:
        in_specs += [pl.BlockSpec((c, 2 * LANES), lambda s: (cm(s), 7424 // (2 * LANES))),
                     full((2 * LANES, wdt)),
                     pl.BlockSpec((c, wdt), lambda s: (cm(s), 0)),
                     pl.BlockSpec((c, wdt), lambda s: (cm(s), 0))]
        args += [p, prm["g_up"], other[0], other[1]]
        out_specs = pl.BlockSpec((c, wdt), lambda s: (cm(s), 0))
        out_shape = jax.ShapeDtypeStruct((t, wdt), BF16)
    else:
        out_specs = [pl.BlockSpec((c, wdt), lambda s: (cm(s), 0))] * 2
        out_shape = [jax.ShapeDtypeStruct((t, wdt), F32)] * 2
    return pl.pallas_call(
        functools.partial(_rwkv_kernel, rev=rev, final=final, c=RWKV_CHUNK, nsub=RWKV_SUB, nc=nc),
        grid=(n,),
        in_specs=in_specs,
        out_specs=out_specs,
        out_shape=out_shape,
        scratch_shapes=[pltpu.VMEM((RWKV_HEADS // 2, RWKV_N, 2 * RWKV_N), F32),
                        pltpu.VMEM((SUBLANES, 3 * wdt + 2 * LANES), F32)],
        compiler_params=_cparams(1),
        name="rwkv7_bwd" if rev else "rwkv7_fwd_merge",
    )(*args)


def _mlstm_kernel(*refs, rev, final, c):
    if final:
        q_ref, k_ref, v_ref, sm_ref, bias_ref, og_ref, ng_ref, hb_ref, o_ref, ct_scr, m_scr = refs
    else:
        q_ref, k_ref, v_ref, sm_ref, bias_ref, o_ref, ct_scr, m_scr = refs
    d = 1 if rev else 0
    dk, dv = MLSTM_DK, MLSTM_DV
    nh = MLSTM_HEADS

    @pl.when(pl.program_id(0) == 0)
    def _():
        ct_scr[...] = jnp.zeros_like(ct_scr)
        m_scr[...] = jnp.zeros_like(m_scr)

    incl, _ = _order_masks(c, rev)
    keep = 0 if rev else c - 1
    sm = sm_ref[...].astype(F32)
    li = sm + bias_ref[0:1, :]
    lf = _log_sigmoid(sm + bias_ref[1:2, :])
    bcum = _dot_mask_lhs(jnp.where(incl, 1.0, 0.0).astype(BF16), lf)
    bcum_t = bcum.T
    li_t = li.T
    ones_col = jnp.where(lax.broadcasted_iota(jnp.int32, (c, LANES), 1) == 0, 1.0, 0.0).astype(BF16)

    heads = range(nh)
    cis = [d * 2 * nh + h for h in heads]
    b_col = [bcum[:, ci + nh:ci + nh + 1] for ci in cis]
    i_col = [li[:, ci:ci + 1] for ci in cis]
    m_prev = [m_scr[h:h + 1, 0:1] for h in heads]
    qs = [q_ref[:, h * dk:(h + 1) * dk].astype(F32) * (dk ** -0.5) for h in heads]
    ks = [k_ref[:, h * dk:(h + 1) * dk].astype(F32) for h in heads]
    v_aug = [jnp.concatenate([v_ref[:, h * dv:(h + 1) * dv].astype(BF16), ones_col], axis=1) for h in heads]
    qk_raw = [_bdot_nt(q, k) for q, k in zip(qs, ks)]
    q_ct = [_bdot(q, ct_scr[h]) for q, h in zip(qs, heads)]
    d_log = [jnp.where(incl, bc - bcum_t[ci + nh:ci + nh + 1, :] + li_t[ci:ci + 1, :], -jnp.inf)
             for bc, ci in zip(b_col, cis)]
    inter = [bc + mp for bc, mp in zip(b_col, m_prev)]
    m_t = [jnp.maximum(jnp.max(dl, axis=-1, keepdims=True), it) for dl, it in zip(d_log, inter)]
    qk = [x * jnp.exp(dl - mt) for x, dl, mt in zip(qk_raw, d_log, m_t)]
    num_aug = [_bdot(x, va) + jnp.exp(it - mt) * qc for x, va, it, mt, qc in zip(qk, v_aug, inter, m_t, q_ct)]
    b_end = [bc[keep:keep + 1, :] for bc in b_col]
    w_log = [be - bc + ic for be, bc, ic in zip(b_end, b_col, i_col)]
    m_new = [jnp.maximum(be + mp, jnp.max(wl, axis=0, keepdims=True)) for be, mp, wl in zip(b_end, m_prev, w_log)]
    kv = [_bdot_tn(k * jnp.exp(wl - mn), va) for k, wl, mn, va in zip(ks, w_log, m_new, v_aug)]
    for h in heads:
        ct_scr[h] = jnp.exp(b_end[h] + m_prev[h] - m_new[h]) * ct_scr[h] + kv[h]
        m_scr[h:h + 1, :] = jnp.broadcast_to(m_new[h], (1, LANES))
        den = num_aug[h][:, dv:dv + 1]
        hout = num_aug[h][:, :dv] / jnp.maximum(jnp.abs(den), jnp.exp(-m_t[h]))
        sl = slice(h * dv, (h + 1) * dv)
        if final:
            tot = hout + hb_ref[:, sl]
            y = tot * lax.rsqrt(jnp.mean(tot * tot, axis=-1, keepdims=True) + EPS)
            o_ref[:, sl] = (y * ng_ref[0:1, sl] * _sigmoid(og_ref[:, sl].astype(F32))).astype(o_ref.dtype)
        else:
            o_ref[:, sl] = hout


def _mlstm_call(p, bias_rows, norm_g, other, rev, n_ctx):
    t = p.shape[0]
    c = MLSTM_CHUNK
    n, nc = t // c, n_ctx // c
    dk, dv, nh = MLSTM_DK, MLSTM_DV, MLSTM_HEADS
    final = other is not None
    cm = lambda s: _chunk_index(s, rev, nc, n)
    full = lambda shape: pl.BlockSpec(shape, lambda s: (0,) * len(shape))
    in_specs = [pl.BlockSpec((c, nh * dk), lambda s: (cm(s), 0)),
                pl.BlockSpec((c, nh * dk), lambda s: (cm(s), 1)),
                pl.BlockSpec((c, nh * dv), lambda s: (cm(s), 1)),
                pl.BlockSpec((c, LANES), lambda s: (cm(s), 7168 // LANES)),
                full((2, LANES))]
    args = [p, p, p, p, bias_rows]
    if final:
        in_specs += [pl.BlockSpec((c, nh * dv), lambda s: (cm(s), 2)), full((1, nh * dv)),
                     pl.BlockSpec((c, nh * dv), lambda s: (cm(s), 0))]
        args += [p, norm_g.reshape(1, nh * dv), other]
    return pl.pallas_call(
        functools.partial(_mlstm_kernel, rev=rev, final=final, c=c),
        grid=(n,),
        in_specs=in_specs,
        out_specs=pl.BlockSpec((c, nh * dv), lambda s: (cm(s), 0)),
        out_shape=jax.ShapeDtypeStruct((t, nh * dv), BF16 if final else F32),
        scratch_shapes=[pltpu.VMEM((nh, dk, dv + LANES), F32), pltpu.VMEM((SUBLANES, LANES), F32)],
        compiler_params=_cparams(1),
        name="mlstm_bwd" if rev else "mlstm_fwd_merge",
    )(*args)


def _gdn_prep_kernel(xm_ref, xp_ref, xn_ref, cw_ref, sm_ref, gp_ref, q_ref, k_ref, v_ref, gb_ref, *, tm, n_ctx, t_total):
    i = pl.program_id(0)
    x = xm_ref[...].astype(F32)
    rows = i * tm + lax.broadcasted_iota(jnp.int32, (tm, 1), 0)
    first = rows == i * tm
    last = rows == i * tm + tm - 1
    prev = jnp.where(first, xp_ref[HALO_ROWS - 1:HALO_ROWS, :].astype(F32), pltpu.roll(x, 1, axis=0))
    nxt = jnp.where(last, xn_ref[0:1, :].astype(F32), pltpu.roll(x, tm - 1, axis=0))
    prev = jnp.where(jnp.logical_or(rows == 0, rows == n_ctx), 0.0, prev)
    nxt = jnp.where(jnp.logical_or(rows == n_ctx - 1, rows == t_total - 1), 0.0, nxt)
    y = _silu(prev * cw_ref[0:1, :] + x * cw_ref[1:2, :] + nxt * cw_ref[2:3, :])
    w = GDN_HEADS * GDN_DK
    q, k, v = y[:, :w], y[:, w:2 * w], y[:, 2 * w:]
    q_ref[...] = (q * lax.rsqrt(_seg_sum(q * q, GDN_DK) + L2_EPS) * (GDN_DK ** -0.5)).astype(BF16)
    k_ref[...] = (k * lax.rsqrt(_seg_sum(k * k, GDN_DK) + L2_EPS)).astype(BF16)
    v_ref[...] = v.astype(BF16)
    sm = sm_ref[...].astype(F32)
    lane = lax.broadcasted_iota(jnp.int32, sm.shape, 1)
    log_alpha = -jnp.exp(gp_ref[0:1, :]) * _softplus(sm + gp_ref[1:2, :])
    gb_ref[...] = jnp.where(lane < 32, log_alpha, _sigmoid(sm))


def _gdn_prep_call(p, conv_w, gate_params, n_ctx):
    t = p.shape[0]
    tm = 256
    w = GDN_HEADS * GDN_DK
    qkv_blk = 3072 // GDN_QKV
    nb8 = t // HALO_ROWS
    r8 = tm // HALO_ROWS
    spec_o = pl.BlockSpec((tm, w), lambda i: (i, 0))
    return pl.pallas_call(
        functools.partial(_gdn_prep_kernel, tm=tm, n_ctx=n_ctx, t_total=t),
        grid=(t // tm,),
        in_specs=[pl.BlockSpec((tm, GDN_QKV), lambda i: (i, qkv_blk)),
                  pl.BlockSpec((HALO_ROWS, GDN_QKV), lambda i: (jnp.maximum(i * r8 - 1, 0), qkv_blk)),
                  pl.BlockSpec((HALO_ROWS, GDN_QKV), lambda i: (jnp.minimum((i + 1) * r8, nb8 - 1), qkv_blk)),
                  pl.BlockSpec((3, GDN_QKV), lambda i: (0, 0)),
                  pl.BlockSpec((tm, LANES), lambda i: (i, 7168 // LANES)),
                  pl.BlockSpec((2, LANES), lambda i: (0, 0))],
        out_specs=[spec_o, spec_o, spec_o, pl.BlockSpec((tm, LANES), lambda i: (i, 0))],
        out_shape=[jax.ShapeDtypeStruct((t, w), BF16)] * 3 + [jax.ShapeDtypeStruct((t, LANES), F32)],
        compiler_params=_cparams(1),
        name="gdn_conv_norm_gates",
    )(p, p, p, conv_w, p, gate_params)


def _gdn_kernel(*refs, rev, final, c, nsub):
    if final:
        q_ref, k_ref, v_ref, gb_ref, gate_ref, ng_ref, ob_ref, o_ref, st_scr = refs
    else:
        q_ref, k_ref, v_ref, gb_ref, o_ref, st_scr = refs
    d = 1 if rev else 0
    dk = GDN_DK
    nh = GDN_HEADS
    rows_all = nsub * c

    @pl.when(pl.program_id(0) == 0)
    def _():
        st_scr[...] = jnp.zeros_like(st_scr)

    keep = 0 if rev else c - 1
    gb = gb_ref[...]
    ri = lax.broadcasted_iota(jnp.int32, (rows_all, rows_all), 0)
    ci = lax.broadcasted_iota(jnp.int32, (rows_all, rows_all), 1)
    ri_in, ci_in = jnp.bitwise_and(ri, c - 1), jnp.bitwise_and(ci, c - 1)
    before = (ci_in >= ri_in) if rev else (ci_in <= ri_in)
    cmask = jnp.logical_and(jnp.right_shift(ri, 6) == jnp.right_shift(ci, 6), before)
    gc = _dot_mask_lhs(jnp.where(cmask, 1.0, 0.0).astype(BF16), gb)
    gc_t = gc.T
    incl2, strict2, eye2, lo_c = _pair_masks(c, rev)
    bdiag = _bdiag
    zeros = jnp.zeros((c, 2 * dk), F32)

    subs = list(range(nsub))
    inst = [(u, h) for u in subs for h in range(nh)]
    rs = {u: slice(u * c, (u + 1) * c) for u in subs}
    sls = [slice(h * dk, (h + 1) * dk) for h in range(nh)]
    cols = [16 + d * nh + h for h in range(nh)]
    g_col = {(u, h): gc[rs[u], cols[h]:cols[h] + 1] for u, h in inst}
    beta = {(u, h): gb[rs[u], cols[h] + 16:cols[h] + 17] for u, h in inst}
    g_end = {x: g_col[x][keep:keep + 1, :] for x in inst}
    e_g = {x: jnp.exp(g_col[x]) for x in inst}
    qs = {(u, h): q_ref[rs[u], sls[h]].astype(F32) for u, h in inst}
    ks = {(u, h): k_ref[rs[u], sls[h]].astype(F32) for u, h in inst}
    vs = {(u, h): v_ref[rs[u], sls[h]].astype(F32) for u, h in inst}

    pairs = [(u, a) for u in subs for a in range(0, nh, 2)]
    kq_e, kq_o, decay2 = [], [], []
    for u, a in pairs:
        e, o = (u, a), (u, a + 1)
        kk2 = jnp.concatenate([ks[e], ks[o]], axis=0)
        kq_e.append(_bdot_nt(jnp.concatenate([ks[e] * beta[e], qs[e]], axis=0), kk2))
        kq_o.append(_bdot_nt(jnp.concatenate([ks[o] * beta[o], qs[o]], axis=0), kk2))
        g_row2 = jnp.concatenate([gc_t[cols[a]:cols[a] + 1, rs[u]], gc_t[cols[a + 1]:cols[a + 1] + 1, rs[u]]], axis=1)
        decay2.append(jnp.exp(jnp.where(incl2, jnp.where(lo_c, g_col[e], g_col[o]) - g_row2, -jnp.inf)))
    l2 = [jnp.where(strict2, -(jnp.where(lo_c, e[:c], o[:c]) * dc), 0.0) for e, o, dc in zip(kq_e, kq_o, decay2)]
    a_qk2 = [jnp.where(lo_c, e[c:], o[c:]) * dc for e, o, dc in zip(kq_e, kq_o, decay2)]
    tinv2 = _neumann_inverse_packed(l2, eye2, c)
    rhs2 = []
    for u, a in pairs:
        e, o = (u, a), (u, a + 1)
        r_e = jnp.concatenate([vs[e] * beta[e], ks[e] * (beta[e] * e_g[e]), zeros], axis=1)
        r_o = jnp.concatenate([zeros, vs[o] * beta[o], ks[o] * (beta[o] * e_g[o])], axis=1)
        rhs2.append(jnp.concatenate([r_e, r_o], axis=0))
    uw2 = [_dot_inv(t, r) for t, r in zip(tinv2, rhs2)]
    op2 = []
    for aq, x in zip(a_qk2, uw2):
        x_bd = jnp.concatenate([jnp.concatenate([x[:, :2 * dk], zeros], axis=1),
                                jnp.concatenate([zeros, x[:, 2 * dk:]], axis=1)], axis=0)
        op2.append(_bdot(aq, x_bd))
    uw, o_part = {}, {}
    for (u, a), x, o2 in zip(pairs, uw2, op2):
        uw[(u, a)], uw[(u, a + 1)] = x[:, :2 * dk], x[:, 2 * dk:]
        o_part[(u, a)], o_part[(u, a + 1)] = o2[:, :2 * dk], o2[:, 2 * dk:]
    s_part = {x: _bdot_tn(ks[x] * jnp.exp(g_end[x] - g_col[x]), uw[x]) for x in inst}

    for u in (subs[::-1] if rev else subs):
        so = [_dot_state(jnp.concatenate([qs[(u, h)] * e_g[(u, h)] - o_part[(u, h)][:, dk:], s_part[(u, h)][:, dk:]], axis=0),
                         st_scr[h]) for h in range(nh)]
        for h in range(nh):
            x = (u, h)
            out = so[h][:c] + o_part[x][:, :dk]
            st_scr[h] = jnp.exp(g_end[x]) * st_scr[h] - so[h][c:] + s_part[x][:, :dk]
            if final:
                tot = out + ob_ref[rs[u], sls[h]]
                y = tot * lax.rsqrt(jnp.mean(tot * tot, axis=-1, keepdims=True) + EPS)
                o_ref[rs[u], sls[h]] = (y * ng_ref[0:1, sls[h]] * _silu(gate_ref[rs[u], sls[h]].astype(F32))).astype(o_ref.dtype)
            else:
                o_ref[rs[u], sls[h]] = out


def _gdn_call(p, q, k, v, gb, norm_g, other, rev, n_ctx):
    t = q.shape[0]
    c = GDN_SUB * GDN_CHUNK
    n, nc = t // c, n_ctx // c
    w = GDN_HEADS * GDN_DK
    final = other is not None
    cm = lambda s: _chunk_index(s, rev, nc, n)
    blk = pl.BlockSpec((c, w), lambda s: (cm(s), 0))
    in_specs = [blk, blk, blk, pl.BlockSpec((c, LANES), lambda s: (cm(s), 0))]
    args = [q, k, v, gb]
    if final:
        in_specs += [pl.BlockSpec((c, w), lambda s: (cm(s), 6144 // w)), pl.BlockSpec((1, w), lambda s: (0, 0)), blk]
        args += [p, norm_g.reshape(1, w), other]
    return pl.pallas_call(
        functools.partial(_gdn_kernel, rev=rev, final=final, c=GDN_CHUNK, nsub=GDN_SUB),
        grid=(n,),
        in_specs=in_specs,
        out_specs=blk,
        out_shape=jax.ShapeDtypeStruct((t, w), BF16 if final else F32),
        scratch_shapes=[pltpu.VMEM((GDN_HEADS, GDN_DK, GDN_DK), F32)],
        compiler_params=_cparams(1),
        name="gdn_bwd" if rev else "gdn_fwd_merge",
    )(*args)


def _pad_cols(w, width):
    return jnp.pad(w, ((0, 0), (0, width - w.shape[1])))


def _lane_row(pieces):
    row = jnp.zeros((LANES,), F32)
    for off, vec in pieces:
        row = row.at[off:off + vec.shape[0]].set(vec.astype(F32))
    return row


def _rwkv_params(e, rwkv_mu, rwkv_w0, rwkv_w_up, rwkv_a0, rwkv_a_up, rwkv_k_k, rwkv_k_a, rwkv_r_k,
                 rwkv_ln_g, rwkv_ln_b, rwkv_g_up):
    wdt = RWKV_W
    mu = rwkv_mu[e]
    zeros64 = jnp.zeros((RWKV_LORA,), F32)
    mu_small, vecs = [], []
    for d in range(2):
        m_wd = mu[d, 3 * wdt:3 * wdt + RWKV_LORA]
        m_ad = mu[d, 3 * wdt + RWKV_LORA:]
        lo = [m_wd, zeros64] if d == 0 else [zeros64, m_wd]
        la = [m_ad, zeros64] if d == 0 else [zeros64, m_ad]
        mu_small.append(jnp.stack([jnp.concatenate(lo), jnp.concatenate(la)]))
        vecs.append(jnp.stack([rwkv_w0[e, d], rwkv_a0[e, d], rwkv_k_k[e], rwkv_k_a[e], rwkv_r_k[e].reshape(wdt),
                               rwkv_ln_g[e], rwkv_ln_b[e], jnp.zeros((wdt,), F32)]))
    return {
        "mu": mu[:, None, :3 * wdt],
        "mu_small": jnp.stack(mu_small),
        "vecs": jnp.stack(vecs),
        "w_up": rwkv_w_up[e].astype(BF16),
        "a_up": rwkv_a_up[e].astype(BF16),
        "g_up": jnp.pad(rwkv_g_up[e], ((0, 2 * LANES - RWKV_GATE_LORA), (0, 0))).astype(BF16),
    }


def _even_mixer(p, e, n_ctx, ret_decay_logit, rwkv_prm):
    logit = ret_decay_logit[e].astype(F32)
    ret_b = _retention_call(p, logit, None, True, n_ctx)
    y_ret = _retention_call(p, logit, ret_b, False, n_ctx)
    rw_b = _rwkv_call(p, rwkv_prm, None, True, n_ctx)
    y_rwkv = _rwkv_call(p, rwkv_prm, rw_b, False, n_ctx)
    return y_ret, y_rwkv


def _odd_mixer(p, o, n_ctx, mlstm_gate_b, mlstm_norm_g, gdn_conv, gdn_a_log, gdn_dt_bias, gdn_norm_g):
    gate_b = mlstm_gate_b[o]
    nh = MLSTM_HEADS
    bias_i = _lane_row([(d * 2 * nh, gate_b[d, 0]) for d in range(2)])
    bias_f = _lane_row([(d * 2 * nh + nh, gate_b[d, 1]) for d in range(2)])
    bias_rows = jnp.stack([bias_i, bias_f])
    ml_b = _mlstm_call(p, bias_rows, mlstm_norm_g[o], None, True, n_ctx)
    y_ml = _mlstm_call(p, bias_rows, mlstm_norm_g[o], ml_b, False, n_ctx)

    gate_params = jnp.stack([_lane_row([(16, gdn_a_log[o].reshape(-1))]), _lane_row([(16, gdn_dt_bias[o].reshape(-1))])])
    q, k, v, gb = _gdn_prep_call(p, gdn_conv[o], gate_params, n_ctx)
    gd_b = _gdn_call(p, q, k, v, gb, gdn_norm_g[o], None, True, n_ctx)
    y_gd = _gdn_call(p, q, k, v, gb, gdn_norm_g[o], gd_b, False, n_ctx)
    return y_ml, y_gd


def kernel(x, c, ctx, c_ctx, ada_w, ada_b, norm1_g, norm2_g, mix_w_out, ffn_w_in, ffn_dw, ffn_dw_b, ffn_w_out, final_norm_g, ev_w_in, ret_decay_logit, rwkv_mu, rwkv_w0, rwkv_w_up, rwkv_a0, rwkv_a_up, rwkv_k_k, rwkv_k_a, rwkv_r_k, rwkv_ln_g, rwkv_ln_b, rwkv_g_up, od_w_in, mlstm_gate_b, mlstm_norm_g, gdn_conv, gdn_a_log, gdn_dt_bias, gdn_norm_g):
    assert x.shape[0] == 1 and ctx.shape[0] == 1
    n_ctx = ctx.shape[1]
    n_lat = x.shape[1]
    depth = ada_w.shape[0]
    cvecs = jnp.stack([c_ctx, c[0]], axis=1)
    mod = _ada_call(cvecs, ada_w, ada_b)
    xt, h1 = _assemble_call(ctx, x, norm1_g[0], mod)

    for l in range(depth):
        last = l == depth - 1
        half = D_MODEL // 2
        if l % 2 == 0:
            e = l // 2
            p = _proj_call(h1, ev_w_in[e], F_PAD)
            prm = _rwkv_params(e, rwkv_mu, rwkv_w0, rwkv_w_up, rwkv_a0, rwkv_a_up, rwkv_k_k, rwkv_k_a, rwkv_r_k,
                               rwkv_ln_g, rwkv_ln_b, rwkv_g_up)
            ya, yb = _even_mixer(p, e, n_ctx, ret_decay_logit, prm)
        else:
            o = l // 2
            p = _proj_odd_call(h1, od_w_in, o)
            ya, yb = _odd_mixer(p, o, n_ctx, mlstm_gate_b, mlstm_norm_g, gdn_conv, gdn_a_log, gdn_dt_bias, gdn_norm_g)
        w_mix = mix_w_out[l].astype(BF16)
        xt, h2 = _outproj_call(ya, yb, w_mix[:half], w_mix[half:], xt, norm2_g[l], mod, l, n_ctx)
        gate, up, wo = _ffn_in_call(h2, ffn_w_in, ffn_w_out, l)
        dw9 = _pad_cols(ffn_dw[l].reshape(9, D_FF), FF_PAD)
        dwb = _pad_cols(ffn_dw_b[l].reshape(1, D_FF), FF_PAD)
        if last:
            xt = _ffn_out_call(gate, up, dw9, dwb, wo, xt, mod, l, final_norm_g, n_ctx, True)
        else:
            xt, h1 = _ffn_out_call(gate, up, dw9, dwb, wo, xt, mod, l, norm1_g[l + 1], n_ctx, False)
    return xt[n_ctx:][None]
```
